```python
import math
import jax
import jax.numpy as jnp
from jax import lax
import numpy as np

D_MODEL = 4096
BATCH = 1
SEQ = 16384
DEPTH = 1

DN_HEADS = 16
DN_HEAD_DIM = 128
DN_WIDTH = DN_HEADS * DN_HEAD_DIM
DN_CONV = 4
DN_CHUNK = 64
ATT_GROUPS = ((128, 1), (512, 4), (2048, 16))
ATT_HEADS_PER_GROUP = 8
ATT_HEAD_DIM = 128
ATT_WIDTH = len(ATT_GROUPS) * ATT_HEADS_PER_GROUP * ATT_HEAD_DIM
ATT_OUT_WIDTH = ATT_HEADS_PER_GROUP * ATT_HEAD_DIM
ATT_BLOCK = 128
ROPE_THETA = 10000.0
N_EXPERTS = 128
TOP_K = 8
N_GROUPS = 8
TOPK_GROUPS = 4
EXPERT_FF = 384
ROUTED_SCALE = 2.5
MOE_ROW_BLOCK = 512
DEEPNORM_ALPHA = (2.0 * DEPTH) ** 0.25
DEEPNORM_BETA = (8.0 * DEPTH) ** -0.25
LN_EPS = 1e-5
RMS_EPS = 1e-6
L2_EPS = 1e-6

IN_SIZES = (DN_WIDTH, DN_WIDTH, DN_WIDTH, DN_WIDTH, DN_HEADS, DN_HEADS,
            ATT_WIDTH, ATT_WIDTH, ATT_WIDTH, D_MODEL, D_MODEL)
IN_COLS = sum(IN_SIZES)

kernel_name = "hybrid_deltanet_dilated_moe_deepnorm"


def layer_norm(x, g, b):
    xf = x.astype(jnp.float32)
    mu = jnp.mean(xf, -1, keepdims=True)
    var = jnp.mean(jnp.square(xf - mu), -1, keepdims=True)
    return ((xf - mu) * lax.rsqrt(var + LN_EPS) * g.astype(jnp.float32) + b.astype(jnp.float32)).astype(x.dtype)


def l2norm(a):
    return a * lax.rsqrt(jnp.sum(a * a, -1, keepdims=True) + L2_EPS)


def rope(x, pos):
    half = x.shape[-1] // 2
    inv_freq = ROPE_THETA ** (-jnp.arange(half, dtype=jnp.float32) / half)
    ang = pos.astype(jnp.float32)[:, None] * inv_freq[None, :]
    cos = jnp.cos(ang)[None, :, None, :]
    sin = jnp.sin(ang)[None, :, None, :]
    x1 = x[..., :half].astype(jnp.float32)
    x2 = x[..., half:].astype(jnp.float32)
    return jnp.concatenate([x1 * cos - x2 * sin, x2 * cos + x1 * sin], -1).astype(x.dtype)


def causal_depthwise_conv(x, w):
    k, c = w.shape
    return lax.conv_general_dilated(
        x, w[:, None, :].astype(x.dtype), window_strides=(1,), padding=((k - 1, 0),),
        dimension_numbers=("NWC", "WIO", "NWC"), feature_group_count=c)


def chunk_gated_delta_rule(q, k, v, g, beta):
    b, t, h, dk = q.shape
    dv = v.shape[-1]
    c = DN_CHUNK
    n = t // c

    def chunks(a):
        return jnp.moveaxis(a.reshape((b, n, c, h) + a.shape[3:]), 3, 1)

    q, k, v, beta = chunks(q), chunks(k), chunks(v), chunks(beta)
    g = jnp.cumsum(chunks(g), axis=-1)
    tril = jnp.tril(jnp.ones((c, c), dtype=bool))
    strict = jnp.tril(jnp.ones((c, c), dtype=bool), -1)
    decay = jnp.exp(jnp.where(tril, g[..., :, None] - g[..., None, :], -jnp.inf))
    kb = k * beta[..., None]
    a_low = jnp.where(strict, jnp.einsum("bhncd,bhnmd->bhncm", kb, k) * decay, 0.0)
    eye = jnp.eye(c, dtype=q.dtype)
    t_mat = lax.linalg.triangular_solve(a_low + eye, jnp.broadcast_to(eye, a_low.shape),
                                        left_side=True, lower=True, unit_diagonal=True)
    u = jnp.einsum("bhncm,bhnmd->bhncd", t_mat, v * beta[..., None])
    w = jnp.einsum("bhncm,bhnmd->bhncd", t_mat, kb * jnp.exp(g)[..., None])
    intra = jnp.einsum("bhncd,bhnmd->bhncm", q, k) * decay

    def step(s, xs):
        q_i, k_i, u_i, w_i, g_i, intra_i = xs
        v_new = u_i - jnp.einsum("bhcd,bhde->bhce", w_i, s)
        o = (jnp.einsum("bhcd,bhde->bhce", q_i * jnp.exp(g_i)[..., None], s)
             + jnp.einsum("bhcm,bhme->bhce", intra_i, v_new))
        g_last = g_i[..., -1:]
        s = (s * jnp.exp(g_last)[..., None]
             + jnp.einsum("bhcd,bhce->bhde", k_i * jnp.exp(g_last - g_i)[..., None], v_new))
        return s, o

    xs = tuple(jnp.moveaxis(a, 2, 0) for a in (q, k, u, w, g, intra))
    s0 = jnp.zeros((b, h, dk, dv), q.dtype)
    _, o = lax.scan(step, s0, xs)
    o = jnp.moveaxis(o, 0, 2)
    return jnp.moveaxis(o, 1, 3).reshape(b, t, h, dv)


def gated_deltanet(q, k, v, z, b_logit, a_logit, conv_w, a_log, dt_bias, norm_w):
    bsz, t, _ = q.shape
    qkv = jax.nn.silu(causal_depthwise_conv(jnp.concatenate([q, k, v], -1), conv_w))
    q, k, v = jnp.split(qkv.astype(jnp.float32), 3, axis=-1)
    shp = (bsz, t, DN_HEADS, DN_HEAD_DIM)
    q = l2norm(q.reshape(shp)) * DN_HEAD_DIM ** -0.5
    k = l2norm(k.reshape(shp))
    v = v.reshape(shp)
    beta = jax.nn.sigmoid(b_logit.astype(jnp.float32))
    g = -jnp.exp(a_log.astype(jnp.float32)) * jax.nn.softplus(a_logit.astype(jnp.float32) + dt_bias.astype(jnp.float32))
    o = chunk_gated_delta_rule(q, k, v, g, beta)
    o = (o * lax.rsqrt(jnp.mean(o * o, -1, keepdims=True) + RMS_EPS) * norm_w.astype(jnp.float32)
         * jax.nn.silu(z.reshape(shp).astype(jnp.float32)))
    return o.reshape(bsz, t, DN_WIDTH).astype(z.dtype)


def dilated_group_attention(q, k, v, dilation, span):
    b, t, h, dh = q.shape
    unit = dilation * ATT_BLOCK
    total = -(-t // unit) * unit
    n = total // dilation
    nb = n // ATT_BLOCK

    def to_sub(a):
        a = jnp.pad(a, ((0, 0), (0, total - t), (0, 0), (0, 0)))
        a = a.reshape(b, n, dilation, h, dh).transpose(0, 2, 3, 1, 4)
        return a.reshape(b, dilation, h, nb, ATT_BLOCK, dh)

    def with_prev(a):
        prev = jnp.pad(a, ((0, 0), (0, 0), (0, 0), (1, 0), (0, 0), (0, 0)))[:, :, :, :-1]
        return jnp.concatenate([prev, a], axis=4)

    qs = to_sub(q)
    kw = with_prev(to_sub(k))
    vw = with_prev(to_sub(v))
    s = jnp.einsum("bdhnqc,bdhnkc->bdhnqk", qs, kw, preferred_element_type=jnp.float32) * dh ** -0.5
    qi = jnp.arange(ATT_BLOCK)[:, None] + ATT_BLOCK
    ki = jnp.arange(2 * ATT_BLOCK)[None, :]
    dist = qi - ki
    blk = jnp.arange(nb)[:, None, None]
    valid = (dist >= 0) & (dist <= span) & (blk * ATT_BLOCK + ki - ATT_BLOCK >= 0)
    s = jnp.where(valid, s, -jnp.inf)
    m = jnp.max(s, -1, keepdims=True)
    p = jnp.exp(s - m)
    l = jnp.sum(p, -1, keepdims=True)
    o = jnp.einsum("bdhnqk,bdhnkc->bdhnqc", p, vw.astype(jnp.float32)) / l
    lse = (m + jnp.log(l))[..., 0]
    o = o.reshape(b, dilation, h, n, dh).transpose(0, 3, 1, 2, 4).reshape(b, total, h, dh)[:, :t]
    lse = lse.reshape(b, dilation, h, n).transpose(0, 3, 1, 2).reshape(b, total, h)[:, :t]
    return o, lse


def dilated_attention(q, k, v, pos):
    b, t, _ = q.shape
    n_g = len(ATT_GROUPS)
    flat = (b, t, n_g * ATT_HEADS_PER_GROUP, ATT_HEAD_DIM)
    grp = (b, t, n_g, ATT_HEADS_PER_GROUP, ATT_HEAD_DIM)
    q = rope(q.reshape(flat), pos).reshape(grp)
    k = rope(k.reshape(flat), pos).reshape(grp)
    v = v.reshape(grp)
    outs, lses = [], []
    for gi, (window, dilation) in enumerate(ATT_GROUPS):
        o_g, lse_g = dilated_group_attention(q[:, :, gi], k[:, :, gi], v[:, :, gi], dilation, window // dilation)
        outs.append(o_g)
        lses.append(lse_g)
    wts = jax.nn.softmax(jnp.stack(lses, 2), axis=2)
    o = jnp.sum(jnp.stack(outs, 2) * wts[..., None], axis=2)
    return o.reshape(b, t, ATT_OUT_WIDTH).astype(q.dtype)


def swiglu(h, w_gate_up, w_down):
    gate, up = jnp.split(h @ w_gate_up, 2, axis=-1)
    return (jax.nn.silu(gate) * up) @ w_down


def route(xf, w_router, router_bias):
    s = jax.nn.sigmoid(xf.astype(jnp.float32) @ w_router.astype(jnp.float32))
    sc = s + router_bias.astype(jnp.float32)
    grp = sc.reshape(-1, N_GROUPS, N_EXPERTS // N_GROUPS)
    grp_score = jnp.sum(lax.top_k(grp, 2)[0], -1)
    _, gidx = lax.top_k(grp_score, TOPK_GROUPS)
    gmask = jnp.sum(jax.nn.one_hot(gidx, N_GROUPS), -2) > 0
    emask = jnp.repeat(gmask, N_EXPERTS // N_GROUPS, axis=-1)
    _, idx = lax.top_k(jnp.where(emask, sc, -jnp.inf), TOP_K)
    w = jnp.take_along_axis(s, idx, -1)
    w = w / jnp.sum(w, -1, keepdims=True) * ROUTED_SCALE
    return idx, w


def routed_experts(xf, idx, w, w_exp_gate_up, w_exp_down):
    n, d = xf.shape
    r = MOE_ROW_BLOCK
    nk = n * TOP_K
    nb = -(-nk // r) + N_EXPERTS
    flat_e = idx.reshape(-1).astype(jnp.int32)
    order = jnp.argsort(flat_e)
    e_sorted = flat_e[order]
    tok_sorted = (order // TOP_K).astype(jnp.int32)
    w_sorted = w.reshape(-1)[order]
    counts = jnp.bincount(flat_e, length=N_EXPERTS).astype(jnp.int32)
    padded = (counts + r - 1) // r * r
    pad_end = jnp.cumsum(padded)
    pad_start = pad_end - padded
    start = jnp.cumsum(counts) - counts
    dest = pad_start[e_sorted] + (jnp.arange(nk, dtype=jnp.int32) - start[e_sorted])
    slot_tok = jnp.zeros((nb * r,), jnp.int32).at[dest].set(tok_sorted).reshape(nb, r)
    slot_w = jnp.zeros((nb * r,), w.dtype).at[dest].set(w_sorted).reshape(nb, r)
    blk_expert = jnp.minimum(jnp.searchsorted(pad_end, jnp.arange(nb, dtype=jnp.int32) * r, side="right"),
                             N_EXPERTS - 1).astype(jnp.int32)

    def body(acc, xs):
        e, tok, wt = xs
        xb = xf[tok]
        gate, up = jnp.split(xb @ w_exp_gate_up[e], 2, axis=-1)
        y = (jax.nn.silu(gate) * up) @ w_exp_down[e]
        y = y * wt[:, None].astype(y.dtype)
        return acc.at[tok].add(y), None

    acc, _ = lax.scan(body, jnp.zeros_like(xf), (blk_expert, slot_tok, slot_w))
    return acc


def moe(x, w_router, router_bias, w_exp_gate_up, w_exp_down, w_sh_gate_up, w_sh_down):
    b, t, d = x.shape
    xf = x.reshape(b * t, d)
    idx, w = route(xf, w_router, router_bias)
    y = routed_experts(xf, idx, w, w_exp_gate_up, w_exp_down) + swiglu(xf, w_sh_gate_up, w_sh_down)
    return y.reshape(b, t, d).astype(x.dtype)


def setup_inputs(seed: int = 0) -> dict:
    key = jax.random.key(seed)
    ks = jax.random.split(key, 20)
    L = DEPTH

    def normal(k, shape, scale):
        return jax.random.normal(k, shape, jnp.float32) * scale

    dt = jnp.exp(jax.random.uniform(ks[4], (L, DN_HEADS), jnp.float32, math.log(1e-3), math.log(1e-1)))
    return {
        "x": normal(ks[0], (BATCH, SEQ, D_MODEL), 1.0),
        "w_in": normal(ks[1], (L, D_MODEL, IN_COLS), D_MODEL ** -0.5),
        "conv_w": normal(ks[2], (L, DN_CONV, 3 * DN_WIDTH), DN_CONV ** -0.5),
        "a_log": jnp.log(jax.random.uniform(ks[3], (L, DN_HEADS), jnp.float32, 1.0, 16.0)),
        "dt_bias": dt + jnp.log(-jnp.expm1(-dt)),
        "dn_norm_w": 1.0 + normal(ks[5], (L, DN_HEAD_DIM), 0.02),
        "w_dn_branch": normal(ks[6], (L, DN_WIDTH, D_MODEL), DN_WIDTH ** -0.5 * DEEPNORM_BETA),
        "w_att_branch": normal(ks[7], (L, ATT_OUT_WIDTH, D_MODEL), ATT_OUT_WIDTH ** -0.5 * DEEPNORM_BETA),
        "w_o": normal(ks[8], (L, D_MODEL, D_MODEL), D_MODEL ** -0.5 * DEEPNORM_BETA),
        "ln1_g": 1.0 + normal(ks[9], (L, D_MODEL), 0.02),
        "ln1_b": normal(ks[10], (L, D_MODEL), 0.02),
        "w_router": normal(ks[11], (L, D_MODEL, N_EXPERTS), D_MODEL ** -0.5),
        "router_bias": normal(ks[12], (L, N_EXPERTS), 0.01),
        "w_exp_gate_up": normal(ks[13], (L, N_EXPERTS, D_MODEL, 2 * EXPERT_FF), D_MODEL ** -0.5),
        "w_exp_down": normal(ks[14], (L, N_EXPERTS, EXPERT_FF, D_MODEL), EXPERT_FF ** -0.5 * DEEPNORM_BETA),
        "w_sh_gate_up": normal(ks[15], (L, D_MODEL, 2 * EXPERT_FF), D_MODEL ** -0.5),
        "w_sh_down": normal(ks[16], (L, EXPERT_FF, D_MODEL), EXPERT_FF ** -0.5 * DEEPNORM_BETA),
        "ln2_g": 1.0 + normal(ks[17], (L, D_MODEL), 0.02),
        "ln2_b": normal(ks[18], (L, D_MODEL), 0.02),
    }


def reference(x, w_in, conv_w, a_log, dt_bias, dn_norm_w, w_dn_branch, w_att_branch, w_o,
              ln1_g, ln1_b, w_router, router_bias, w_exp_gate_up, w_exp_down,
              w_sh_gate_up, w_sh_down, ln2_g, ln2_b):
    pos = jnp.arange(x.shape[1], dtype=jnp.int32)
    split_points = [int(c) for c in np.cumsum(IN_SIZES)[:-1]]
    for l in range(DEPTH):
        proj = x @ w_in[l]
        (dn_q, dn_k, dn_v, dn_z, dn_b, dn_a, at_q, at_k, at_v,
         gate_dn, gate_att) = jnp.split(proj, split_points, axis=-1)
        o_dn = gated_deltanet(dn_q, dn_k, dn_v, dn_z, dn_b, dn_a, conv_w[l], a_log[l], dt_bias[l], dn_norm_w[l])
        o_att = dilated_attention(at_q, at_k, at_v, pos)
        merged = (jax.nn.sigmoid(gate_dn) * (o_dn @ w_dn_branch[l])
                  + jax.nn.sigmoid(gate_att) * (o_att @ w_att_branch[l]))
        x = layer_norm(DEEPNORM_ALPHA * x + merged @ w_o[l], ln1_g[l], ln1_b[l])
        ffn = moe(x, w_router[l], router_bias[l], w_exp_gate_up[l], w_exp_down[l], w_sh_gate_up[l], w_sh_down[l])
        x = layer_norm(DEEPNORM_ALPHA * x + ffn, ln2_g[l], ln2_b[l])
    return x
```

```python
import functools

import jax
import jax.numpy as jnp
from jax import lax
from jax.experimental import pallas as pl
from jax.experimental.pallas import tpu as pltpu

F32 = jnp.float32
BF16 = jnp.bfloat16

DN_HEADS = 16
DN_HEAD_DIM = 128
DN_WIDTH = DN_HEADS * DN_HEAD_DIM
DN_CONV = 4
DN_CHUNK = 64
ATT_GROUPS = ((128, 1), (512, 4), (2048, 16))
ATT_HEADS_PER_GROUP = 8
ATT_HEAD_DIM = 128
ATT_GROUP_WIDTH = ATT_HEADS_PER_GROUP * ATT_HEAD_DIM
ATT_WIDTH = len(ATT_GROUPS) * ATT_GROUP_WIDTH
ATT_BLOCK = 128
ROPE_THETA = 10000.0
N_EXPERTS = 128
TOP_K = 8
N_GROUPS = 8
GROUP_SIZE = N_EXPERTS // N_GROUPS
TOPK_GROUPS = 4
EXPERT_FF = 384
ROUTED_SCALE = 2.5
LN_EPS = 1e-5
RMS_EPS = 1e-6
L2_EPS = 1e-6

LANES = 128
SUBLANES = 8
BF16_SUBLANES = 16
VMEM_LIMIT = 56 * 1024 * 1024

MOE_ROWS = 256
NEG_BIG = -1e30


def _cparams(sem, vmem=VMEM_LIMIT):
    return pltpu.CompilerParams(dimension_semantics=sem, vmem_limit_bytes=vmem)


def _split3(a):
    hi = a.astype(BF16)
    r1 = a - hi.astype(F32)
    mid = r1.astype(BF16)
    lo = (r1 - mid.astype(F32)).astype(BF16)
    return hi, mid, lo


def _dot(a, b):
    return jnp.dot(a, b, preferred_element_type=F32)


def _dot_nt(a, b):
    return lax.dot_general(a, b, (((1,), (1,)), ((), ())), preferred_element_type=F32)


def _dot_tn(a, b):
    return lax.dot_general(a, b, (((0,), (0,)), ((), ())), preferred_element_type=F32)


def _sigmoid(x):
    return 1.0 / (1.0 + jnp.exp(-x))


def _mm_kernel(a_ref, b_ref, o_ref):
    o_ref[...] = _dot(a_ref[...], b_ref[...]).astype(o_ref.dtype)


def _matmul(a, b, tm, tn, out_dtype):
    m, k = a.shape
    n = b.shape[1]
    tm, tn = min(tm, m), min(tn, n)
    return pl.pallas_call(
        _mm_kernel,
        grid=(m // tm, n // tn),
        in_specs=[pl.BlockSpec((tm, k), lambda i, j: (i, 0)),
                  pl.BlockSpec((k, tn), lambda i, j: (0, j))],
        out_specs=pl.BlockSpec((tm, tn), lambda i, j: (i, j)),
        out_shape=jax.ShapeDtypeStruct((m, n), out_dtype),
        compiler_params=_cparams(("parallel", "parallel")),
        name="proj",
    )(a, b)


def _gates_kernel(x_ref, w_ref, wt_ref, prm_ref, prmt_ref, betab_ref, gcb_ref, gct_ref):
    tm = x_ref.shape[0]
    xs = _split3(x_ref[...])
    ws = _split3(w_ref[...])
    wts = _split3(wt_ref[...])
    logits = jnp.zeros((tm, LANES), F32)
    logits_t = jnp.zeros((2 * DN_HEADS, tm), F32)
    for ia, ib in ((0, 0), (0, 1), (1, 0), (1, 1), (0, 2), (2, 0)):
        logits = logits + _dot(xs[ia], ws[ib])
        logits_t = logits_t + _dot_nt(wts[ib], xs[ia])

    def softplus(v):
        return jnp.maximum(v, 0.0) + jnp.log(1.0 + jnp.exp(-jnp.abs(v)))

    prm = prm_ref[...]
    prmt = prmt_ref[...]
    beta = _sigmoid(logits)
    g = prm[0:1, :] * softplus(logits + prm[1:2, :])
    g_t = prmt[:, 0:1] * softplus(logits_t + prmt[:, 1:2])

    ri = lax.broadcasted_iota(jnp.int32, (tm, tm), 0)
    ci = lax.broadcasted_iota(jnp.int32, (tm, tm), 1)
    same = (ri // DN_CHUNK) == (ci // DN_CHUNK)
    lower = jnp.where(same & (ci <= ri), 1.0, 0.0).astype(BF16)
    upper = jnp.where(same & (ri <= ci), 1.0, 0.0).astype(BF16)
    gc = jnp.zeros((tm, LANES), F32)
    gc_t = jnp.zeros((2 * DN_HEADS, tm), F32)
    for part in _split3(g):
        gc = gc + _dot(lower, part)
    for part in _split3(g_t):
        gc_t = gc_t + _dot(part, upper)
    gct_ref[...] = gc_t[DN_HEADS:2 * DN_HEADS, :]
    for h in range(DN_HEADS):
        sl = slice(h * LANES, (h + 1) * LANES)
        betab_ref[:, sl] = jnp.broadcast_to(beta[:, h:h + 1], (tm, LANES))
        gcb_ref[:, sl] = jnp.broadcast_to(gc[:, DN_HEADS + h:DN_HEADS + h + 1], (tm, LANES))


def _dn_gates(x2, w_ba, a_log, dt_bias):
    t, d = x2.shape
    tm = min(512, t)
    w_pad = jnp.zeros((d, LANES), F32).at[:, :2 * DN_HEADS].set(w_ba)
    w_t = w_ba.T
    neg_a = -jnp.exp(a_log.astype(F32))
    prm = jnp.zeros((SUBLANES, LANES), F32)
    prm = prm.at[0, DN_HEADS:2 * DN_HEADS].set(neg_a).at[1, DN_HEADS:2 * DN_HEADS].set(dt_bias.astype(F32))
    prmt = jnp.zeros((2 * DN_HEADS, LANES), F32)
    prmt = prmt.at[DN_HEADS:, 0].set(neg_a).at[DN_HEADS:, 1].set(dt_bias.astype(F32))
    return pl.pallas_call(
        _gates_kernel,
        grid=(t // tm,),
        in_specs=[pl.BlockSpec((tm, d), lambda i: (i, 0)),
                  pl.BlockSpec((d, LANES), lambda i: (0, 0)),
                  pl.BlockSpec((2 * DN_HEADS, d), lambda i: (0, 0)),
                  pl.BlockSpec((SUBLANES, LANES), lambda i: (0, 0)),
                  pl.BlockSpec((2 * DN_HEADS, LANES), lambda i: (0, 0))],
        out_specs=[pl.BlockSpec((tm, DN_WIDTH), lambda i: (i, 0)),
                   pl.BlockSpec((tm, DN_WIDTH), lambda i: (i, 0)),
                   pl.BlockSpec((DN_HEADS, tm), lambda i: (0, i))],
        out_shape=[jax.ShapeDtypeStruct((t, DN_WIDTH), F32),
                   jax.ShapeDtypeStruct((t, DN_WIDTH), F32),
                   jax.ShapeDtypeStruct((DN_HEADS, t), F32)],
        compiler_params=_cparams(("parallel",)),
        name="dn_gates",
    )(x2, w_pad, w_t, prm, prmt)


def _dn_prep_kernel(q_ref, k_ref, v_ref, hq_ref, hk_ref, hv_ref, cq_ref, ck_ref, cv_ref,
                    betab_ref, gcb_ref, gct_ref,
                    u_ref, w_ref, qe_ref, kd_ref, intra_ref, egl_ref):
    i = pl.program_id(1)
    rows = q_ref.shape[0]
    c = DN_CHUNK
    halo_rows = hq_ref.shape[0]

    def conv_silu(x_ref, halo_ref, cw_ref):
        x = x_ref[...].astype(F32)
        halo = jnp.where(i > 0, halo_ref[...].astype(F32), 0.0)
        full = jnp.concatenate([halo, x], axis=0)
        cw = cw_ref[...]
        y = jnp.zeros((rows, LANES), F32)
        for j in range(DN_CONV):
            off = halo_rows - (DN_CONV - 1) + j
            y = y + cw[j:j + 1, :] * full[off:off + rows, :]
        return y * _sigmoid(y)

    def l2n(a):
        return a * lax.rsqrt(jnp.sum(a * a, axis=-1, keepdims=True) + L2_EPS)

    q = l2n(conv_silu(q_ref, hq_ref, cq_ref)) * (DN_HEAD_DIM ** -0.5)
    k = l2n(conv_silu(k_ref, hk_ref, ck_ref))
    v = conv_silu(v_ref, hv_ref, cv_ref)
    beta = betab_ref[...]
    gcol = gcb_ref[...]
    grow_all = gct_ref[...]

    ii = lax.broadcasted_iota(jnp.int32, (c, c), 0)
    jj = lax.broadcasted_iota(jnp.int32, (c, c), 1)
    eye = jnp.where(ii == jj, 1.0, 0.0)
    n_chunks = rows // c

    decays, a_mats = [], []
    kbs, egcs = [], []
    for n in range(n_chunks):
        rs = slice(n * c, (n + 1) * c)
        kc = k[rs]
        kb = kc * beta[rs]
        grow = grow_all[:, rs]
        diff = jnp.where(ii >= jj, gcol[rs, :c] - grow, 0.0)
        decay = jnp.where(ii >= jj, jnp.exp(diff), 0.0)
        kk = _dot_nt(kb.astype(BF16), kc.astype(BF16))
        a_mats.append(jnp.where(ii > jj, kk * decay, 0.0))
        decays.append(decay)
        kbs.append(kb)
        egcs.append(jnp.exp(gcol[rs]))

    invs = []
    for n in range(n_chunks):
        invs.append(eye - jnp.where((ii // 2 == jj // 2) & (ii > jj), a_mats[n], 0.0))
    s = 2
    while s < c:
        sel = ((ii // (2 * s)) == (jj // (2 * s))) & ((ii // s) > (jj // s))
        for n in range(n_chunks):
            d_b = invs[n].astype(BF16)
            x_b = jnp.where(sel, a_mats[n], 0.0).astype(BF16)
            t1 = _dot(d_b, x_b).astype(BF16)
            invs[n] = invs[n] - _dot(t1, d_b)
        s *= 2

    for n in range(n_chunks):
        rs = slice(n * c, (n + 1) * c)
        t_b = invs[n].astype(BF16)
        u_ref[rs, :] = _dot(t_b, (v[rs] * beta[rs]).astype(BF16)).astype(u_ref.dtype)
        w_ref[rs, :] = _dot(t_b, (kbs[n] * egcs[n]).astype(BF16)).astype(w_ref.dtype)
        qc = q[rs]
        kc = k[rs]
        qk = _dot_nt(qc.astype(BF16), kc.astype(BF16))
        intra = qk * decays[n]
        intra_ref[rs, :] = jnp.concatenate([intra, jnp.zeros_like(intra)], axis=1).astype(intra_ref.dtype)
        qe_ref[rs, :] = (qc * egcs[n]).astype(qe_ref.dtype)
        glast = gcol[(n + 1) * c - 1:(n + 1) * c, :]
        kd_ref[rs, :] = (kc * jnp.exp(glast - gcol[rs])).astype(kd_ref.dtype)
        egl_ref[n * SUBLANES:(n + 1) * SUBLANES, :] = jnp.broadcast_to(jnp.exp(glast),
                                                                       (SUBLANES, LANES))


def _dn_prep(proj, conv_w, betab, gcb, gct, t):
    rows = min(512, t)
    hb = BF16_SUBLANES
    nq = DN_WIDTH // LANES

    def blk(off):
        return pl.BlockSpec((rows, LANES), lambda h, i, off=off: (i, off + h))

    def halo(off):
        return pl.BlockSpec((hb, LANES),
                            lambda h, i, off=off: (jnp.maximum(i * (rows // hb) - 1, 0), off + h))

    def cw(off):
        return pl.BlockSpec((DN_CONV, LANES), lambda h, i, off=off: (0, off + h))

    per_head = pl.BlockSpec((rows, LANES), lambda h, i: (i, h))
    outs = [jax.ShapeDtypeStruct((t, DN_WIDTH), BF16)] * 5
    outs.append(jax.ShapeDtypeStruct((t // DN_CHUNK * SUBLANES, DN_WIDTH), F32))
    return pl.pallas_call(
        _dn_prep_kernel,
        grid=(DN_HEADS, t // rows),
        in_specs=[blk(0), blk(nq), blk(2 * nq), halo(0), halo(nq), halo(2 * nq),
                  cw(0), cw(nq), cw(2 * nq), per_head, per_head,
                  pl.BlockSpec((None, 1, rows), lambda h, i: (h, 0, i))],
        out_specs=[per_head] * 5 + [pl.BlockSpec((rows // DN_CHUNK * SUBLANES, LANES), lambda h, i: (i, h))],
        out_shape=outs,
        compiler_params=_cparams(("parallel", "parallel")),
        name="dn_prep",
    )(proj, proj, proj, proj, proj, proj, conv_w, conv_w, conv_w, betab, gcb,
      gct.reshape(DN_HEADS, 1, t))


def _dn_scan_kernel(u_ref, w_ref, qe_ref, kd_ref, intra_ref, egl_ref, z_ref, nw_ref, o_ref, s_ref):
    i = pl.program_id(0)
    c = DN_CHUNK
    n_chunks = u_ref.shape[0] // c

    @pl.when(i == 0)
    def _():
        s_ref[...] = jnp.zeros_like(s_ref)

    nw = nw_ref[...]

    def chunk(n, carry):
        r0 = pl.multiple_of(n * c, c)
        e0 = pl.multiple_of(n * SUBLANES, SUBLANES)
        for h in range(DN_HEADS):
            ls = slice(h * LANES, (h + 1) * LANES)
            s_h = s_ref[h]
            s_b = s_h.astype(BF16)
            wq = jnp.concatenate([w_ref[pl.ds(r0, c), ls], qe_ref[pl.ds(r0, c), ls]], axis=0)
            ws_qs = _dot(wq, s_b)
            v_new = u_ref[pl.ds(r0, c), ls].astype(F32) - ws_qs[:c]
            v_b = v_new.astype(BF16)
            intra = intra_ref[pl.ds(r0, c), ls][:, :c]
            o = ws_qs[c:] + _dot(intra, v_b)
            eg = egl_ref[pl.ds(e0, SUBLANES), ls]
            s_dec = (s_h.reshape(DN_HEAD_DIM // SUBLANES, SUBLANES, LANES) * eg[None]
                     ).reshape(DN_HEAD_DIM, LANES)
            s_ref[h] = s_dec + _dot_tn(kd_ref[pl.ds(r0, c), ls], v_b)
            z = z_ref[pl.ds(r0, c), ls].astype(F32)
            o = o * lax.rsqrt(jnp.mean(o * o, axis=-1, keepdims=True) + RMS_EPS) * nw * (z * _sigmoid(z))
            o_ref[pl.ds(r0, c), ls] = o.astype(o_ref.dtype)
        return carry

    lax.fori_loop(0, n_chunks, chunk, 0)


def _dn_scan(u, w, qe, kd, intra, egl, proj, norm_w, t):
    rows = min(512, t)
    full = pl.BlockSpec((rows, DN_WIDTH), lambda i: (i, 0))
    return pl.pallas_call(
        _dn_scan_kernel,
        grid=(t // rows,),
        in_specs=[full, full, full, full, full,
                  pl.BlockSpec((rows // DN_CHUNK * SUBLANES, DN_WIDTH), lambda i: (i, 0)),
                  pl.BlockSpec((rows, DN_WIDTH), lambda i: (i, 3)),
                  pl.BlockSpec((1, LANES), lambda i: (0, 0))],
        out_specs=full,
        out_shape=jax.ShapeDtypeStruct((t, DN_WIDTH), BF16),
        scratch_shapes=[pltpu.VMEM((DN_HEADS, DN_HEAD_DIM, DN_HEAD_DIM), F32)],
        compiler_params=_cparams(("arbitrary",)),
        name="dn_scan",
    )(u, w, qe, kd, intra, egl, proj, norm_w.reshape(1, LANES).astype(F32))


def _attn_kernel(q_ref, k_ref, kp_ref, v_ref, vp_ref, cos_ref, sin_ref, cosp_ref, sinp_ref,
                 o_ref, lse_ref, *, span):
    i = pl.program_id(1)
    qb = q_ref.shape[0]
    blk = ATT_BLOCK
    half = ATT_HEAD_DIM // 2
    cos, sin = cos_ref[...], sin_ref[...]
    cosp, sinp = cosp_ref[...], sinp_ref[...]

    def rope(a, cs, sn):
        a = a.astype(F32)
        return a * cs + pltpu.roll(a, half, 1) * sn

    qi = lax.broadcasted_iota(jnp.int32, (blk, 2 * blk), 0) + blk
    ki = lax.broadcasted_iota(jnp.int32, (blk, 2 * blk), 1)
    dist = qi - ki
    band = (dist >= 0) & (dist <= span)
    band_first = band & ((ki >= blk) | (i > 0))
    scale = ATT_HEAD_DIM ** -0.5

    for h in range(ATT_HEADS_PER_GROUP):
        ls = slice(h * LANES, (h + 1) * LANES)
        qh = (rope(q_ref[:, ls], cos, sin) * scale).astype(BF16)
        k_all = jnp.concatenate([rope(kp_ref[:, ls], cosp, sinp), rope(k_ref[:, ls], cos, sin)],
                                axis=0).astype(BF16)
        v_all = jnp.concatenate([vp_ref[:, ls], v_ref[:, ls]], axis=0)
        for j in range(qb // blk):
            s = _dot_nt(qh[j * blk:(j + 1) * blk], k_all[j * blk:(j + 2) * blk])
            s = jnp.where(band_first if j == 0 else band, s, NEG_BIG)
            m = jnp.max(s, axis=-1, keepdims=True)
            p = jnp.exp(s - m)
            l = jnp.sum(p, axis=-1, keepdims=True)
            o = _dot(p.astype(BF16), v_all[j * blk:(j + 2) * blk]) / l
            o_ref[j * blk:(j + 1) * blk, ls] = o.astype(o_ref.dtype)
            lse_ref[j * blk:(j + 1) * blk, ls] = jnp.broadcast_to(m + jnp.log(l), (blk, LANES))


def _attn_group(proj, cos_t, sin_t, gi, dilation, span, t, ncols):
    n = t // dilation
    qb = min(512, n)
    bpq = qb // ATT_BLOCK
    gw = ATT_GROUP_WIDTH
    cpr = ncols // gw
    q0 = 4 * DN_WIDTH // gw + gi
    k0 = q0 + ATT_WIDTH // gw
    v0 = k0 + ATT_WIDTH // gw
    proj_r = proj.reshape(n, dilation * ncols)
    cos_r = cos_t.reshape(n, dilation * LANES)
    sin_r = sin_t.reshape(n, dilation * LANES)

    def cur(c0):
        return pl.BlockSpec((qb, gw), lambda r, i, c0=c0: (i, r * cpr + c0))

    def prev(c0):
        return pl.BlockSpec((ATT_BLOCK, gw),
                            lambda r, i, c0=c0: (jnp.maximum(i * bpq - 1, 0), r * cpr + c0))

    tab = pl.BlockSpec((qb, LANES), lambda r, i: (i, r))
    tabp = pl.BlockSpec((ATT_BLOCK, LANES), lambda r, i: (jnp.maximum(i * bpq - 1, 0), r))
    out = pl.BlockSpec((qb, gw), lambda r, i: (i, r))
    o, lse = pl.pallas_call(
        functools.partial(_attn_kernel, span=span),
        grid=(dilation, n // qb),
        in_specs=[cur(q0), cur(k0), prev(k0), cur(v0), prev(v0), tab, tab, tabp, tabp],
        out_specs=[out, out],
        out_shape=[jax.ShapeDtypeStruct((n, dilation * gw), BF16),
                   jax.ShapeDtypeStruct((n, dilation * gw), F32)],
        compiler_params=_cparams(("parallel", "parallel")),
        name=f"attn_g{gi}",
    )(proj_r, proj_r, proj_r, proj_r, proj_r, cos_r, sin_r, cos_r, sin_r)
    return o.reshape(t, gw), lse.reshape(t, gw)


def _mix_kernel(odn_ref, o0_ref, o1_ref, o2_ref, l0_ref, l1_ref, l2_ref, wdn_ref, watt_ref,
                gdn_ref, gatt_ref, out_ref):
    l0, l1, l2 = l0_ref[...], l1_ref[...], l2_ref[...]
    m = jnp.maximum(jnp.maximum(l0, l1), l2)
    e0, e1, e2 = jnp.exp(l0 - m), jnp.exp(l1 - m), jnp.exp(l2 - m)
    o_att = (e0 * o0_ref[...].astype(F32) + e1 * o1_ref[...].astype(F32)
             + e2 * o2_ref[...].astype(F32)) / (e0 + e1 + e2)
    y_dn = _dot(odn_ref[...], wdn_ref[...])
    y_att = _dot(o_att.astype(BF16), watt_ref[...])
    merged = (_sigmoid(gdn_ref[...].astype(F32)) * y_dn
              + _sigmoid(gatt_ref[...].astype(F32)) * y_att)
    out_ref[...] = merged.astype(out_ref.dtype)


def _mix(o_dn, ogs, lses, w_dn, w_att, proj, t, d):
    tm = min(512, t)
    tn = ATT_GROUP_WIDTH
    g0 = (4 * DN_WIDTH + 3 * ATT_WIDTH) // tn
    row_dn = pl.BlockSpec((tm, DN_WIDTH), lambda j, i: (i, 0))
    row_g = pl.BlockSpec((tm, ATT_GROUP_WIDTH), lambda j, i: (i, 0))
    return pl.pallas_call(
        _mix_kernel,
        grid=(d // tn, t // tm),
        in_specs=[row_dn, row_g, row_g, row_g, row_g, row_g, row_g,
                  pl.BlockSpec((DN_WIDTH, tn), lambda j, i: (0, j)),
                  pl.BlockSpec((ATT_GROUP_WIDTH, tn), lambda j, i: (0, j)),
                  pl.BlockSpec((tm, tn), lambda j, i: (i, g0 + j)),
                  pl.BlockSpec((tm, tn), lambda j, i: (i, g0 + d // tn + j))],
        out_specs=pl.BlockSpec((tm, tn), lambda j, i: (i, j)),
        out_shape=jax.ShapeDtypeStruct((t, d), BF16),
        compiler_params=_cparams(("parallel", "parallel")),
        name="mix",
    )(o_dn, *ogs, *lses, w_dn, w_att, proj, proj)


def _layer_norm(y, g, b):
    mu = jnp.mean(y, axis=-1, keepdims=True)
    yc = y - mu
    var = jnp.mean(yc * yc, axis=-1, keepdims=True)
    return yc * lax.rsqrt(var + LN_EPS) * g + b


def _wo_ln_kernel(m_ref, w_ref, x_ref, g_ref, b_ref, x1_ref, x1b_ref, acc_ref, *, alpha):
    kk = pl.program_id(1)

    @pl.when(kk == 0)
    def _():
        acc_ref[...] = jnp.zeros_like(acc_ref)

    acc_ref[...] += _dot(m_ref[...], w_ref[...])

    @pl.when(kk == pl.num_programs(1) - 1)
    def _():
        y = _layer_norm(alpha * x_ref[...] + acc_ref[...], g_ref[...], b_ref[...])
        x1_ref[...] = y
        x1b_ref[...] = y.astype(BF16)


def _wo_ln(merged, w_o, x2, g, b, t, d, alpha):
    tm = min(256, t)
    tk = min(1024, d)
    row = pl.BlockSpec((tm, d), lambda i, k: (i, 0))
    vec = pl.BlockSpec((1, d), lambda i, k: (0, 0))
    return pl.pallas_call(
        functools.partial(_wo_ln_kernel, alpha=alpha),
        grid=(t // tm, d // tk),
        in_specs=[pl.BlockSpec((tm, tk), lambda i, k: (i, k)),
                  pl.BlockSpec((tk, d), lambda i, k: (k, 0)),
                  row, vec, vec],
        out_specs=[row, row],
        out_shape=[jax.ShapeDtypeStruct((t, d), F32), jax.ShapeDtypeStruct((t, d), BF16)],
        scratch_shapes=[pltpu.VMEM((tm, d), F32)],
        compiler_params=_cparams(("parallel", "arbitrary")),
        name="wo_ln1",
    )(merged, w_o, x2, g.reshape(1, d).astype(F32), b.reshape(1, d).astype(F32))


def _router_kernel(x_ref, w_ref, bias_ref, topi_ref, topw_ref, rank_ref, cnt_ref, run_ref):
    i = pl.program_id(0)
    tm = x_ref.shape[0]

    @pl.when(i == 0)
    def _():
        run_ref[...] = jnp.zeros_like(run_ref)

    xs = _split3(x_ref[...])
    ws = _split3(w_ref[...])
    logits = jnp.zeros((tm, N_EXPERTS), F32)
    for ia, ib in ((0, 0), (0, 1), (1, 0), (1, 1), (0, 2), (2, 0)):
        logits = logits + _dot(xs[ia], ws[ib])
    s = _sigmoid(logits)
    sc = s + bias_ref[...]
    lane = lax.broadcasted_iota(jnp.int32, (tm, N_EXPERTS), 1)
    grp = lane // GROUP_SIZE
    neg = -jnp.inf

    def first_argmax(v):
        m = jnp.max(v, axis=-1, keepdims=True)
        idx = jnp.min(jnp.where(v == m, lane, N_EXPERTS), axis=-1, keepdims=True)
        return m, idx

    gscore = []
    for gi in range(N_GROUPS):
        vg = jnp.where(grp == gi, sc, neg)
        m1, i1 = first_argmax(vg)
        m2 = jnp.max(jnp.where(lane == i1, neg, vg), axis=-1, keepdims=True)
        gscore.append(m1 + m2)
    emask = jnp.zeros((tm, N_EXPERTS), jnp.bool_)
    for gi in range(N_GROUPS):
        ahead = jnp.zeros((tm, 1), jnp.int32)
        for gj in range(N_GROUPS):
            if gj == gi:
                continue
            beats = (gscore[gj] > gscore[gi]) | ((gscore[gj] == gscore[gi]) & (gj < gi))
            ahead = ahead + beats.astype(jnp.int32)
        emask = emask | ((grp == gi) & (ahead < TOPK_GROUPS))
    masked = jnp.where(emask, sc, neg)

    sel =jnp.zeros((tm, N_EXPERTS), jnp.bool_)
    idxs, vals = [], []
    for _ in range(TOP_K):
        _, ik = first_argmax(masked)
        hit = lane == ik
        sel = sel | hit
        masked = jnp.where(hit, neg, masked)
        idxs.append(ik)
        vals.append(jnp.sum(jnp.where(hit, s, 0.0), axis=-1, keepdims=True))
    wsum = vals[0]
    for v in vals[1:]:
        wsum = wsum + v

    sel_b = jnp.where(sel, 1.0, 0.0).astype(BF16)
    strict = jnp.where(lax.broadcasted_iota(jnp.int32, (tm, tm), 1)
                       < lax.broadcasted_iota(jnp.int32, (tm, tm), 0), 1.0, 0.0).astype(BF16)
    rank_excl = run_ref[0:1, :] + _dot(strict, sel_b)
    run_new = run_ref[0:1, :] + jnp.sum(sel_b.astype(F32), axis=0, keepdims=True)
    run_ref[...] = jnp.broadcast_to(run_new, run_ref.shape)
    cnt_ref[...] = jnp.broadcast_to(run_new, cnt_ref.shape).astype(jnp.int32)

    topi = jnp.zeros((tm, N_EXPERTS), jnp.int32)
    topw = jnp.zeros((tm, N_EXPERTS), F32)
    rnk = jnp.zeros((tm, N_EXPERTS), F32)
    for kk in range(TOP_K):
        hit = lane == idxs[kk]
        rk = jnp.sum(jnp.where(hit, rank_excl, 0.0), axis=-1, keepdims=True)
        topi = jnp.where(lane == kk, idxs[kk], topi)
        topw = jnp.where(lane == kk, vals[kk] / wsum * ROUTED_SCALE, topw)
        rnk = jnp.where(lane == kk, rk, rnk)
    topi_ref[...] = topi
    topw_ref[...] = topw
    rank_ref[...] = rnk.astype(jnp.int32)


def _router(x1, w_router, bias, t, d):
    tm = min(512, t)
    row = pl.BlockSpec((tm, N_EXPERTS), lambda i: (i, 0))
    return pl.pallas_call(
        _router_kernel,
        grid=(t // tm,),
        in_specs=[pl.BlockSpec((tm, d), lambda i: (i, 0)),
                  pl.BlockSpec((d, N_EXPERTS), lambda i: (0, 0)),
                  pl.BlockSpec((1, N_EXPERTS), lambda i: (0, 0))],
        out_specs=[row, row, row, pl.BlockSpec((SUBLANES, N_EXPERTS), lambda i: (0, 0))],
        out_shape=[jax.ShapeDtypeStruct((t, N_EXPERTS), jnp.int32),
                   jax.ShapeDtypeStruct((t, N_EXPERTS), F32),
                   jax.ShapeDtypeStruct((t, N_EXPERTS), jnp.int32),
                   jax.ShapeDtypeStruct((SUBLANES, N_EXPERTS), jnp.int32)],
        scratch_shapes=[pltpu.VMEM((SUBLANES, N_EXPERTS), F32)],
        compiler_params=_cparams(("arbitrary",)),
        name="router",
    )(x1, w_router.astype(F32), bias.reshape(1, N_EXPERTS).astype(F32))


def _row_copy(src_ref, src_row, dst_ref, dst_row, sem):
    return pltpu.make_async_copy(src_ref.at[pl.ds(src_row, 1)], dst_ref.at[pl.ds(dst_row, 1)], sem)


def _dispatch_kernel(pos_ref, x_ref, xs_ref, sem):
    i = pl.program_id(0)
    tb = x_ref.shape[0]
    base = i * (tb * TOP_K)

    def start(r, carry):
        for kk in range(TOP_K):
            _row_copy(x_ref, r, xs_ref, pos_ref[base + r * TOP_K + kk], sem).start()
        return carry

    def wait(r, carry):
        for kk in range(TOP_K):
            _row_copy(x_ref, r, xs_ref, pos_ref[base + r * TOP_K + kk], sem).wait()
        return carry

    lax.fori_loop(0, tb, start, 0)
    lax.fori_loop(0, tb, wait, 0)


def _dispatch(pos_flat, x1, n_slots, t, d):
    tb = min(256, t)
    return pl.pallas_call(
        _dispatch_kernel,
        grid_spec=pltpu.PrefetchScalarGridSpec(
            num_scalar_prefetch=1,
            grid=(t // tb,),
            in_specs=[pl.BlockSpec((tb, d), lambda i, pos: (i, 0))],
            out_specs=pl.BlockSpec(memory_space=pl.ANY),
            scratch_shapes=[pltpu.SemaphoreType.DMA],
        ),
        out_shape=jax.ShapeDtypeStruct((n_slots, d), x1.dtype),
        compiler_params=_cparams(("arbitrary",)),
        name="dispatch",
    )(pos_flat, x1)


def _expert_kernel(be_ref, nv_ref, nu_ref, xs_ref, wgu_ref, wdn_ref, y_ref):
    b = pl.program_id(0)

    @pl.when(b < nu_ref[0])
    def _():
        rows = xs_ref.shape[0]
        rid = lax.broadcasted_iota(jnp.int32, (rows, 1), 0)
        x = jnp.where(rid < nv_ref[b], xs_ref[...], 0.0).astype(BF16)
        hgu = _dot(x, wgu_ref[...])
        gate, up = hgu[:, :EXPERT_FF], hgu[:, EXPERT_FF:]
        act = (gate * _sigmoid(gate) * up).astype(BF16)
        y_ref[...] = _dot(act, wdn_ref[...]).astype(y_ref.dtype)


def _experts(blk_expert, blk_valid, n_used, xs, w_gu, w_dn, d):
    n_slots = xs.shape[0]
    nb = n_slots // MOE_ROWS

    def row_map(b, be, nv, nu):
        return (jnp.minimum(b, nu[0] - 1), 0)

    def w_map(b, be, nv, nu):
        return (be[jnp.minimum(b, nu[0] - 1)], 0, 0)

    return pl.pallas_call(
        _expert_kernel,
        grid_spec=pltpu.PrefetchScalarGridSpec(
            num_scalar_prefetch=3,
            grid=(nb,),
            in_specs=[pl.BlockSpec((MOE_ROWS, d), row_map),
                      pl.BlockSpec((None, d, 2 * EXPERT_FF), w_map),
                      pl.BlockSpec((None, EXPERT_FF, d), w_map)],
            out_specs=pl.BlockSpec((MOE_ROWS, d), row_map),
        ),
        out_shape=jax.ShapeDtypeStruct((n_slots, d), F32),
        compiler_params=_cparams(("arbitrary",)),
        name="experts",
    )(blk_expert, blk_valid, n_used, xs, w_gu, w_dn)


def _shared_kernel(x_ref, wgu_ref, wdn_ref, y_ref):
    hgu = _dot(x_ref[...], wgu_ref[...])
    gate, up = hgu[:, :EXPERT_FF], hgu[:, EXPERT_FF:]
    act = (gate * _sigmoid(gate) * up).astype(BF16)
    y_ref[...] = _dot(act, wdn_ref[...]).astype(y_ref.dtype)


def _shared(x1b, w_gu, w_dn, t, d):
    tm = min(512, t)
    return pl.pallas_call(
        _shared_kernel,
        grid=(t // tm,),
        in_specs=[pl.BlockSpec((tm, d), lambda i: (i, 0)),
                  pl.BlockSpec((d, 2 * EXPERT_FF), lambda i: (0, 0)),
                  pl.BlockSpec((EXPERT_FF, d), lambda i: (0, 0))],
        out_specs=pl.BlockSpec((tm, d), lambda i: (i, 0)),
        out_shape=jax.ShapeDtypeStruct((t, d), BF16),
        compiler_params=_cparams(("parallel",)),
        name="shared",
    )(x1b, w_gu, w_dn)


def _combine_kernel(pos_ref, x_ref, ysh_ref, topw_ref, g_ref, b_ref, y_hbm, out_ref, buf_ref, sem,
                    *, alpha):
    i = pl.program_id(0)
    n = pl.num_programs(0)
    tb = x_ref.shape[0]

    def gather(tile, slot, do_start):
        base = tile * (tb * TOP_K)

        def body(r, carry):
            for kk in range(TOP_K):
                cp = _row_copy(y_hbm, pos_ref[base + r * TOP_K + kk], buf_ref.at[slot, kk], r,
                               sem.at[slot])
                if do_start:
                    cp.start()
                else:
                    cp.wait()
            return carry

        lax.fori_loop(0, tb, body, 0)

    @pl.when(i == 0)
    def _():
        gather(0, 0, True)

    @pl.when(i + 1 < n)
    def _():
        gather(i + 1, (i + 1) % 2, True)

    slot = i % 2
    gather(i, slot, False)
    topw = topw_ref[...]
    acc = alpha * x_ref[...] + ysh_ref[...].astype(F32)
    for kk in range(TOP_K):
        acc = acc + topw[:, kk:kk + 1] * buf_ref[slot, kk]
    out_ref[...] = _layer_norm(acc, g_ref[...], b_ref[...])


def _combine(pos_flat, x1, ysh, topw, g, b, y, t, d, alpha):
    tb = min(64, t)
    row = pl.BlockSpec((tb, d), lambda i, pos: (i, 0))
    vec = pl.BlockSpec((1, d), lambda i, pos: (0, 0))
    return pl.pallas_call(
        functools.partial(_combine_kernel, alpha=alpha),
        grid_spec=pltpu.PrefetchScalarGridSpec(
            num_scalar_prefetch=1,
            grid=(t // tb,),
            in_specs=[row, row, pl.BlockSpec((tb, N_EXPERTS), lambda i, pos: (i, 0)), vec, vec,
                      pl.BlockSpec(memory_space=pl.ANY)],
            out_specs=row,
            scratch_shapes=[pltpu.VMEM((2, TOP_K, tb, d), F32),
                            pltpu.SemaphoreType.DMA((2,))],
        ),
        out_shape=jax.ShapeDtypeStruct((t, d), F32),
        compiler_params=_cparams(("arbitrary",)),
        name="combine",
    )(pos_flat, x1, ysh, topw, g.reshape(1, d).astype(F32), b.reshape(1, d).astype(F32), y)


def _rope_tables(t):
    half = ATT_HEAD_DIM // 2
    inv_freq = ROPE_THETA ** (-jnp.arange(half, dtype=F32) / half)
    ang = jnp.arange(t, dtype=F32)[:, None] * inv_freq[None, :]
    cos, sin = jnp.cos(ang), jnp.sin(ang)
    return jnp.concatenate([cos, cos], -1), jnp.concatenate([-sin, sin], -1)


def _token_mixer(x2, w_in, conv_w, a_log, dt_bias, dn_norm_w, w_dn_branch, w_att_branch, w_o,
                 ln_g, ln_b, alpha):
    t, d = x2.shape
    n_main = 4 * DN_WIDTH
    n_ba = 2 * DN_HEADS
    w_main = jnp.concatenate([w_in[:, :n_main], w_in[:, n_main + n_ba:]], axis=1).astype(BF16)
    w_ba = w_in[:, n_main:n_main + n_ba].astype(F32)
    ncols = w_main.shape[1]

    proj = _matmul(x2.astype(BF16), w_main, 1024, 1024, BF16)
    betab, gcb, gct = _dn_gates(x2, w_ba, a_log, dt_bias)
    u, w, qe, kd, intra, egl = _dn_prep(proj, conv_w.astype(F32), betab, gcb, gct, t)
    o_dn = _dn_scan(u, w, qe, kd, intra, egl, proj, dn_norm_w, t)

    cos_t, sin_t = _rope_tables(t)
    ogs, lses = [], []
    for gi, (window, dilation) in enumerate(ATT_GROUPS):
        o_g, lse_g = _attn_group(proj, cos_t, sin_t, gi, dilation, window // dilation, t, ncols)
        ogs.append(o_g)
        lses.append(lse_g)

    merged = _mix(o_dn, ogs, lses, w_dn_branch.astype(BF16), w_att_branch.astype(BF16), proj, t, d)
    return _wo_ln(merged, w_o.astype(BF16), x2, ln_g, ln_b, t, d, alpha)


def _moe(x1, x1b, w_router, router_bias, w_exp_gate_up, w_exp_down, w_sh_gate_up, w_sh_down,
         ln_g, ln_b, alpha):
    t, d = x1.shape
    r = MOE_ROWS
    topi, topw, rank, cnt = _router(x1, w_router, router_bias, t, d)
    counts = cnt[0]
    padded = (counts + r - 1) // r * r
    pad_end = jnp.cumsum(padded)
    pad_start = pad_end - padded
    nb = (t * TOP_K) // r + N_EXPERTS
    blk_row0 = jnp.arange(nb, dtype=jnp.int32) * r
    blk_expert = jnp.minimum(jnp.searchsorted(pad_end, blk_row0, side="right"),
                             N_EXPERTS - 1).astype(jnp.int32)
    blk_valid = jnp.clip(counts[blk_expert] - (blk_row0 - pad_start[blk_expert]), 0, r).astype(jnp.int32)
    n_used = (pad_end[-1:] // r).astype(jnp.int32)
    pos = (pad_start[topi[:, :TOP_K]] + rank[:, :TOP_K]).astype(jnp.int32)
    pos_flat = pos.reshape(-1)

    xs = _dispatch(pos_flat, x1, nb * r, t, d)
    y = _experts(blk_expert, blk_valid, n_used, xs, w_exp_gate_up.astype(BF16),
                 w_exp_down.astype(BF16), d)
    ysh = _shared(x1b, w_sh_gate_up.astype(BF16), w_sh_down.astype(BF16), t, d)
    return _combine(pos_flat, x1, ysh, topw, ln_g, ln_b, y, t, d, alpha)


def kernel(x, w_in, conv_w, a_log, dt_bias, dn_norm_w, w_dn_branch, w_att_branch, w_o, ln1_g, ln1_b,
           w_router, router_bias, w_exp_gate_up, w_exp_down, w_sh_gate_up, w_sh_down, ln2_g, ln2_b):
    bsz, t, d = x.shape
    depth = w_in.shape[0]
    alpha = (2.0 * depth) ** 0.25
    outs = []
    for bi in range(bsz):
        xb = x[bi]
        for l in range(depth):
            x1, x1b = _token_mixer(xb, w_in[l], conv_w[l], a_log[l], dt_bias[l], dn_norm_w[l],
                                   w_dn_branch[l], w_att_branch[l], w_o[l], ln1_g[l], ln1_b[l],
                                   alpha)
            xb = _moe(x1, x1b, w_router[l], router_bias[l], w_exp_gate_up[l], w_exp_down[l],
                      w_sh_gate_up[l], w_sh_down[l], ln2_g[l], ln2_b[l], alpha)
        outs.append(xb)
    return jnp.stack(outs, 0)
```

```python
import functools

import jax
import jax.numpy as jnp
from jax import lax
from jax.experimental import pallas as pl
from jax.experimental.pallas import tpu as pltpu

F32 = jnp.float32
BF16 = jnp.bfloat16

DN_HEADS = 16
DN_HEAD_DIM = 128
DN_WIDTH = DN_HEADS * DN_HEAD_DIM
DN_CONV = 4
DN_CHUNK = 64
ATT_GROUPS = ((128, 1), (512, 4), (2048, 16))
ATT_HEADS_PER_GROUP = 8
ATT_HEAD_DIM = 128
ATT_GROUP_WIDTH = ATT_HEADS_PER_GROUP * ATT_HEAD_DIM
ATT_WIDTH = len(ATT_GROUPS) * ATT_GROUP_WIDTH
ATT_BLOCK = 128
ROPE_THETA = 10000.0
N_EXPERTS = 128
TOP_K = 8
N_GROUPS = 8
GROUP_SIZE = N_EXPERTS // N_GROUPS
TOPK_GROUPS = 4
EXPERT_FF = 384
ROUTED_SCALE = 2.5
LN_EPS = 1e-5
RMS_EPS = 1e-6
L2_EPS = 1e-6

LANES = 128
SUBLANES = 8
BF16_SUBLANES = 16
VMEM_LIMIT = 56 * 1024 * 1024

EXPERT_VMEM_LIMIT = 60 * 1024 * 1024

MOE_ROWS = 256
EXPERT_CHUNK = 1024
NEG_BIG = -1e30


def _cparams(sem, vmem=VMEM_LIMIT):
    return pltpu.CompilerParams(dimension_semantics=sem, vmem_limit_bytes=vmem)


def _split3(a):
    hi = a.astype(BF16)
    r1 = a - hi.astype(F32)
    mid = r1.astype(BF16)
    lo = (r1 - mid.astype(F32)).astype(BF16)
    return hi, mid, lo


def _dot(a, b):
    return jnp.dot(a, b, preferred_element_type=F32)


def _dot_nt(a, b):
    return lax.dot_general(a, b, (((1,), (1,)), ((), ())), preferred_element_type=F32)


def _dot_tn(a, b):
    return lax.dot_general(a, b, (((0,), (0,)), ((), ())), preferred_element_type=F32)


def _sigmoid(x):
    return 1.0 / (1.0 + jnp.exp(-x))


def _pack_halves(lo, hi):
    lo_b = lax.bitcast_convert_type(lo.astype(BF16).astype(F32), jnp.uint32)
    hi_b = lax.bitcast_convert_type(hi.astype(BF16).astype(F32), jnp.uint32)
    return (hi_b & jnp.uint32(0xFFFF0000)) | (lo_b >> 16)


def _unpack_halves(u):
    lo = lax.bitcast_convert_type(u << 16, F32)
    hi = lax.bitcast_convert_type(u & jnp.uint32(0xFFFF0000), F32)
    return lo, hi


def _mm_kernel(a_ref, b_ref, o_ref):
    o_ref[...] = _dot(a_ref[...], b_ref[...]).astype(o_ref.dtype)


def _matmul(a, b, tm, tn, out_dtype):
    m, k = a.shape
    n = b.shape[1]
    tm, tn = min(tm, m), min(tn, n)
    return pl.pallas_call(
        _mm_kernel,
        grid=(m // tm, n // tn),
        in_specs=[pl.BlockSpec((tm, k), lambda i, j: (i, 0)),
                  pl.BlockSpec((k, tn), lambda i, j: (0, j))],
        out_specs=pl.BlockSpec((tm, tn), lambda i, j: (i, j)),
        out_shape=jax.ShapeDtypeStruct((m, n), out_dtype),
        compiler_params=_cparams(("parallel", "parallel")),
        name="proj",
    )(a, b)


def _gates_kernel(x_ref, w_ref, wt_ref, prm_ref, prmt_ref, betab_ref, gcb_ref, gct_ref):
    tm = x_ref.shape[0]
    xs = _split3(x_ref[...])
    ws = _split3(w_ref[...])
    wts = _split3(wt_ref[...])
    logits = jnp.zeros((tm, LANES), F32)
    logits_t = jnp.zeros((2 * DN_HEADS, tm), F32)
    for ia, ib in ((0, 0), (0, 1), (1, 0), (1, 1), (0, 2), (2, 0)):
        logits = logits + _dot(xs[ia], ws[ib])
        logits_t = logits_t + _dot_nt(wts[ib], xs[ia])

    def softplus(v):
        return jnp.maximum(v, 0.0) + jnp.log(1.0 + jnp.exp(-jnp.abs(v)))

    prm = prm_ref[...]
    prmt = prmt_ref[...]
    beta = _sigmoid(logits)
    g = prm[0:1, :] * softplus(logits + prm[1:2, :])
    g_t = prmt[:, 0:1] * softplus(logits_t + prmt[:, 1:2])

    ri = lax.broadcasted_iota(jnp.int32, (tm, tm), 0)
    ci = lax.broadcasted_iota(jnp.int32, (tm, tm), 1)
    same = (ri // DN_CHUNK) == (ci // DN_CHUNK)
    lower = jnp.where(same & (ci <= ri), 1.0, 0.0).astype(BF16)
    upper = jnp.where(same & (ri <= ci), 1.0, 0.0).astype(BF16)
    gc = jnp.zeros((tm, LANES), F32)
    gc_t = jnp.zeros((2 * DN_HEADS, tm), F32)
    for part in _split3(g):
        gc = gc + _dot(lower, part)
    for part in _split3(g_t):
        gc_t = gc_t + _dot(part, upper)
    gct_ref[...] = gc_t[DN_HEADS:2 * DN_HEADS, :]
    for h in range(DN_HEADS):
        sl = slice(h * LANES, (h + 1) * LANES)
        betab_ref[:, sl] = jnp.broadcast_to(beta[:, h:h + 1], (tm, LANES))
        gcb_ref[:, sl] = jnp.broadcast_to(gc[:, DN_HEADS + h:DN_HEADS + h + 1], (tm, LANES))


def _dn_gates(x2, w_ba, a_log, dt_bias):
    t, d = x2.shape
    tm = min(512, t)
    w_pad = jnp.zeros((d, LANES), F32).at[:, :2 * DN_HEADS].set(w_ba)
    w_t = w_ba.T
    neg_a = -jnp.exp(a_log.astype(F32))
    prm = jnp.zeros((SUBLANES, LANES), F32)
    prm = prm.at[0, DN_HEADS:2 * DN_HEADS].set(neg_a).at[1, DN_HEADS:2 * DN_HEADS].set(dt_bias.astype(F32))
    prmt = jnp.zeros((2 * DN_HEADS, LANES), F32)
    prmt = prmt.at[DN_HEADS:, 0].set(neg_a).at[DN_HEADS:, 1].set(dt_bias.astype(F32))
    return pl.pallas_call(
        _gates_kernel,
        grid=(t // tm,),
        in_specs=[pl.BlockSpec((tm, d), lambda i: (i, 0)),
                  pl.BlockSpec((d, LANES), lambda i: (0, 0)),
                  pl.BlockSpec((2 * DN_HEADS, d), lambda i: (0, 0)),
                  pl.BlockSpec((SUBLANES, LANES), lambda i: (0, 0)),
                  pl.BlockSpec((2 * DN_HEADS, LANES), lambda i: (0, 0))],
        out_specs=[pl.BlockSpec((tm, DN_WIDTH), lambda i: (i, 0)),
                   pl.BlockSpec((tm, DN_WIDTH), lambda i: (i, 0)),
                   pl.BlockSpec((DN_HEADS, tm), lambda i: (0, i))],
        out_shape=[jax.ShapeDtypeStruct((t, DN_WIDTH), F32),
                   jax.ShapeDtypeStruct((t, DN_WIDTH), F32),
                   jax.ShapeDtypeStruct((DN_HEADS, t), F32)],
        compiler_params=_cparams(("parallel",)),
        name="dn_gates",
    )(x2, w_pad, w_t, prm, prmt)


def _dn_prep_kernel(q_ref, k_ref, v_ref, hq_ref, hk_ref, hv_ref, cq_ref, ck_ref, cv_ref,
                    betab_ref, gcb_ref, gct_ref,
                    u_ref, w_ref, qe_ref, kd_ref, intra_ref, egl_ref):
    i = pl.program_id(1)
    rows = q_ref.shape[0]
    c = DN_CHUNK
    halo_rows = hq_ref.shape[0]

    def conv_silu(x_ref, halo_ref, cw_ref):
        x = x_ref[...].astype(F32)
        halo = jnp.where(i > 0, halo_ref[...].astype(F32), 0.0)
        full = jnp.concatenate([halo, x], axis=0)
        cw = cw_ref[...]
        y = jnp.zeros((rows, LANES), F32)
        for j in range(DN_CONV):
            off = halo_rows - (DN_CONV - 1) + j
            y = y + cw[j:j + 1, :] * full[off:off + rows, :]
        return y * _sigmoid(y)

    def l2n(a):
        return a * lax.rsqrt(jnp.sum(a * a, axis=-1, keepdims=True) + L2_EPS)

    q = l2n(conv_silu(q_ref, hq_ref, cq_ref)) * (DN_HEAD_DIM ** -0.5)
    k = l2n(conv_silu(k_ref, hk_ref, ck_ref))
    v = conv_silu(v_ref, hv_ref, cv_ref)
    beta = betab_ref[...]
    gcol = gcb_ref[...]
    grow_all = gct_ref[...]

    ii = lax.broadcasted_iota(jnp.int32, (c, c), 0)
    jj = lax.broadcasted_iota(jnp.int32, (c, c), 1)
    eye = jnp.where(ii == jj, 1.0, 0.0)
    n_chunks = rows // c

    decays, a_mats = [], []
    kbs, egcs = [], []
    for n in range(n_chunks):
        rs = slice(n * c, (n + 1) * c)
        kc = k[rs]
        kb = kc * beta[rs]
        grow = grow_all[:, rs]
        diff = jnp.where(ii >= jj, gcol[rs, :c] - grow, 0.0)
        decay = jnp.where(ii >= jj, jnp.exp(diff), 0.0)
        kk = _dot_nt(kb.astype(BF16), kc.astype(BF16))
        a_mats.append(jnp.where(ii > jj, kk * decay, 0.0))
        decays.append(decay)
        kbs.append(kb)
        egcs.append(jnp.exp(gcol[rs]))

    invs = []
    for n in range(n_chunks):
        invs.append(eye - jnp.where((ii // 2 == jj // 2) & (ii > jj), a_mats[n], 0.0))
    s = 2
    while s < c:
        sel = ((ii // (2 * s)) == (jj // (2 * s))) & ((ii // s) > (jj // s))
        for n in range(n_chunks):
            d_b = invs[n].astype(BF16)
            x_b = jnp.where(sel, a_mats[n], 0.0).astype(BF16)
            t1 = _dot(d_b, x_b).astype(BF16)
            invs[n] = invs[n] - _dot(t1, d_b)
        s *= 2

    for n in range(n_chunks):
        rs = slice(n * c, (n + 1) * c)
        t_b = invs[n].astype(BF16)
        u_ref[rs, :] = _dot(t_b, (v[rs] * beta[rs]).astype(BF16)).astype(u_ref.dtype)
        w_ref[rs, :] = _dot(t_b, (kbs[n] * egcs[n]).astype(BF16)).astype(w_ref.dtype)
        qc = q[rs]
        kc = k[rs]
        qk = _dot_nt(qc.astype(BF16), kc.astype(BF16))
        intra = qk * decays[n]
        intra_ref[rs, :] = jnp.concatenate([intra, jnp.zeros_like(intra)], axis=1).astype(intra_ref.dtype)
        qe_ref[rs, :] = (qc * egcs[n]).astype(qe_ref.dtype)
        glast = gcol[(n + 1) * c - 1:(n + 1) * c, :]
        kd_ref[rs, :] = (kc * jnp.exp(glast - gcol[rs])).astype(kd_ref.dtype)
        egl_ref[n * SUBLANES:(n + 1) * SUBLANES, :] = jnp.broadcast_to(jnp.exp(glast),
                                                                       (SUBLANES, LANES))


def _dn_prep(proj, conv_w, betab, gcb, gct, t):
    rows = min(512, t)
    hb = BF16_SUBLANES
    nq = DN_WIDTH // LANES

    def blk(off):
        return pl.BlockSpec((rows, LANES), lambda h, i, off=off: (i, off + h))

    def halo(off):
        return pl.BlockSpec((hb, LANES),
                            lambda h, i, off=off: (jnp.maximum(i * (rows // hb) - 1, 0), off + h))

    def cw(off):
        return pl.BlockSpec((DN_CONV, LANES), lambda h, i, off=off: (0, off + h))

    per_head = pl.BlockSpec((rows, LANES), lambda h, i: (i, h))
    outs = [jax.ShapeDtypeStruct((t, DN_WIDTH), BF16)] * 5
    outs.append(jax.ShapeDtypeStruct((t // DN_CHUNK * SUBLANES, DN_WIDTH), F32))
    return pl.pallas_call(
        _dn_prep_kernel,
        grid=(DN_HEADS, t // rows),
        in_specs=[blk(0), blk(nq), blk(2 * nq), halo(0), halo(nq), halo(2 * nq),
                  cw(0), cw(nq), cw(2 * nq), per_head, per_head,
                  pl.BlockSpec((None, 1, rows), lambda h, i: (h, 0, i))],
        out_specs=[per_head] * 5 + [pl.BlockSpec((rows // DN_CHUNK * SUBLANES, LANES), lambda h, i: (i, h))],
        out_shape=outs,
        compiler_params=_cparams(("parallel", "parallel")),
        name="dn_prep",
    )(proj, proj, proj, proj, proj, proj, conv_w, conv_w, conv_w, betab, gcb,
      gct.reshape(DN_HEADS, 1, t))


def _dn_scan_kernel(u_ref, w_ref, qe_ref, kd_ref, intra_ref, egl_ref, z_ref, nw_ref, o_ref, s_ref):
    i = pl.program_id(0)
    c = DN_CHUNK
    n_chunks = u_ref.shape[0] // c

    @pl.when(i == 0)
    def _():
        s_ref[...] = jnp.zeros_like(s_ref)

    nw = nw_ref[...]

    def chunk(n, carry):
        r0 = pl.multiple_of(n * c, c)
        e0 = pl.multiple_of(n * SUBLANES, SUBLANES)
        for h in range(DN_HEADS):
            ls = slice(h * LANES, (h + 1) * LANES)
            s_h = s_ref[h]
            s_b = s_h.astype(BF16)
            wq = jnp.concatenate([w_ref[pl.ds(r0, c), ls], qe_ref[pl.ds(r0, c), ls]], axis=0)
            ws_qs = _dot(wq, s_b)
            v_new = u_ref[pl.ds(r0, c), ls].astype(F32) - ws_qs[:c]
            v_b = v_new.astype(BF16)
            intra = intra_ref[pl.ds(r0, c), ls][:, :c]
            o = ws_qs[c:] + _dot(intra, v_b)
            eg = egl_ref[pl.ds(e0, SUBLANES), ls]
            s_dec = (s_h.reshape(DN_HEAD_DIM // SUBLANES, SUBLANES, LANES) * eg[None]
                     ).reshape(DN_HEAD_DIM, LANES)
            s_ref[h] = s_dec + _dot_tn(kd_ref[pl.ds(r0, c), ls], v_b)
            z = z_ref[pl.ds(r0, c), ls].astype(F32)
            o = o * lax.rsqrt(jnp.mean(o * o, axis=-1, keepdims=True) + RMS_EPS) * nw * (z * _sigmoid(z))
            o_ref[pl.ds(r0, c), ls] = o.astype(o_ref.dtype)
        return carry

    lax.fori_loop(0, n_chunks, chunk, 0)


def _dn_scan(u, w, qe, kd, intra, egl, proj, norm_w, t):
    rows = min(512, t)
    full = pl.BlockSpec((rows, DN_WIDTH), lambda i: (i, 0))
    return pl.pallas_call(
        _dn_scan_kernel,
        grid=(t // rows,),
        in_specs=[full, full, full, full, full,
                  pl.BlockSpec((rows // DN_CHUNK * SUBLANES, DN_WIDTH), lambda i: (i, 0)),
                  pl.BlockSpec((rows, DN_WIDTH), lambda i: (i, 3)),
                  pl.BlockSpec((1, LANES), lambda i: (0, 0))],
        out_specs=full,
        out_shape=jax.ShapeDtypeStruct((t, DN_WIDTH), BF16),
        scratch_shapes=[pltpu.VMEM((DN_HEADS, DN_HEAD_DIM, DN_HEAD_DIM), F32)],
        compiler_params=_cparams(("arbitrary",)),
        name="dn_scan",
    )(u, w, qe, kd, intra, egl, proj, norm_w.reshape(1, LANES).astype(F32))


def _attn_kernel(q_ref, k_ref, v_ref, cos_ref, sin_ref, o_ref, lse_ref,
                 kbuf, vbuf, qbuf, obuf, lbuf, *, dilation, span):
    i = pl.program_id(1)
    bt, width = q_ref.shape
    d = dilation
    blk = ATT_BLOCK
    prev = d * blk
    half = ATT_HEAD_DIM // 2
    scale = ATT_HEAD_DIM ** -0.5

    heads = width // LANES

    @pl.when(i == 0)
    def _():
        kbuf[:, 0:prev, :] = jnp.zeros((heads, prev, LANES), F32)
        vbuf[:, 0:prev, :] = jnp.zeros((heads, prev, LANES), F32)

    cos, sin = cos_ref[...], sin_ref[...]

    def rope(a):
        a = a.astype(F32)
        return a * cos + pltpu.roll(a, half, 1) * sin

    for h in range(heads):
        ls = slice(h * LANES, (h + 1) * LANES)
        qbuf[h] = rope(q_ref[:, ls]) * scale
        kbuf[h, prev:prev + bt, :] = rope(k_ref[:, ls])
        vbuf[h, prev:prev + bt, :] = v_ref[:, ls].astype(F32)

    qi = lax.broadcasted_iota(jnp.int32, (blk, 2 * blk), 0) + blk
    ki = lax.broadcasted_iota(jnp.int32, (blk, 2 * blk), 1)
    dist = qi - ki
    band = (dist >= 0) & (dist <= span)
    band_first = band & ((ki >= blk) | (i > 0))

    def rows(start, size):
        return pl.ds(start, size, stride=d) if d > 1 else pl.ds(start, size)

    for j in range(bt // prev):
        for r in range(d):
            r0 = j * prev + r
            for h in range(heads):
                qj = qbuf[h, rows(r0, blk), :].astype(BF16)
                kj = kbuf[h, rows(r0, 2 * blk), :].astype(BF16)
                vj = vbuf[h, rows(r0, 2 * blk), :].astype(BF16)
                s = _dot_nt(qj, kj)
                s = jnp.where(band_first if j == 0 else band, s, NEG_BIG)
                m = jnp.max(s, axis=-1, keepdims=True)
                p = jnp.exp(s - m)
                l = jnp.sum(p, axis=-1, keepdims=True)
                obuf[h, rows(r0, blk), :] = _dot(p.astype(BF16), vj) / l
                lbuf[h, rows(r0, blk), :] = jnp.broadcast_to(m + jnp.log(l), (blk, LANES))

    for h in range(heads):
        ls = slice(h * LANES, (h + 1) * LANES)
        o_ref[:, ls] = obuf[h].astype(o_ref.dtype)
        lse_ref[:, ls] = lbuf[h]
    kbuf[:, 0:prev, :] = kbuf[:, bt:bt + prev, :]
    vbuf[:, 0:prev, :] = vbuf[:, bt:bt + prev, :]


def _attn_group(proj, cos_t, sin_t, gi, dilation, span, t):
    prev = dilation * ATT_BLOCK
    bt = max(min(512, t), prev)
    width = 2 * LANES
    gw = ATT_GROUP_WIDTH
    per_g = gw // width
    q0 = (4 * DN_WIDTH + gi * gw) // width
    k0 = q0 + ATT_WIDTH // width
    v0 = k0 + ATT_WIDTH // width

    def cur(c0):
        return pl.BlockSpec((bt, width), lambda hg, i, c0=c0: (i, c0 + hg))

    tab = pl.BlockSpec((bt, LANES), lambda hg, i: (i, 0))
    out = pl.BlockSpec((bt, width), lambda hg, i: (i, hg))
    return pl.pallas_call(
        functools.partial(_attn_kernel, dilation=dilation, span=span),
        grid=(per_g, t // bt),
        in_specs=[cur(q0), cur(k0), cur(v0), tab, tab],
        out_specs=[out, out],
        out_shape=[jax.ShapeDtypeStruct((t, gw), BF16), jax.ShapeDtypeStruct((t, gw), F32)],
        scratch_shapes=[pltpu.VMEM((width // LANES, prev + bt, LANES), F32),
                        pltpu.VMEM((width // LANES, prev + bt, LANES), F32),
                        pltpu.VMEM((width // LANES, bt, LANES), F32),
                        pltpu.VMEM((width // LANES, bt, LANES), F32),
                        pltpu.VMEM((width // LANES, bt, LANES), F32)],
        compiler_params=_cparams(("parallel", "arbitrary")),
        name=f"attn_g{gi}",
    )(proj, proj, proj, cos_t, sin_t)


def _mix_kernel(odn_ref, o0_ref, o1_ref, o2_ref, l0_ref, l1_ref, l2_ref, wdn_ref, watt_ref,
                gdn_ref, gatt_ref, out_ref):
    l0, l1, l2 = l0_ref[...], l1_ref[...], l2_ref[...]
    m = jnp.maximum(jnp.maximum(l0, l1), l2)
    e0, e1, e2 = jnp.exp(l0 - m), jnp.exp(l1 - m), jnp.exp(l2 - m)
    o_att = (e0 * o0_ref[...].astype(F32) + e1 * o1_ref[...].astype(F32)
             + e2 * o2_ref[...].astype(F32)) / (e0 + e1 + e2)
    y_dn = _dot(odn_ref[...], wdn_ref[...])
    y_att = _dot(o_att.astype(BF16), watt_ref[...])
    merged = (_sigmoid(gdn_ref[...].astype(F32)) * y_dn
              + _sigmoid(gatt_ref[...].astype(F32)) * y_att)
    out_ref[...] = merged.astype(out_ref.dtype)


def _mix(o_dn, ogs, lses, w_dn, w_att, proj, t, d):
    tm = min(512, t)
    tn = ATT_GROUP_WIDTH
    g0 = (4 * DN_WIDTH + 3 * ATT_WIDTH) // tn
    row_dn = pl.BlockSpec((tm, DN_WIDTH), lambda j, i: (i, 0))
    row_g = pl.BlockSpec((tm, ATT_GROUP_WIDTH), lambda j, i: (i, 0))
    return pl.pallas_call(
        _mix_kernel,
        grid=(d // tn, t // tm),
        in_specs=[row_dn, row_g, row_g, row_g, row_g, row_g, row_g,
                  pl.BlockSpec((DN_WIDTH, tn), lambda j, i: (0, j)),
                  pl.BlockSpec((ATT_GROUP_WIDTH, tn), lambda j, i: (0, j)),
                  pl.BlockSpec((tm, tn), lambda j, i: (i, g0 + j)),
                  pl.BlockSpec((tm, tn), lambda j, i: (i, g0 + d // tn + j))],
        out_specs=pl.BlockSpec((tm, tn), lambda j, i: (i, j)),
        out_shape=jax.ShapeDtypeStruct((t, d), BF16),
        compiler_params=_cparams(("parallel", "parallel")),
        name="mix",
    )(o_dn, *ogs, *lses, w_dn, w_att, proj, proj)


def _layer_norm(y, g, b):
    mu = jnp.mean(y, axis=-1, keepdims=True)
    yc = y - mu
    var = jnp.mean(yc * yc, axis=-1, keepdims=True)
    return yc * lax.rsqrt(var + LN_EPS) * g + b


def _wo_ln_kernel(m_ref, w_ref, x_ref, g_ref, b_ref, x1_ref, x1p_ref, acc_ref, *, alpha):
    kk = pl.program_id(1)

    @pl.when(kk == 0)
    def _():
        acc_ref[...] = jnp.zeros_like(acc_ref)

    acc_ref[...] += _dot(m_ref[...], w_ref[...])

    @pl.when(kk == pl.num_programs(1) - 1)
    def _():
        y = _layer_norm(alpha * x_ref[...] + acc_ref[...], g_ref[...], b_ref[...])
        x1_ref[...] = y
        hd = y.shape[1] // 2
        x1p_ref[...] = _pack_halves(y[:, :hd], y[:, hd:])


def _wo_ln(merged, w_o, x2, g, b, t, d, alpha):
    tm = min(256, t)
    tk = min(1024, d)
    row = pl.BlockSpec((tm, d), lambda i, k: (i, 0))
    vec = pl.BlockSpec((1, d), lambda i, k: (0, 0))
    return pl.pallas_call(
        functools.partial(_wo_ln_kernel, alpha=alpha),
        grid=(t // tm, d // tk),
        in_specs=[pl.BlockSpec((tm, tk), lambda i, k: (i, k)),
                  pl.BlockSpec((tk, d), lambda i, k: (k, 0)),
                  row, vec, vec],
        out_specs=[row, pl.BlockSpec((tm, d // 2), lambda i, k: (i, 0))],
        out_shape=[jax.ShapeDtypeStruct((t, d), F32), jax.ShapeDtypeStruct((t, d // 2), jnp.uint32)],
        scratch_shapes=[pltpu.VMEM((tm, d), F32)],
        compiler_params=_cparams(("parallel", "arbitrary")),
        name="wo_ln1",
    )(merged, w_o, x2, g.reshape(1, d).astype(F32), b.reshape(1, d).astype(F32))


def _router_kernel(x_ref, w_ref, bias_ref, topi_ref, topw_ref, rank_ref, cnt_ref, run_ref):
    i = pl.program_id(0)
    tm = x_ref.shape[0]

    @pl.when(i == 0)
    def _():
        run_ref[...] = jnp.zeros_like(run_ref)

    xs = _split3(x_ref[...])
    ws = _split3(w_ref[...])
    logits = jnp.zeros((tm, N_EXPERTS), F32)
    for ia, ib in ((0, 0), (0, 1), (1, 0), (1, 1), (0, 2), (2, 0)):
        logits = logits + _dot(xs[ia], ws[ib])
    s = _sigmoid(logits)
    sc = s + bias_ref[...]
    lane = lax.broadcasted_iota(jnp.int32, (tm, N_EXPERTS), 1)
    grp = lane // GROUP_SIZE
    neg = -jnp.inf

    def first_argmax(v):
        m = jnp.max(v, axis=-1, keepdims=True)
        idx = jnp.min(jnp.where(v == m, lane, N_EXPERTS), axis=-1, keepdims=True)
        return m, idx

    gscore = []
    for gi in range(N_GROUPS):
        vg = jnp.where(grp == gi, sc, neg)
        m1, i1 = first_argmax(vg)
        m2 = jnp.max(jnp.where(lane == i1, neg, vg), axis=-1, keepdims=True)
        gscore.append(m1 + m2)
    emask = jnp.zeros((tm, N_EXPERTS), jnp.bool_)
    for gi in range(N_GROUPS):
        ahead = jnp.zeros((tm, 1), jnp.int32)
        for gj in range(N_GROUPS):
            if gj == gi:
                continue
            beats = (gscore[gj] > gscore[gi]) | ((gscore[gj] == gscore[gi]) & (gj < gi))
            ahead = ahead + beats.astype(jnp.int32)
        emask = emask | ((grp == gi) & (ahead < TOPK_GROUPS))
    masked = jnp.where(emask, sc, neg)

    sel =jnp.zeros((tm, N_EXPERTS), jnp.bool_)
    idxs, vals = [], []
    for _ in range(TOP_K):
        _, ik = first_argmax(masked)
        hit = lane == ik
        sel = sel | hit
        masked = jnp.where(hit, neg, masked)
        idxs.append(ik)
        vals.append(jnp.sum(jnp.where(hit, s, 0.0), axis=-1, keepdims=True))
    wsum = vals[0]
    for v in vals[1:]:
        wsum = wsum + v

    sel_b = jnp.where(sel, 1.0, 0.0).astype(BF16)
    strict = jnp.where(lax.broadcasted_iota(jnp.int32, (tm, tm), 1)
                       < lax.broadcasted_iota(jnp.int32, (tm, tm), 0), 1.0, 0.0).astype(BF16)
    rank_excl = run_ref[0:1, :] + _dot(strict, sel_b)
    run_new = run_ref[0:1, :] + jnp.sum(sel_b.astype(F32), axis=0, keepdims=True)
    run_ref[...] = jnp.broadcast_to(run_new, run_ref.shape)
    cnt_ref[...] = jnp.broadcast_to(run_new, cnt_ref.shape).astype(jnp.int32)

    topi = jnp.zeros((tm, N_EXPERTS), jnp.int32)
    topw = jnp.zeros((tm, N_EXPERTS), F32)
    rnk = jnp.zeros((tm, N_EXPERTS), F32)
    for kk in range(TOP_K):
        hit = lane == idxs[kk]
        rk = jnp.sum(jnp.where(hit, rank_excl, 0.0), axis=-1, keepdims=True)
        topi = jnp.where(lane == kk, idxs[kk], topi)
        topw = jnp.where(lane == kk, vals[kk] / wsum * ROUTED_SCALE, topw)
        rnk = jnp.where(lane == kk, rk, rnk)
    topi_ref[...] = topi
    topw_ref[...] = topw
    rank_ref[...] = rnk.astype(jnp.int32)


def _router(x1, w_router, bias, t, d):
    tm = min(512, t)
    row = pl.BlockSpec((tm, N_EXPERTS), lambda i: (i, 0))
    return pl.pallas_call(
        _router_kernel,
        grid=(t // tm,),
        in_specs=[pl.BlockSpec((tm, d), lambda i: (i, 0)),
                  pl.BlockSpec((d, N_EXPERTS), lambda i: (0, 0)),
                  pl.BlockSpec((1, N_EXPERTS), lambda i: (0, 0))],
        out_specs=[row, row, row, pl.BlockSpec((SUBLANES, N_EXPERTS), lambda i: (0, 0))],
        out_shape=[jax.ShapeDtypeStruct((t, N_EXPERTS), jnp.int32),
                   jax.ShapeDtypeStruct((t, N_EXPERTS), F32),
                   jax.ShapeDtypeStruct((t, N_EXPERTS), jnp.int32),
                   jax.ShapeDtypeStruct((SUBLANES, N_EXPERTS), jnp.int32)],
        scratch_shapes=[pltpu.VMEM((SUBLANES, N_EXPERTS), F32)],
        compiler_params=_cparams(("arbitrary",)),
        name="router",
    )(x1, w_router.astype(F32), bias.reshape(1, N_EXPERTS).astype(F32))


def _row_copy(src_ref, src_row, dst_ref, dst_row, sem):
    return pltpu.make_async_copy(src_ref.at[pl.ds(src_row, 1)], dst_ref.at[pl.ds(dst_row, 1)], sem)


def _dispatch_kernel(pos_ref, x_ref, xs_ref, sem):
    i = pl.program_id(0)
    tb = x_ref.shape[0]
    base = i * (tb * TOP_K)

    def start(r, carry):
        for kk in range(TOP_K):
            _row_copy(x_ref, r, xs_ref, pos_ref[base + r * TOP_K + kk], sem).start()
        return carry

    def wait(r, carry):
        for kk in range(TOP_K):
            _row_copy(x_ref, r, xs_ref, pos_ref[base + r * TOP_K + kk], sem).wait()
        return carry

    lax.fori_loop(0, tb, start, 0)
    lax.fori_loop(0, tb, wait, 0)


def _dispatch(pos_flat, x1, n_slots, t, d):
    tb = min(256, t)
    return pl.pallas_call(
        _dispatch_kernel,
        grid_spec=pltpu.PrefetchScalarGridSpec(
            num_scalar_prefetch=1,
            grid=(t // tb,),
            in_specs=[pl.BlockSpec((tb, d), lambda i, pos: (i, 0))],
            out_specs=pl.BlockSpec(memory_space=pl.ANY),
            scratch_shapes=[pltpu.SemaphoreType.DMA],
        ),
        out_shape=jax.ShapeDtypeStruct((n_slots, d), x1.dtype),
        compiler_params=_cparams(("arbitrary",)),
        name="dispatch",
    )(pos_flat, x1)


def _expert_kernel(be_ref, nv_ref, nu_ref, xs_ref, wgu_ref, wdn_ref, y_ref):
    b = pl.program_id(0)

    @pl.when(b < nu_ref[0])
    def _():
        rows, hd = xs_ref.shape
        kc = min(EXPERT_CHUNK, hd)
        rid = lax.broadcasted_iota(jnp.int32, (rows, 1), 0)
        valid = rid < nv_ref[b]
        lo, hi = _unpack_halves(xs_ref[...])
        x_lo = jnp.where(valid, lo, 0.0).astype(BF16)
        x_hi = jnp.where(valid, hi, 0.0).astype(BF16)
        hgu = jnp.zeros((rows, 2 * EXPERT_FF), F32)
        for c0 in range(0, hd, kc):
            hgu = hgu + _dot(x_lo[:, c0:c0 + kc], wgu_ref[c0:c0 + kc, :].astype(BF16))
            hgu = hgu + _dot(x_hi[:, c0:c0 + kc], wgu_ref[hd + c0:hd + c0 + kc, :].astype(BF16))
        gate, up = hgu[:, :EXPERT_FF], hgu[:, EXPERT_FF:]
        act = (gate * _sigmoid(gate) * up).astype(BF16)
        for c0 in range(0, hd, kc):
            y_lo = _dot(act, wdn_ref[:, c0:c0 + kc].astype(BF16))
            y_hi = _dot(act, wdn_ref[:, hd + c0:hd + c0 + kc].astype(BF16))
            y_ref[:, c0:c0 + kc] = _pack_halves(y_lo, y_hi)


def _experts(blk_expert, blk_valid, n_used, xs, w_gu, w_dn, d):
    n_slots = xs.shape[0]
    nb = n_slots // MOE_ROWS

    def row_map(b, be, nv, nu):
        return (jnp.minimum(b, nu[0] - 1), 0)

    def w_map(b, be, nv, nu):
        return (be[jnp.minimum(b, nu[0] - 1)], 0, 0)

    return pl.pallas_call(
        _expert_kernel,
        grid_spec=pltpu.PrefetchScalarGridSpec(
            num_scalar_prefetch=3,
            grid=(nb,),
            in_specs=[pl.BlockSpec((MOE_ROWS, d // 2), row_map),
                      pl.BlockSpec((None, d, 2 * EXPERT_FF), w_map),
                      pl.BlockSpec((None, EXPERT_FF, d), w_map)],
            out_specs=pl.BlockSpec((MOE_ROWS, d // 2), row_map),
        ),
        out_shape=jax.ShapeDtypeStruct((n_slots, d // 2), jnp.uint32),
        compiler_params=_cparams(("arbitrary",), EXPERT_VMEM_LIMIT),
        name="experts",
    )(blk_expert, blk_valid, n_used, xs, w_gu, w_dn)


def _shared_kernel(x_ref, wgu_ref, wdn_ref, y_ref):
    hd = x_ref.shape[1]
    lo, hi = _unpack_halves(x_ref[...])
    hgu = _dot(lo.astype(BF16), wgu_ref[:hd, :]) + _dot(hi.astype(BF16), wgu_ref[hd:, :])
    gate, up = hgu[:, :EXPERT_FF], hgu[:, EXPERT_FF:]
    act = (gate * _sigmoid(gate) * up).astype(BF16)
    y_ref[...] = _dot(act, wdn_ref[...]).astype(y_ref.dtype)


def _shared(x1p, w_gu, w_dn, t, d):
    tm = min(512, t)
    return pl.pallas_call(
        _shared_kernel,
        grid=(t // tm,),
        in_specs=[pl.BlockSpec((tm, d // 2), lambda i: (i, 0)),
                  pl.BlockSpec((d, 2 * EXPERT_FF), lambda i: (0, 0)),
                  pl.BlockSpec((EXPERT_FF, d), lambda i: (0, 0))],
        out_specs=pl.BlockSpec((tm, d), lambda i: (i, 0)),
        out_shape=jax.ShapeDtypeStruct((t, d), BF16),
        compiler_params=_cparams(("parallel",)),
        name="shared",
    )(x1p, w_gu, w_dn)


def _combine_kernel(pos_ref, x_ref, ysh_ref, topw_ref, g_ref, b_ref, y_hbm, out_ref, buf_ref, sem,
                    *, alpha):
    i = pl.program_id(0)
    n = pl.num_programs(0)
    tb = x_ref.shape[0]

    def gather(tile, slot, do_start):
        base = tile * (tb * TOP_K)

        def body(r, carry):
            for kk in range(TOP_K):
                cp = _row_copy(y_hbm, pos_ref[base + r * TOP_K + kk], buf_ref.at[slot, kk], r,
                               sem.at[slot])
                if do_start:
                    cp.start()
                else:
                    cp.wait()
            return carry

        lax.fori_loop(0, tb, body, 0)

    @pl.when(i == 0)
    def _():
        gather(0, 0, True)

    @pl.when(i + 1 < n)
    def _():
        gather(i + 1, (i + 1) % 2, True)

    slot = i % 2
    gather(i, slot, False)
    topw = topw_ref[...]
    hd = x_ref.shape[1] // 2
    acc_lo = alpha * x_ref[:, :hd] + ysh_ref[:, :hd].astype(F32)
    acc_hi = alpha * x_ref[:, hd:] + ysh_ref[:, hd:].astype(F32)
    for kk in range(TOP_K):
        lo, hi = _unpack_halves(buf_ref[slot, kk])
        acc_lo = acc_lo + topw[:, kk:kk + 1] * lo
        acc_hi = acc_hi + topw[:, kk:kk + 1] * hi
    inv_d = 1.0 / (2 * hd)
    mu = (jnp.sum(acc_lo, axis=-1, keepdims=True) + jnp.sum(acc_hi, axis=-1, keepdims=True)) * inv_d
    c_lo, c_hi = acc_lo - mu, acc_hi - mu
    var = (jnp.sum(c_lo * c_lo, axis=-1, keepdims=True)
           + jnp.sum(c_hi * c_hi, axis=-1, keepdims=True)) * inv_d
    rstd = lax.rsqrt(var + LN_EPS)
    out_ref[:, :hd] = c_lo * rstd * g_ref[:, :hd] + b_ref[:, :hd]
    out_ref[:, hd:] = c_hi * rstd * g_ref[:, hd:] + b_ref[:, hd:]


def _combine(pos_flat, x1, ysh, topw, g, b, y, t, d, alpha):
    tb = min(128, t)
    row = pl.BlockSpec((tb, d), lambda i, pos: (i, 0))
    vec = pl.BlockSpec((1, d), lambda i, pos: (0, 0))
    return pl.pallas_call(
        functools.partial(_combine_kernel, alpha=alpha),
        grid_spec=pltpu.PrefetchScalarGridSpec(
            num_scalar_prefetch=1,
            grid=(t // tb,),
            in_specs=[row, row, pl.BlockSpec((tb, N_EXPERTS), lambda i, pos: (i, 0)), vec, vec,
                      pl.BlockSpec(memory_space=pl.ANY)],
            out_specs=row,
            scratch_shapes=[pltpu.VMEM((2, TOP_K, tb, d // 2), jnp.uint32),
                            pltpu.SemaphoreType.DMA((2,))],
        ),
        out_shape=jax.ShapeDtypeStruct((t, d), F32),
        compiler_params=_cparams(("arbitrary",)),
        name="combine",
    )(pos_flat, x1, ysh, topw, g.reshape(1, d).astype(F32), b.reshape(1, d).astype(F32), y)


def _rope_tables(t):
    half = ATT_HEAD_DIM // 2
    inv_freq = ROPE_THETA ** (-jnp.arange(half, dtype=F32) / half)
    ang = jnp.arange(t, dtype=F32)[:, None] * inv_freq[None, :]
    cos, sin = jnp.cos(ang), jnp.sin(ang)
    return jnp.concatenate([cos, cos], -1), jnp.concatenate([-sin, sin], -1)


def _token_mixer(x2, w_in, conv_w, a_log, dt_bias, dn_norm_w, w_dn_branch, w_att_branch, w_o,
                 ln_g, ln_b, alpha):
    t, d = x2.shape
    n_main = 4 * DN_WIDTH
    n_ba = 2 * DN_HEADS
    w_main = jnp.concatenate([w_in[:, :n_main], w_in[:, n_main + n_ba:]], axis=1).astype(BF16)
    w_ba = w_in[:, n_main:n_main + n_ba].astype(F32)

    proj = _matmul(x2.astype(BF16), w_main, 1024, 1024, BF16)
    betab, gcb, gct = _dn_gates(x2, w_ba, a_log, dt_bias)
    u, w, qe, kd, intra, egl = _dn_prep(proj, conv_w.astype(F32), betab, gcb, gct, t)
    o_dn = _dn_scan(u, w, qe, kd, intra, egl, proj, dn_norm_w, t)

    cos_t, sin_t = _rope_tables(t)
    ogs, lses = [], []
    for gi, (window, dilation) in enumerate(ATT_GROUPS):
        o_g, lse_g = _attn_group(proj, cos_t, sin_t, gi, dilation, window // dilation, t)
        ogs.append(o_g)
        lses.append(lse_g)

    merged = _mix(o_dn, ogs, lses, w_dn_branch.astype(BF16), w_att_branch.astype(BF16), proj, t, d)
    return _wo_ln(merged, w_o.astype(BF16), x2, ln_g, ln_b, t, d, alpha)


def _moe(x1, x1p, w_router, router_bias, w_exp_gate_up, w_exp_down, w_sh_gate_up, w_sh_down,
         ln_g, ln_b, alpha):
    t, d = x1.shape
    r = MOE_ROWS
    topi, topw, rank, cnt = _router(x1, w_router, router_bias, t, d)
    counts = cnt[0]
    padded = (counts + r - 1) // r * r
    pad_end = jnp.cumsum(padded)
    pad_start = pad_end - padded
    nb = (t * TOP_K) // r + N_EXPERTS
    blk_row0 = jnp.arange(nb, dtype=jnp.int32) * r
    in_blk = (blk_row0[:, None] >= pad_start[None, :]) & (blk_row0[:, None] < pad_end[None, :])
    expert_ids = jnp.arange(N_EXPERTS, dtype=jnp.int32)
    blk_expert = jnp.where(blk_row0 < pad_end[-1],
                           jnp.sum(jnp.where(in_blk, expert_ids[None, :], 0), axis=1),
                           N_EXPERTS - 1).astype(jnp.int32)
    blk_valid = jnp.sum(jnp.where(in_blk, jnp.clip(counts[None, :] - (blk_row0[:, None] - pad_start[None, :]), 0, r), 0),
                        axis=1).astype(jnp.int32)
    n_used = (pad_end[-1:] // r).astype(jnp.int32)
    sel_start = jnp.sum(jnp.where(topi[:, :TOP_K, None] == expert_ids[None, None, :],
                                  pad_start[None, None, :], 0), axis=-1)
    pos_flat = (sel_start + rank[:, :TOP_K]).astype(jnp.int32).reshape(-1)

    xs = _dispatch(pos_flat, x1p, nb * r, t, d // 2)
    y = _experts(blk_expert, blk_valid, n_used, xs, w_exp_gate_up, w_exp_down, d)
    ysh = _shared(x1p, w_sh_gate_up.astype(BF16), w_sh_down.astype(BF16), t, d)
    return _combine(pos_flat, x1, ysh, topw, ln_g, ln_b, y, t, d, alpha)


def kernel(x, w_in, conv_w, a_log, dt_bias, dn_norm_w, w_dn_branch, w_att_branch, w_o, ln1_g, ln1_b,
           w_router, router_bias, w_exp_gate_up, w_exp_down, w_sh_gate_up, w_sh_down, ln2_g, ln2_b):
    bsz, t, d = x.shape
    depth = w_in.shape[0]
    alpha = (2.0 * depth) ** 0.25
    outs = []
    for bi in range(bsz):
        xb = x[bi]
        for l in range(depth):
            x1, x1p = _token_mixer(xb, w_in[l], conv_w[l], a_log[l], dt_bias[l], dn_norm_w[l],
                                   w_dn_branch[l], w_att_branch[l], w_o[l], ln1_g[l], ln1_b[l],
                                   alpha)
            xb = _moe(x1, x1p, w_router[l], router_bias[l], w_exp_gate_up[l], w_exp_down[l],
                      w_sh_gate_up[l], w_sh_down[l], ln2_g[l], ln2_b[l], alpha)
        outs.append(xb)
    return jnp.stack(outs, 0)
```

```python
import functools

import jax
import jax.numpy as jnp
from jax import lax
from jax.experimental import pallas as pl
from jax.experimental.pallas import tpu as pltpu

F32 = jnp.float32
BF16 = jnp.bfloat16

DN_HEADS = 16
DN_HEAD_DIM = 128
DN_WIDTH = DN_HEADS * DN_HEAD_DIM
DN_CONV = 4
DN_CHUNK = 64
ATT_GROUPS = ((128, 1), (512, 4), (2048, 16))
ATT_HEADS_PER_GROUP = 8
ATT_HEAD_DIM = 128
ATT_GROUP_WIDTH = ATT_HEADS_PER_GROUP * ATT_HEAD_DIM
ATT_WIDTH = len(ATT_GROUPS) * ATT_GROUP_WIDTH
ATT_BLOCK = 128
ROPE_THETA = 10000.0
N_EXPERTS = 128
TOP_K = 8
N_GROUPS = 8
GROUP_SIZE = N_EXPERTS // N_GROUPS
TOPK_GROUPS = 4
EXPERT_FF = 384
ROUTED_SCALE = 2.5
LN_EPS = 1e-5
RMS_EPS = 1e-6
L2_EPS = 1e-6

LANES = 128
SUBLANES = 8
BF16_SUBLANES = 16
VMEM_LIMIT = 56 * 1024 * 1024

MOE_ROWS = 256
EXPERT_CHUNK = 1024
NEG_BIG = -1e30


def _cparams(sem, vmem=VMEM_LIMIT):
    return pltpu.CompilerParams(dimension_semantics=sem, vmem_limit_bytes=vmem)


def _split3(a):
    hi = a.astype(BF16)
    r1 = a - hi.astype(F32)
    mid = r1.astype(BF16)
    lo = (r1 - mid.astype(F32)).astype(BF16)
    return hi, mid, lo


def _split2(a):
    hi = a.astype(BF16)
    return hi, (a - hi.astype(F32)).astype(BF16)


def _dot(a, b):
    return jnp.dot(a, b, preferred_element_type=F32)


def _dot_nt(a, b):
    return lax.dot_general(a, b, (((1,), (1,)), ((), ())), preferred_element_type=F32)


def _dot_tn(a, b):
    return lax.dot_general(a, b, (((0,), (0,)), ((), ())), preferred_element_type=F32)


def _sigmoid(x):
    return 1.0 / (1.0 + jnp.exp(-x))


def _pack_halves(lo, hi):
    lo_b = lax.bitcast_convert_type(lo.astype(BF16).astype(F32), jnp.uint32)
    hi_b = lax.bitcast_convert_type(hi.astype(BF16).astype(F32), jnp.uint32)
    return (hi_b & jnp.uint32(0xFFFF0000)) | (lo_b >> 16)


def _unpack_halves(u):
    lo = lax.bitcast_convert_type(u << 16, F32)
    hi = lax.bitcast_convert_type(u & jnp.uint32(0xFFFF0000), F32)
    return lo, hi


def _mm_kernel(a_ref, b_ref, o_ref):
    o_ref[...] = _dot(a_ref[...], b_ref[...]).astype(o_ref.dtype)


def _matmul(a, b, tm, tn, out_dtype):
    m, k = a.shape
    n = b.shape[1]
    tm, tn = min(tm, m), min(tn, n)
    return pl.pallas_call(
        _mm_kernel,
        grid=(m // tm, n // tn),
        in_specs=[pl.BlockSpec((tm, k), lambda i, j: (i, 0)),
                  pl.BlockSpec((k, tn), lambda i, j: (0, j))],
        out_specs=pl.BlockSpec((tm, tn), lambda i, j: (i, j)),
        out_shape=jax.ShapeDtypeStruct((m, n), out_dtype),
        compiler_params=_cparams(("parallel", "parallel")),
        name="proj",
    )(a, b)


def _gates_kernel(x_ref, w_ref, prm_ref, betab_ref, gcb_ref, gct_ref):
    tm = x_ref.shape[0]
    x_hi, x_mid = _split2(x_ref[...])
    w_hi, w_mid = _split2(w_ref[...])
    logits = _dot(x_hi, w_hi) + _dot(x_hi, w_mid) + _dot(x_mid, w_hi)

    def softplus(v):
        return jnp.maximum(v, 0.0) + jnp.log(1.0 + jnp.exp(-jnp.abs(v)))

    prm = prm_ref[...]
    beta = _sigmoid(logits)
    g = prm[0:1, :] * softplus(logits + prm[1:2, :])

    ri = lax.broadcasted_iota(jnp.int32, (tm, tm), 0)
    ci = lax.broadcasted_iota(jnp.int32, (tm, tm), 1)
    same = (ri // DN_CHUNK) == (ci // DN_CHUNK)
    lower = jnp.where(same & (ci <= ri), 1.0, 0.0).astype(BF16)
    gc = jnp.zeros((tm, LANES), F32)
    for part in _split3(g):
        gc = gc + _dot(lower, part)
    gct_ref[...] = gc.T[DN_HEADS:2 * DN_HEADS, :]
    for h in range(DN_HEADS):
        sl = slice(h * LANES, (h + 1) * LANES)
        betab_ref[:, sl] = jnp.broadcast_to(beta[:, h:h + 1], (tm, LANES))
        gcb_ref[:, sl] = jnp.broadcast_to(gc[:, DN_HEADS + h:DN_HEADS + h + 1], (tm, LANES))


def _dn_gates(x2, w_ba, a_log, dt_bias):
    t, d = x2.shape
    tm = min(512, t)
    w_pad = jnp.zeros((d, LANES), F32).at[:, :2 * DN_HEADS].set(w_ba)
    neg_a = -jnp.exp(a_log.astype(F32))
    prm = jnp.zeros((SUBLANES, LANES), F32)
    prm = prm.at[0, DN_HEADS:2 * DN_HEADS].set(neg_a).at[1, DN_HEADS:2 * DN_HEADS].set(dt_bias.astype(F32))
    return pl.pallas_call(
        _gates_kernel,
        grid=(t // tm,),
        in_specs=[pl.BlockSpec((tm, d), lambda i: (i, 0)),
                  pl.BlockSpec((d, LANES), lambda i: (0, 0)),
                  pl.BlockSpec((SUBLANES, LANES), lambda i: (0, 0))],
        out_specs=[pl.BlockSpec((tm, DN_WIDTH), lambda i: (i, 0)),
                   pl.BlockSpec((tm, DN_WIDTH), lambda i: (i, 0)),
                   pl.BlockSpec((DN_HEADS, tm), lambda i: (0, i))],
        out_shape=[jax.ShapeDtypeStruct((t, DN_WIDTH), F32),
                   jax.ShapeDtypeStruct((t, DN_WIDTH), F32),
                   jax.ShapeDtypeStruct((DN_HEADS, t), F32)],
        compiler_params=_cparams(("parallel",)),
        name="dn_gates",
    )(x2, w_pad, prm)


def _dn_prep_kernel(q_ref, k_ref, v_ref, hq_ref, hk_ref, hv_ref, cq_ref, ck_ref, cv_ref,
                    betab_ref, gcb_ref, gct_ref,
                    u_ref, w_ref, qe_ref, kd_ref, intra_ref, egl_ref):
    i = pl.program_id(1)
    rows = q_ref.shape[0]
    c = DN_CHUNK
    halo_rows = hq_ref.shape[0]

    def conv_silu(x_ref, halo_ref, cw_ref):
        x = x_ref[...].astype(F32)
        halo = jnp.where(i > 0, halo_ref[...].astype(F32), 0.0)
        full = jnp.concatenate([halo, x], axis=0)
        cw = cw_ref[...]
        y = jnp.zeros((rows, LANES), F32)
        for j in range(DN_CONV):
            off = halo_rows - (DN_CONV - 1) + j
            y = y + cw[j:j + 1, :] * full[off:off + rows, :]
        return y * _sigmoid(y)

    def l2n(a):
        return a * lax.rsqrt(jnp.sum(a * a, axis=-1, keepdims=True) + L2_EPS)

    q = l2n(conv_silu(q_ref, hq_ref, cq_ref)) * (DN_HEAD_DIM ** -0.5)
    k = l2n(conv_silu(k_ref, hk_ref, ck_ref))
    v = conv_silu(v_ref, hv_ref, cv_ref)
    beta = betab_ref[...]
    gcol = gcb_ref[...]
    grow_all = gct_ref[...]

    ii = lax.broadcasted_iota(jnp.int32, (c, c), 0)
    jj = lax.broadcasted_iota(jnp.int32, (c, c), 1)
    eye = jnp.where(ii == jj, 1.0, 0.0)
    n_chunks = rows // c

    decays, a_mats = [], []
    kbs, egcs = [], []
    for n in range(n_chunks):
        rs = slice(n * c, (n + 1) * c)
        kc = k[rs]
        kb = kc * beta[rs]
        grow = grow_all[:, rs]
        diff = jnp.where(ii >= jj, gcol[rs, :c] - grow, 0.0)
        decay = jnp.where(ii >= jj, jnp.exp(diff), 0.0)
        kk = _dot_nt(kb.astype(BF16), kc.astype(BF16))
        a_mats.append(jnp.where(ii > jj, kk * decay, 0.0))
        decays.append(decay)
        kbs.append(kb)
        egcs.append(jnp.exp(gcol[rs]))

    invs = []
    for n in range(n_chunks):
        invs.append(eye - jnp.where((ii // 2 == jj // 2) & (ii > jj), a_mats[n], 0.0))
    s = 2
    while s < c:
        sel = ((ii // (2 * s)) == (jj // (2 * s))) & ((ii // s) > (jj // s))
        for n in range(n_chunks):
            d_b = invs[n].astype(BF16)
            x_b = jnp.where(sel, a_mats[n], 0.0).astype(BF16)
            t1 = _dot(d_b, x_b).astype(BF16)
            invs[n] = invs[n] - _dot(t1, d_b)
        s *= 2

    for n in range(n_chunks):
        rs = slice(n * c, (n + 1) * c)
        t_b = invs[n].astype(BF16)
        u_ref[rs, :] = _dot(t_b, (v[rs] * beta[rs]).astype(BF16)).astype(u_ref.dtype)
        w_ref[rs, :] = _dot(t_b, (kbs[n] * egcs[n]).astype(BF16)).astype(w_ref.dtype)
        qc = q[rs]
        kc = k[rs]
        qk = _dot_nt(qc.astype(BF16), kc.astype(BF16))
        intra = qk * decays[n]
        intra_ref[rs, :] = jnp.concatenate([intra, jnp.zeros_like(intra)], axis=1).astype(intra_ref.dtype)
        qe_ref[rs, :] = (qc * egcs[n]).astype(qe_ref.dtype)
        glast = gcol[(n + 1) * c - 1:(n + 1) * c, :]
        kd_ref[rs, :] = (kc * jnp.exp(glast - gcol[rs])).astype(kd_ref.dtype)
        egl_ref[n * SUBLANES:(n + 1) * SUBLANES, :] = jnp.broadcast_to(jnp.exp(glast),
                                                                       (SUBLANES, LANES))


def _dn_prep(proj, conv_w, betab, gcb, gct, t):
    rows = min(512, t)
    hb = BF16_SUBLANES
    nq = DN_WIDTH // LANES

    def blk(off):
        return pl.BlockSpec((rows, LANES), lambda h, i, off=off: (i, off + h))

    def halo(off):
        return pl.BlockSpec((hb, LANES),
                            lambda h, i, off=off: (jnp.maximum(i * (rows // hb) - 1, 0), off + h))

    def cw(off):
        return pl.BlockSpec((DN_CONV, LANES), lambda h, i, off=off: (0, off + h))

    per_head = pl.BlockSpec((rows, LANES), lambda h, i: (i, h))
    outs = [jax.ShapeDtypeStruct((t, DN_WIDTH), BF16)] * 5
    outs.append(jax.ShapeDtypeStruct((t // DN_CHUNK * SUBLANES, DN_WIDTH), F32))
    return pl.pallas_call(
        _dn_prep_kernel,
        grid=(DN_HEADS, t // rows),
        in_specs=[blk(0), blk(nq), blk(2 * nq), halo(0), halo(nq), halo(2 * nq),
                  cw(0), cw(nq), cw(2 * nq), per_head, per_head,
                  pl.BlockSpec((None, 1, rows), lambda h, i: (h, 0, i))],
        out_specs=[per_head] * 5 + [pl.BlockSpec((rows // DN_CHUNK * SUBLANES, LANES), lambda h, i: (i, h))],
        out_shape=outs,
        compiler_params=_cparams(("parallel", "parallel")),
        name="dn_prep",
    )(proj, proj, proj, proj, proj, proj, conv_w, conv_w, conv_w, betab, gcb,
      gct.reshape(DN_HEADS, 1, t))


def _dn_scan_kernel(u_ref, w_ref, qe_ref, kd_ref, intra_ref, egl_ref, z_ref, nw_ref, o_ref, s_ref):
    i = pl.program_id(0)
    c = DN_CHUNK
    n_chunks = u_ref.shape[0] // c

    @pl.when(i == 0)
    def _():
        s_ref[...] = jnp.zeros_like(s_ref)

    nw = nw_ref[...]

    def chunk(n, carry):
        r0 = pl.multiple_of(n * c, c)
        e0 = pl.multiple_of(n * SUBLANES, SUBLANES)
        for h in range(DN_HEADS):
            ls = slice(h * LANES, (h + 1) * LANES)
            s_h = s_ref[h]
            s_b = s_h.astype(BF16)
            wq = jnp.concatenate([w_ref[pl.ds(r0, c), ls], qe_ref[pl.ds(r0, c), ls]], axis=0)
            ws_qs = _dot(wq, s_b)
            v_new = u_ref[pl.ds(r0, c), ls].astype(F32) - ws_qs[:c]
            v_b = v_new.astype(BF16)
            intra = intra_ref[pl.ds(r0, c), ls][:, :c]
            o = ws_qs[c:] + _dot(intra, v_b)
            eg = egl_ref[pl.ds(e0, SUBLANES), ls]
            s_dec = (s_h.reshape(DN_HEAD_DIM // SUBLANES, SUBLANES, LANES) * eg[None]
                     ).reshape(DN_HEAD_DIM, LANES)
            s_ref[h] = s_dec + _dot_tn(kd_ref[pl.ds(r0, c), ls], v_b)
            z = z_ref[pl.ds(r0, c), ls].astype(F32)
            o = o * lax.rsqrt(jnp.mean(o * o, axis=-1, keepdims=True) + RMS_EPS) * nw * (z * _sigmoid(z))
            o_ref[pl.ds(r0, c), ls] = o.astype(o_ref.dtype)
        return carry

    lax.fori_loop(0, n_chunks, chunk, 0)


def _dn_scan(u, w, qe, kd, intra, egl, proj, norm_w, t):
    rows = min(512, t)
    full = pl.BlockSpec((rows, DN_WIDTH), lambda i: (i, 0))
    return pl.pallas_call(
        _dn_scan_kernel,
        grid=(t // rows,),
        in_specs=[full, full, full, full, full,
                  pl.BlockSpec((rows // DN_CHUNK * SUBLANES, DN_WIDTH), lambda i: (i, 0)),
                  pl.BlockSpec((rows, DN_WIDTH), lambda i: (i, 3)),
                  pl.BlockSpec((1, LANES), lambda i: (0, 0))],
        out_specs=full,
        out_shape=jax.ShapeDtypeStruct((t, DN_WIDTH), BF16),
        scratch_shapes=[pltpu.VMEM((DN_HEADS, DN_HEAD_DIM, DN_HEAD_DIM), F32)],
        compiler_params=_cparams(("arbitrary",)),
        name="dn_scan",
    )(u, w, qe, kd, intra, egl, proj, norm_w.reshape(1, LANES).astype(F32))


def _attn_kernel(*refs, dilation, span, n_earlier):
    q_ref, k_ref, v_ref, cos_ref, sin_ref = refs[:5]
    earlier = refs[5:5 + 2 * n_earlier]
    out_refs = refs[5 + 2 * n_earlier:-5]
    kbuf, vbuf, qbuf, obuf, lbuf = refs[-5:]
    i = pl.program_id(1)
    bt, width = q_ref.shape
    d = dilation
    blk = ATT_BLOCK
    prev = d * blk
    half = ATT_HEAD_DIM // 2
    scale = ATT_HEAD_DIM ** -0.5

    heads = width // LANES

    @pl.when(i == 0)
    def _():
        kbuf[:, 0:prev, :] = jnp.zeros((heads, prev, LANES), F32)
        vbuf[:, 0:prev, :] = jnp.zeros((heads, prev, LANES), F32)

    cos, sin = cos_ref[...], sin_ref[...]

    def rope(a):
        a = a.astype(F32)
        return a * cos + pltpu.roll(a, half, 1) * sin

    for h in range(heads):
        ls = slice(h * LANES, (h + 1) * LANES)
        qbuf[h] = rope(q_ref[:, ls]) * scale
        kbuf[h, prev:prev + bt, :] = rope(k_ref[:, ls])
        vbuf[h, prev:prev + bt, :] = v_ref[:, ls].astype(F32)

    qi = lax.broadcasted_iota(jnp.int32, (blk, 2 * blk), 0) + blk
    ki = lax.broadcasted_iota(jnp.int32, (blk, 2 * blk), 1)
    dist = qi - ki
    band = (dist >= 0) & (dist <= span)
    band_first = band & ((ki >= blk) | (i > 0))

    def rows(start, size):
        return pl.ds(start, size, stride=d) if d > 1 else pl.ds(start, size)

    for j in range(bt // prev):
        for r in range(d):
            r0 = j * prev + r
            for h in range(heads):
                qj = qbuf[h, rows(r0, blk), :].astype(BF16)
                kj = kbuf[h, rows(r0, 2 * blk), :].astype(BF16)
                vj = vbuf[h, rows(r0, 2 * blk), :].astype(BF16)
                s = _dot_nt(qj, kj)
                s = jnp.where(band_first if j == 0 else band, s, NEG_BIG)
                m = jnp.max(s, axis=-1, keepdims=True)
                p = jnp.exp(s - m)
                l = jnp.sum(p, axis=-1, keepdims=True)
                obuf[h, rows(r0, blk), :] = _dot(p.astype(BF16), vj) / l
                lbuf[h, rows(r0, blk), :] = jnp.broadcast_to(m + jnp.log(l), (blk, LANES))

    for h in range(heads):
        ls = slice(h * LANES, (h + 1) * LANES)
        if n_earlier == 0:
            o_ref, lse_ref = out_refs
            o_ref[:, ls] = obuf[h].astype(o_ref.dtype)
            lse_ref[:, ls] = lbuf[h]
        else:
            (o_ref,) = out_refs
            outs = [earlier[2 * g][:, ls].astype(F32) for g in range(n_earlier)] + [obuf[h]]
            lses = [earlier[2 * g + 1][:, ls] for g in range(n_earlier)] + [lbuf[h]]
            m = functools.reduce(jnp.maximum, lses)
            es = [jnp.exp(l - m) for l in lses]
            num = functools.reduce(lambda a, b: a + b, [e * o for e, o in zip(es, outs)])
            den = functools.reduce(lambda a, b: a + b, es)
            o_ref[:, ls] = (num / den).astype(o_ref.dtype)
    kbuf[:, 0:prev, :] = kbuf[:, bt:bt + prev, :]
    vbuf[:, 0:prev, :] = vbuf[:, bt:bt + prev, :]


def _attn_group(proj, cos_t, sin_t, gi, dilation, span, t, earlier=()):
    prev = dilation * ATT_BLOCK
    bt = max(min(512, t), prev)
    width = 2 * LANES
    gw = ATT_GROUP_WIDTH
    per_g = gw // width
    q0 = (4 * DN_WIDTH + gi * gw) // width
    k0 = q0 + ATT_WIDTH // width
    v0 = k0 + ATT_WIDTH // width

    def cur(c0):
        return pl.BlockSpec((bt, width), lambda hg, i, c0=c0: (i, c0 + hg))

    tab = pl.BlockSpec((bt, LANES), lambda hg, i: (i, 0))
    out = pl.BlockSpec((bt, width), lambda hg, i: (i, hg))
    flat_earlier = [a for pair in earlier for a in pair]
    if earlier:
        out_specs, out_shape = [out], [jax.ShapeDtypeStruct((t, gw), BF16)]
    else:
        out_specs = [out, out]
        out_shape = [jax.ShapeDtypeStruct((t, gw), BF16), jax.ShapeDtypeStruct((t, gw), F32)]
    return pl.pallas_call(
        functools.partial(_attn_kernel, dilation=dilation, span=span, n_earlier=len(earlier)),
        grid=(per_g, t // bt),
        in_specs=[cur(q0), cur(k0), cur(v0), tab, tab] + [out] * len(flat_earlier),
        out_specs=out_specs,
        out_shape=out_shape,
        scratch_shapes=[pltpu.VMEM((width // LANES, prev + bt, LANES), F32),
                        pltpu.VMEM((width // LANES, prev + bt, LANES), F32),
                        pltpu.VMEM((width // LANES, bt, LANES), F32),
                        pltpu.VMEM((width // LANES, bt, LANES), F32),
                        pltpu.VMEM((width // LANES, bt, LANES), F32)],
        compiler_params=_cparams(("parallel", "arbitrary")),
        name=f"attn_g{gi}",
    )(proj, proj, proj, cos_t, sin_t, *flat_earlier)


def _attention(proj, cos_t, sin_t, t):
    earlier = []
    for gi, (window, dilation) in enumerate(ATT_GROUPS[:-1]):
        earlier.append(_attn_group(proj, cos_t, sin_t, gi, dilation, window // dilation, t))
    window, dilation = ATT_GROUPS[-1]
    (o_att,) = _attn_group(proj, cos_t, sin_t, len(ATT_GROUPS) - 1, dilation, window // dilation, t,
                           earlier=tuple(earlier))
    return o_att


def _mix_kernel(odn_ref, oatt_ref, wdn_ref, watt_ref, gdn_ref, gatt_ref, out_ref):
    y_dn = _dot(odn_ref[...], wdn_ref[...])
    y_att = _dot(oatt_ref[...], watt_ref[...])
    merged = (_sigmoid(gdn_ref[...].astype(F32)) * y_dn
              + _sigmoid(gatt_ref[...].astype(F32)) * y_att)
    out_ref[...] = merged.astype(out_ref.dtype)


def _mix(o_dn, o_att, w_dn, w_att, proj, t, d):
    tm = min(1024, t)
    tn = ATT_GROUP_WIDTH
    g0 = (4 * DN_WIDTH + 3 * ATT_WIDTH) // tn
    row_dn = pl.BlockSpec((tm, DN_WIDTH), lambda j, i: (i, 0))
    row_g = pl.BlockSpec((tm, ATT_GROUP_WIDTH), lambda j, i: (i, 0))
    return pl.pallas_call(
        _mix_kernel,
        grid=(d // tn, t // tm),
        in_specs=[row_dn, row_g,
                  pl.BlockSpec((DN_WIDTH, tn), lambda j, i: (0, j)),
                  pl.BlockSpec((ATT_GROUP_WIDTH, tn), lambda j, i: (0, j)),
                  pl.BlockSpec((tm, tn), lambda j, i: (i, g0 + j)),
                  pl.BlockSpec((tm, tn), lambda j, i: (i, g0 + d // tn + j))],
        out_specs=pl.BlockSpec((tm, tn), lambda j, i: (i, j)),
        out_shape=jax.ShapeDtypeStruct((t, d), BF16),
        compiler_params=_cparams(("parallel", "parallel")),
        name="mix",
    )(o_dn, o_att, w_dn, w_att, proj, proj)


def _layer_norm(y, g, b):
    mu = jnp.mean(y, axis=-1, keepdims=True)
    yc = y - mu
    var = jnp.mean(yc * yc, axis=-1, keepdims=True)
    return yc * lax.rsqrt(var + LN_EPS) * g + b


def _wo_kernel(m_ref, w_ref, x_ref, y_ref, *, alpha):
    y_ref[...] = alpha * x_ref[...] + _dot(m_ref[...], w_ref[...])


def _wo_residual(merged, w_o, x2, t, d, alpha):
    tm, tn = min(1024, t), min(1024, d)
    tile = pl.BlockSpec((tm, tn), lambda i, j: (i, j))
    return pl.pallas_call(
        functools.partial(_wo_kernel, alpha=alpha),
        grid=(t // tm, d // tn),
        in_specs=[pl.BlockSpec((tm, d), lambda i, j: (i, 0)),
                  pl.BlockSpec((d, tn), lambda i, j: (0, j)),
                  tile],
        out_specs=tile,
        out_shape=jax.ShapeDtypeStruct((t, d), F32),
        compiler_params=_cparams(("parallel", "parallel")),
        name="wo",
    )(merged, w_o, x2)


def _router_kernel(y_ref, g_ref, b_ref, w_ref, bias_ref,
                   x1_ref, x1p_ref, topi_ref, topw_ref, rank_ref, cnt_ref, run_ref):
    i = pl.program_id(0)
    tm, d = y_ref.shape

    @pl.when(i == 0)
    def _():
        run_ref[...] = jnp.zeros_like(run_ref)

    x1 = _layer_norm(y_ref[...], g_ref[...], b_ref[...])
    x1_ref[...] = x1
    x1p_ref[...] = _pack_halves(x1[:, :d // 2], x1[:, d // 2:])

    x_hi, x_mid = _split2(x1)
    w_hi, w_mid = _split2(w_ref[...])
    logits = _dot(x_hi, w_hi) + _dot(x_hi, w_mid) + _dot(x_mid, w_hi)
    s = _sigmoid(logits)
    sc = s + bias_ref[...]
    lane = lax.broadcasted_iota(jnp.int32, (tm, N_EXPERTS), 1)
    grp = lane // GROUP_SIZE
    neg = -jnp.inf

    def first_argmax(v):
        m = jnp.max(v, axis=-1, keepdims=True)
        idx = jnp.min(jnp.where(v == m, lane, N_EXPERTS), axis=-1, keepdims=True)
        return m, idx

    gscore = []
    for gi in range(N_GROUPS):
        vg = jnp.where(grp == gi, sc, neg)
        m1, i1 = first_argmax(vg)
        m2 = jnp.max(jnp.where(lane == i1, neg, vg), axis=-1, keepdims=True)
        gscore.append(m1 + m2)
    emask = jnp.zeros((tm, N_EXPERTS), jnp.bool_)
    for gi in range(N_GROUPS):
        ahead = jnp.zeros((tm, 1), jnp.int32)
        for gj in range(N_GROUPS):
            if gj == gi:
                continue
            beats = (gscore[gj] > gscore[gi]) | ((gscore[gj] == gscore[gi]) & (gj < gi))
            ahead = ahead + beats.astype(jnp.int32)
        emask = emask | ((grp == gi) & (ahead < TOPK_GROUPS))
    masked = jnp.where(emask, sc, neg)

    sel =jnp.zeros((tm, N_EXPERTS), jnp.bool_)
    idxs, vals = [], []
    for _ in range(TOP_K):
        _, ik = first_argmax(masked)
        hit = lane == ik
        sel = sel | hit
        masked = jnp.where(hit, neg, masked)
        idxs.append(ik)
        vals.append(jnp.sum(jnp.where(hit, s, 0.0), axis=-1, keepdims=True))
    wsum = vals[0]
    for v in vals[1:]:
        wsum = wsum + v

    sel_b = jnp.where(sel, 1.0, 0.0).astype(BF16)
    strict = jnp.where(lax.broadcasted_iota(jnp.int32, (tm, tm), 1)
                       < lax.broadcasted_iota(jnp.int32, (tm, tm), 0), 1.0, 0.0).astype(BF16)
    rank_excl = run_ref[0:1, :] + _dot(strict, sel_b)
    run_new = run_ref[0:1, :] + jnp.sum(sel_b.astype(F32), axis=0, keepdims=True)
    run_ref[...] = jnp.broadcast_to(run_new, run_ref.shape)
    cnt_ref[...] = jnp.broadcast_to(run_new, cnt_ref.shape).astype(jnp.int32)

    topi = jnp.zeros((tm, N_EXPERTS), jnp.int32)
    topw = jnp.zeros((tm, N_EXPERTS), F32)
    rnk = jnp.zeros((tm, N_EXPERTS), F32)
    for kk in range(TOP_K):
        hit = lane == idxs[kk]
        rk = jnp.sum(jnp.where(hit, rank_excl, 0.0), axis=-1, keepdims=True)
        topi = jnp.where(lane == kk, idxs[kk], topi)
        topw = jnp.where(lane == kk, vals[kk] / wsum * ROUTED_SCALE, topw)
        rnk = jnp.where(lane == kk, rk, rnk)
    topi_ref[...] = topi
    topw_ref[...] = topw
    rank_ref[...] = rnk.astype(jnp.int32)


def _ln_router(y, g, b, w_router, bias, t, d):
    tm = min(256, t)
    row = pl.BlockSpec((tm, N_EXPERTS), lambda i: (i, 0))
    full = pl.BlockSpec((tm, d), lambda i: (i, 0))
    vec = pl.BlockSpec((1, d), lambda i: (0, 0))
    return pl.pallas_call(
        _router_kernel,
        grid=(t // tm,),
        in_specs=[full, vec, vec,
                  pl.BlockSpec((d, N_EXPERTS), lambda i: (0, 0)),
                  pl.BlockSpec((1, N_EXPERTS), lambda i: (0, 0))],
        out_specs=[full, pl.BlockSpec((tm, d // 2), lambda i: (i, 0)), row, row, row,
                   pl.BlockSpec((SUBLANES, N_EXPERTS), lambda i: (0, 0))],
        out_shape=[jax.ShapeDtypeStruct((t, d), F32),
                   jax.ShapeDtypeStruct((t, d // 2), jnp.uint32),
                   jax.ShapeDtypeStruct((t, N_EXPERTS), jnp.int32),
                   jax.ShapeDtypeStruct((t, N_EXPERTS), F32),
                   jax.ShapeDtypeStruct((t, N_EXPERTS), jnp.int32),
                   jax.ShapeDtypeStruct((SUBLANES, N_EXPERTS), jnp.int32)],
        scratch_shapes=[pltpu.VMEM((SUBLANES, N_EXPERTS), F32)],
        compiler_params=_cparams(("arbitrary",)),
        name="ln_router",
    )(y, g.reshape(1, d).astype(F32), b.reshape(1, d).astype(F32), w_router.astype(F32),
      bias.reshape(1, N_EXPERTS).astype(F32))


def _row_copy(src_ref, src_row, dst_ref, dst_row, sem):
    return pltpu.make_async_copy(src_ref.at[pl.ds(src_row, 1)], dst_ref.at[pl.ds(dst_row, 1)], sem)


def _dispatch_kernel(pos_ref, x_ref, xs_ref, sem):
    i = pl.program_id(0)
    tb = x_ref.shape[0]
    base = i * (tb * TOP_K)

    def start(r, carry):
        for kk in range(TOP_K):
            _row_copy(x_ref, r, xs_ref, pos_ref[base + r * TOP_K + kk], sem).start()
        return carry

    lax.fori_loop(0, tb, start, 0)
    for kk in range(TOP_K):
        pltpu.make_async_copy(x_ref, xs_ref.at[pl.ds(0, tb)], sem).wait()


def _dispatch(pos_flat, x1, n_slots, t, d):
    tb = min(256, t)
    return pl.pallas_call(
        _dispatch_kernel,
        grid_spec=pltpu.PrefetchScalarGridSpec(
            num_scalar_prefetch=1,
            grid=(t // tb,),
            in_specs=[pl.BlockSpec((tb, d), lambda i, pos: (i, 0))],
            out_specs=pl.BlockSpec(memory_space=pl.ANY),
            scratch_shapes=[pltpu.SemaphoreType.DMA],
        ),
        out_shape=jax.ShapeDtypeStruct((n_slots, d), x1.dtype),
        compiler_params=_cparams(("arbitrary",)),
        name="dispatch",
    )(pos_flat, x1)


def _expert_kernel(be_ref, nv_ref, first_ref, nxt_ref, nu_ref, xs_ref, wgu_hbm, wdn_hbm, y_ref,
                   gu_stage, dn_stage, gu_cache, dn_cache, sem):
    b = pl.program_id(0)

    def weight_copies(e):
        return (pltpu.make_async_copy(wgu_hbm.at[e], gu_stage, sem.at[0]),
                pltpu.make_async_copy(wdn_hbm.at[e], dn_stage, sem.at[1]))

    @pl.when(b < nu_ref[0])
    def _():
        rows, hd = xs_ref.shape
        kc = min(EXPERT_CHUNK, hd)
        e = be_ref[b]

        @pl.when(first_ref[b] == 1)
        def _():
            @pl.when(b == 0)
            def _():
                for cp in weight_copies(e):
                    cp.start()

            for cp in weight_copies(e):
                cp.wait()
            for c0 in range(0, 2 * hd, kc):
                gu_cache[c0:c0 + kc, :] = gu_stage[c0:c0 + kc, :].astype(BF16)
                dn_cache[:, c0:c0 + kc] = dn_stage[:, c0:c0 + kc].astype(BF16)

            @pl.when(nxt_ref[b] >= 0)
            def _():
                for cp in weight_copies(nxt_ref[b]):
                    cp.start()

        rid = lax.broadcasted_iota(jnp.int32, (rows, 1), 0)
        valid = rid < nv_ref[b]
        lo, hi = _unpack_halves(xs_ref[...])
        x_lo = jnp.where(valid, lo, 0.0).astype(BF16)
        x_hi = jnp.where(valid, hi, 0.0).astype(BF16)
        hgu = _dot(x_lo, gu_cache[:hd, :]) + _dot(x_hi, gu_cache[hd:, :])
        gate, up = hgu[:, :EXPERT_FF], hgu[:, EXPERT_FF:]
        act = (gate * _sigmoid(gate) * up).astype(BF16)
        y_ref[...] = _pack_halves(_dot(act, dn_cache[:, :hd]), _dot(act, dn_cache[:, hd:]))


def _experts(blk_expert, blk_valid, blk_first, blk_next, n_used, xs, w_gu, w_dn, d):
    n_slots = xs.shape[0]
    nb = n_slots // MOE_ROWS

    def row_map(b, be, nv, fi, nx, nu):
        return (jnp.minimum(b, nu[0] - 1), 0)

    return pl.pallas_call(
        _expert_kernel,
        grid_spec=pltpu.PrefetchScalarGridSpec(
            num_scalar_prefetch=5,
            grid=(nb,),
            in_specs=[pl.BlockSpec((MOE_ROWS, d // 2), row_map),
                      pl.BlockSpec(memory_space=pl.ANY),
                      pl.BlockSpec(memory_space=pl.ANY)],
            out_specs=pl.BlockSpec((MOE_ROWS, d // 2), row_map),
            scratch_shapes=[pltpu.VMEM((d, 2 * EXPERT_FF), w_gu.dtype),
                            pltpu.VMEM((EXPERT_FF, d), w_dn.dtype),
                            pltpu.VMEM((d, 2 * EXPERT_FF), BF16),
                            pltpu.VMEM((EXPERT_FF, d), BF16),
                            pltpu.SemaphoreType.DMA((2,))],
        ),
        out_shape=jax.ShapeDtypeStruct((n_slots, d // 2), jnp.uint32),
        compiler_params=_cparams(("arbitrary",)),
        name="experts",
    )(blk_expert, blk_valid, blk_first, blk_next, n_used, xs, w_gu, w_dn)


def _shared_kernel(x_ref, wgu_ref, wdn_ref, y_ref):
    hd = x_ref.shape[1]
    lo, hi = _unpack_halves(x_ref[...])
    hgu = _dot(lo.astype(BF16), wgu_ref[:hd, :]) + _dot(hi.astype(BF16), wgu_ref[hd:, :])
    gate, up = hgu[:, :EXPERT_FF], hgu[:, EXPERT_FF:]
    act = (gate * _sigmoid(gate) * up).astype(BF16)
    y_ref[...] = _dot(act, wdn_ref[...]).astype(y_ref.dtype)


def _shared(x1p, w_gu, w_dn, t, d):
    tm = min(512, t)
    return pl.pallas_call(
        _shared_kernel,
        grid=(t // tm,),
        in_specs=[pl.BlockSpec((tm, d // 2), lambda i: (i, 0)),
                  pl.BlockSpec((d, 2 * EXPERT_FF), lambda i: (0, 0)),
                  pl.BlockSpec((EXPERT_FF, d), lambda i: (0, 0))],
        out_specs=pl.BlockSpec((tm, d), lambda i: (i, 0)),
        out_shape=jax.ShapeDtypeStruct((t, d), BF16),
        compiler_params=_cparams(("parallel",)),
        name="shared",
    )(x1p, w_gu, w_dn)


def _combine_kernel(pos_ref, x_ref, ysh_ref, topw_ref, g_ref, b_ref, y_hbm, out_ref, buf_ref, sem,
                    *, alpha):
    i = pl.program_id(0)
    n = pl.num_programs(0)
    tb = x_ref.shape[0]

    def gather(tile, slot):
        base = tile * (tb * TOP_K)

        def body(r, carry):
            for kk in range(TOP_K):
                _row_copy(y_hbm, pos_ref[base + r * TOP_K + kk], buf_ref.at[slot, kk], r,
                          sem.at[slot]).start()
            return carry

        lax.fori_loop(0, tb, body, 0)

    @pl.when(i == 0)
    def _():
        gather(0, 0)

    @pl.when(i + 1 < n)
    def _():
        gather(i + 1, (i + 1) % 2)

    slot = i % 2
    for kk in range(TOP_K):
        pltpu.make_async_copy(y_hbm.at[pl.ds(0, tb)], buf_ref.at[slot, kk], sem.at[slot]).wait()
    topw = topw_ref[...]
    hd = x_ref.shape[1] // 2
    acc_lo = alpha * x_ref[:, :hd] + ysh_ref[:, :hd].astype(F32)
    acc_hi = alpha * x_ref[:, hd:] + ysh_ref[:, hd:].astype(F32)
    for kk in range(TOP_K):
        lo, hi = _unpack_halves(buf_ref[slot, kk])
        acc_lo = acc_lo + topw[:, kk:kk + 1] * lo
        acc_hi = acc_hi + topw[:, kk:kk + 1] * hi
    inv_d = 1.0 / (2 * hd)
    mu = (jnp.sum(acc_lo, axis=-1, keepdims=True) + jnp.sum(acc_hi, axis=-1, keepdims=True)) * inv_d
    c_lo, c_hi = acc_lo - mu, acc_hi - mu
    var = (jnp.sum(c_lo * c_lo, axis=-1, keepdims=True)
           + jnp.sum(c_hi * c_hi, axis=-1, keepdims=True)) * inv_d
    rstd = lax.rsqrt(var + LN_EPS)
    out_ref[:, :hd] = c_lo * rstd * g_ref[:, :hd] + b_ref[:, :hd]
    out_ref[:, hd:] = c_hi * rstd * g_ref[:, hd:] + b_ref[:, hd:]


def _combine(pos_flat, x1, ysh, topw, g, b, y, t, d, alpha):
    tb = min(128, t)
    row = pl.BlockSpec((tb, d), lambda i, pos: (i, 0))
    vec = pl.BlockSpec((1, d), lambda i, pos: (0, 0))
    return pl.pallas_call(
        functools.partial(_combine_kernel, alpha=alpha),
        grid_spec=pltpu.PrefetchScalarGridSpec(
            num_scalar_prefetch=1,
            grid=(t // tb,),
            in_specs=[row, row, pl.BlockSpec((tb, N_EXPERTS), lambda i, pos: (i, 0)), vec, vec,
                      pl.BlockSpec(memory_space=pl.ANY)],
            out_specs=row,
            scratch_shapes=[pltpu.VMEM((2, TOP_K, tb, d // 2), jnp.uint32),
                            pltpu.SemaphoreType.DMA((2,))],
        ),
        out_shape=jax.ShapeDtypeStruct((t, d), F32),
        compiler_params=_cparams(("arbitrary",)),
        name="combine",
    )(pos_flat, x1, ysh, topw, g.reshape(1, d).astype(F32), b.reshape(1, d).astype(F32), y)


def _rope_tables(t):
    half = ATT_HEAD_DIM // 2
    inv_freq = ROPE_THETA ** (-jnp.arange(half, dtype=F32) / half)
    ang = jnp.arange(t, dtype=F32)[:, None] * inv_freq[None, :]
    cos, sin = jnp.cos(ang), jnp.sin(ang)
    return jnp.concatenate([cos, cos], -1), jnp.concatenate([-sin, sin], -1)


def _token_mixer(x2, w_in, conv_w, a_log, dt_bias, dn_norm_w, w_dn_branch, w_att_branch, w_o,
                 alpha):
    t, d = x2.shape
    n_main = 4 * DN_WIDTH
    n_ba = 2 * DN_HEADS
    w_main = jnp.concatenate([w_in[:, :n_main], w_in[:, n_main + n_ba:]], axis=1).astype(BF16)
    w_ba = w_in[:, n_main:n_main + n_ba].astype(F32)

    proj = _matmul(x2.astype(BF16), w_main, 1024, 1024, BF16)
    betab, gcb, gct = _dn_gates(x2, w_ba, a_log, dt_bias)
    u, w, qe, kd, intra, egl = _dn_prep(proj, conv_w.astype(F32), betab, gcb, gct, t)
    o_dn = _dn_scan(u, w, qe, kd, intra, egl, proj, dn_norm_w, t)

    cos_t, sin_t = _rope_tables(t)
    o_att = _attention(proj, cos_t, sin_t, t)
    merged = _mix(o_dn, o_att, w_dn_branch.astype(BF16), w_att_branch.astype(BF16), proj, t, d)
    return _wo_residual(merged, w_o.astype(BF16), x2, t, d, alpha)


def _moe(y_mix, ln1_g, ln1_b, w_router, router_bias, w_exp_gate_up, w_exp_down, w_sh_gate_up,
         w_sh_down, ln_g, ln_b, alpha):
    t, d = y_mix.shape
    r = MOE_ROWS
    x1, x1p, topi, topw, rank, cnt = _ln_router(y_mix, ln1_g, ln1_b, w_router, router_bias, t, d)
    counts = cnt[0]
    padded = (counts + r - 1) // r * r
    pad_end = jnp.cumsum(padded)
    pad_start = pad_end - padded
    nb = (t * TOP_K) // r + N_EXPERTS
    blk_row0 = jnp.arange(nb, dtype=jnp.int32) * r
    in_blk = (blk_row0[:, None] >= pad_start[None, :]) & (blk_row0[:, None] < pad_end[None, :])
    expert_ids = jnp.arange(N_EXPERTS, dtype=jnp.int32)
    blk_expert = jnp.where(blk_row0 < pad_end[-1],
                           jnp.sum(jnp.where(in_blk, expert_ids[None, :], 0), axis=1),
                           N_EXPERTS - 1).astype(jnp.int32)
    blk_valid = jnp.sum(jnp.where(in_blk, jnp.clip(counts[None, :] - (blk_row0[:, None] - pad_start[None, :]), 0, r), 0),
                        axis=1).astype(jnp.int32)
    n_used = (pad_end[-1:] // r).astype(jnp.int32)
    blk_ids = jnp.arange(nb, dtype=jnp.int32)
    prev_expert = jnp.concatenate([jnp.full((1,), -1, jnp.int32), blk_expert[:-1]])
    blk_first = (blk_expert != prev_expert).astype(jnp.int32)
    after = jnp.sum(jnp.where(in_blk, pad_end[None, :] // r, 0), axis=1)
    after_expert = jnp.sum(jnp.where(after[:, None] == blk_ids[None, :], blk_expert[None, :], 0), axis=1)
    blk_next = jnp.where(after < n_used[0], after_expert, -1).astype(jnp.int32)
    sel_start =jnp.sum(jnp.where(topi[:, :TOP_K, None] == expert_ids[None, None, :],
                                  pad_start[None, None, :], 0), axis=-1)
    pos_flat = (sel_start + rank[:, :TOP_K]).astype(jnp.int32).reshape(-1)

    xs = _dispatch(pos_flat, x1p, nb * r, t, d // 2)
    y = _experts(blk_expert, blk_valid, blk_first, blk_next, n_used, xs, w_exp_gate_up, w_exp_down, d)
    ysh = _shared(x1p, w_sh_gate_up.astype(BF16), w_sh_down.astype(BF16), t, d)
    return _combine(pos_flat, x1, ysh, topw, ln_g, ln_b, y, t, d, alpha)


def kernel(x, w_in, conv_w, a_log, dt_bias, dn_norm_w, w_dn_branch, w_att_branch, w_o, ln1_g, ln1_b,
           w_router, router_bias, w_exp_gate_up, w_exp_down, w_sh_gate_up, w_sh_down, ln2_g, ln2_b):
    bsz, t, d = x.shape
    depth = w_in.shape[0]
    alpha = (2.0 * depth) ** 0.25
    outs = []
    for bi in range(bsz):
        xb = x[bi]
        for l in range(depth):
            y_mix = _token_mixer(xb, w_in[l], conv_w[l], a_log[l], dt_bias[l], dn_norm_w[l],
                                 w_dn_branch[l], w_att_branch[l], w_o[l], alpha)
            xb = _moe(y_mix, ln1_g[l], ln1_b[l], w_router[l], router_bias[l], w_exp_gate_up[l],
                      w_exp_down[l], w_sh_gate_up[l], w_sh_down[l], ln2_g[l], ln2_b[l], alpha)
        outs.append(xb)
    return jnp.stack(outs, 0)
```

```python
import functools

import jax
import jax.numpy as jnp
from jax import lax
from jax.experimental import pallas as pl
from jax.experimental.pallas import tpu as pltpu

F32 = jnp.float32
BF16 = jnp.bfloat16

DN_HEADS = 16
DN_HEAD_DIM = 128
DN_WIDTH = DN_HEADS * DN_HEAD_DIM
DN_CONV = 4
DN_CHUNK = 64
ATT_GROUPS = ((128, 1), (512, 4), (2048, 16))
ATT_HEADS_PER_GROUP = 8
ATT_HEAD_DIM = 128
ATT_GROUP_WIDTH = ATT_HEADS_PER_GROUP * ATT_HEAD_DIM
ATT_WIDTH = len(ATT_GROUPS) * ATT_GROUP_WIDTH
ATT_BLOCK = 128
ROPE_THETA = 10000.0
N_EXPERTS = 128
TOP_K = 8
N_GROUPS = 8
GROUP_SIZE = N_EXPERTS // N_GROUPS
TOPK_GROUPS = 4
EXPERT_FF = 384
ROUTED_SCALE = 2.5
LN_EPS = 1e-5
RMS_EPS = 1e-6
L2_EPS = 1e-6

LANES = 128
SUBLANES = 8
BF16_SUBLANES = 16
VMEM_LIMIT = 56 * 1024 * 1024

MOE_ROWS = 256
EXPERT_CHUNK = 1024
EXPERT_OUT_CHUNK = 512
NEG_BIG = -1e30


def _cparams(sem, vmem=VMEM_LIMIT):
    return pltpu.CompilerParams(dimension_semantics=sem, vmem_limit_bytes=vmem)


def _split3(a):
    hi = a.astype(BF16)
    r1 = a - hi.astype(F32)
    mid = r1.astype(BF16)
    lo = (r1 - mid.astype(F32)).astype(BF16)
    return hi, mid, lo


def _split2(a):
    hi = a.astype(BF16)
    return hi, (a - hi.astype(F32)).astype(BF16)


def _dot(a, b):
    return jnp.dot(a, b, preferred_element_type=F32)


def _dot_nt(a, b):
    return lax.dot_general(a, b, (((1,), (1,)), ((), ())), preferred_element_type=F32)


def _dot_tn(a, b):
    return lax.dot_general(a, b, (((0,), (0,)), ((), ())), preferred_element_type=F32)


def _sigmoid(x):
    return 1.0 / (1.0 + jnp.exp(-x))


def _pack_halves(lo, hi):
    lo_b = lax.bitcast_convert_type(lo.astype(BF16).astype(F32), jnp.uint32)
    hi_b = lax.bitcast_convert_type(hi.astype(BF16).astype(F32), jnp.uint32)
    return (hi_b & jnp.uint32(0xFFFF0000)) | (lo_b >> 16)


def _unpack_halves(u):
    lo = lax.bitcast_convert_type(u << 16, F32)
    hi = lax.bitcast_convert_type(u & jnp.uint32(0xFFFF0000), F32)
    return lo, hi


def _mm_kernel(a_ref, b_ref, o_ref):
    o_ref[...] = _dot(a_ref[...], b_ref[...]).astype(o_ref.dtype)


def _matmul(a, b, tm, tn, out_dtype, name):
    m, k = a.shape
    n = b.shape[1]
    tm, tn = min(tm, m), min(tn, n)
    return pl.pallas_call(
        _mm_kernel,
        grid=(m // tm, n // tn),
        in_specs=[pl.BlockSpec((tm, k), lambda i, j: (i, 0)),
                  pl.BlockSpec((k, tn), lambda i, j: (0, j))],
        out_specs=pl.BlockSpec((tm, tn), lambda i, j: (i, j)),
        out_shape=jax.ShapeDtypeStruct((m, n), out_dtype),
        compiler_params=_cparams(("parallel", "parallel")),
        name=name,
    )(a, b)


def _gates_kernel(x_ref, w_ref, prm_ref, betab_ref, gcb_ref, gct_ref):
    tm = x_ref.shape[0]
    x_hi, x_mid = _split2(x_ref[...])
    w_hi, w_mid = _split2(w_ref[...])
    logits = _dot(x_hi, w_hi) + _dot(x_hi, w_mid) + _dot(x_mid, w_hi)

    def softplus(v):
        return jnp.maximum(v, 0.0) + jnp.log(1.0 + jnp.exp(-jnp.abs(v)))

    prm = prm_ref[...]
    beta = _sigmoid(logits)
    g = prm[0:1, :] * softplus(logits + prm[1:2, :])

    ri = lax.broadcasted_iota(jnp.int32, (tm, tm), 0)
    ci = lax.broadcasted_iota(jnp.int32, (tm, tm), 1)
    same = (ri // DN_CHUNK) == (ci // DN_CHUNK)
    lower = jnp.where(same & (ci <= ri), 1.0, 0.0).astype(BF16)
    gc = jnp.zeros((tm, LANES), F32)
    for part in _split3(g):
        gc = gc + _dot(lower, part)
    gct_ref[...] = gc.T[DN_HEADS:2 * DN_HEADS, :]
    for h in range(DN_HEADS):
        sl = slice(h * LANES, (h + 1) * LANES)
        betab_ref[:, sl] = jnp.broadcast_to(beta[:, h:h + 1], (tm, LANES))
        gcb_ref[:, sl] = jnp.broadcast_to(gc[:, DN_HEADS + h:DN_HEADS + h + 1], (tm, LANES))


def _dn_gates(x2, w_in, col0, a_log, dt_bias):
    t, d = x2.shape
    tm = min(512, t)
    neg_a = -jnp.exp(a_log.astype(F32))
    prm = jnp.zeros((SUBLANES, LANES), F32)
    prm = prm.at[0, DN_HEADS:2 * DN_HEADS].set(neg_a).at[1, DN_HEADS:2 * DN_HEADS].set(dt_bias.astype(F32))
    return pl.pallas_call(
        _gates_kernel,
        grid=(t // tm,),
        in_specs=[pl.BlockSpec((tm, d), lambda i: (i, 0)),
                  pl.BlockSpec((d, LANES), lambda i: (0, col0 // LANES)),
                  pl.BlockSpec((SUBLANES, LANES), lambda i: (0, 0))],
        out_specs=[pl.BlockSpec((tm, DN_WIDTH), lambda i: (i, 0)),
                   pl.BlockSpec((tm, DN_WIDTH), lambda i: (i, 0)),
                   pl.BlockSpec((DN_HEADS, tm), lambda i: (0, i))],
        out_shape=[jax.ShapeDtypeStruct((t, DN_WIDTH), F32),
                   jax.ShapeDtypeStruct((t, DN_WIDTH), F32),
                   jax.ShapeDtypeStruct((DN_HEADS, t), F32)],
        compiler_params=_cparams(("parallel",)),
        name="dn_gates",
    )(x2, w_in.astype(F32), prm)


def _dn_prep_kernel(q_ref, k_ref, v_ref, hq_ref, hk_ref, hv_ref, cq_ref, ck_ref, cv_ref,
                    betab_ref, gcb_ref, gct_ref,
                    u_ref, w_ref, qe_ref, kd_ref, intra_ref, egl_ref, cbuf):
    i = pl.program_id(1)
    rows = q_ref.shape[0]
    c = DN_CHUNK
    halo_rows = hq_ref.shape[0]

    def conv_silu(slot, x_ref, halo_ref, cw_ref):
        cbuf[slot, 0:halo_rows, :] = jnp.where(i > 0, halo_ref[...].astype(F32), 0.0)
        cbuf[slot, halo_rows:halo_rows + rows, :] = x_ref[...].astype(F32)
        cw = cw_ref[...]
        y = jnp.zeros((rows, LANES), F32)
        for j in range(DN_CONV):
            off = halo_rows - (DN_CONV - 1) + j
            y = y + cw[j:j + 1, :] * cbuf[slot, off:off + rows, :]
        return y * _sigmoid(y)

    def l2n(a):
        return a * lax.rsqrt(jnp.sum(a * a, axis=-1, keepdims=True) + L2_EPS)

    q = l2n(conv_silu(0, q_ref, hq_ref, cq_ref)) * (DN_HEAD_DIM ** -0.5)
    k = l2n(conv_silu(1, k_ref, hk_ref, ck_ref))
    v = conv_silu(2, v_ref, hv_ref, cv_ref)
    beta = betab_ref[...]
    gcol = gcb_ref[...]
    grow_all = gct_ref[...]

    ii = lax.broadcasted_iota(jnp.int32, (c, c), 0)
    jj = lax.broadcasted_iota(jnp.int32, (c, c), 1)
    eye = jnp.where(ii == jj, 1.0, 0.0)
    n_chunks = rows // c

    decays, a_mats = [], []
    kbs, egcs = [], []
    for n in range(n_chunks):
        rs = slice(n * c, (n + 1) * c)
        kc = k[rs]
        kb = kc * beta[rs]
        grow = grow_all[:, rs]
        diff = jnp.where(ii >= jj, gcol[rs, :c] - grow, 0.0)
        decay = jnp.where(ii >= jj, jnp.exp(diff), 0.0)
        kk = _dot_nt(kb.astype(BF16), kc.astype(BF16))
        a_mats.append(jnp.where(ii > jj, kk * decay, 0.0))
        decays.append(decay)
        kbs.append(kb)
        egcs.append(jnp.exp(gcol[rs]))

    invs = []
    for n in range(n_chunks):
        invs.append(eye - jnp.where((ii // 2 == jj // 2) & (ii > jj), a_mats[n], 0.0))
    s = 2
    while s < c:
        sel = ((ii // (2 * s)) == (jj // (2 * s))) & ((ii // s) > (jj // s))
        for n in range(n_chunks):
            d_b = invs[n].astype(BF16)
            x_b = jnp.where(sel, a_mats[n], 0.0).astype(BF16)
            t1 = _dot(d_b, x_b).astype(BF16)
            invs[n] = invs[n] - _dot(t1, d_b)
        s *= 2

    for n in range(n_chunks):
        rs = slice(n * c, (n + 1) * c)
        t_b = invs[n].astype(BF16)
        u_ref[rs, :] = _dot(t_b, (v[rs] * beta[rs]).astype(BF16)).astype(u_ref.dtype)
        w_ref[rs, :] = _dot(t_b, (kbs[n] * egcs[n]).astype(BF16)).astype(w_ref.dtype)
        qc = q[rs]
        kc = k[rs]
        qk = _dot_nt(qc.astype(BF16), kc.astype(BF16))
        intra = qk * decays[n]
        intra_ref[rs, :] = jnp.concatenate([intra, jnp.zeros_like(intra)], axis=1).astype(intra_ref.dtype)
        qe_ref[rs, :] = (qc * egcs[n]).astype(qe_ref.dtype)
        glast = gcol[(n + 1) * c - 1:(n + 1) * c, :]
        kd_ref[rs, :] = (kc * jnp.exp(glast - gcol[rs])).astype(kd_ref.dtype)
        egl_ref[n * SUBLANES:(n + 1) * SUBLANES, :] = jnp.broadcast_to(jnp.exp(glast),
                                                                       (SUBLANES, LANES))


def _dn_prep(proj, conv_w, betab, gcb, gct, t):
    rows = min(512, t)
    hb = BF16_SUBLANES
    nq = DN_WIDTH // LANES

    def blk(off):
        return pl.BlockSpec((rows, LANES), lambda h, i, off=off: (i, off + h))

    def halo(off):
        return pl.BlockSpec((hb, LANES),
                            lambda h, i, off=off: (jnp.maximum(i * (rows // hb) - 1, 0), off + h))

    def cw(off):
        return pl.BlockSpec((DN_CONV, LANES), lambda h, i, off=off: (0, off + h))

    per_head = pl.BlockSpec((rows, LANES), lambda h, i: (i, h))
    outs = [jax.ShapeDtypeStruct((t, DN_WIDTH), BF16)] * 5
    outs.append(jax.ShapeDtypeStruct((t // DN_CHUNK * SUBLANES, DN_WIDTH), F32))
    return pl.pallas_call(
        _dn_prep_kernel,
        grid=(DN_HEADS, t // rows),
        in_specs=[blk(0), blk(nq), blk(2 * nq), halo(0), halo(nq), halo(2 * nq),
                  cw(0), cw(nq), cw(2 * nq), per_head, per_head,
                  pl.BlockSpec((None, 1, rows), lambda h, i: (h, 0, i))],
        out_specs=[per_head] * 5 + [pl.BlockSpec((rows // DN_CHUNK * SUBLANES, LANES), lambda h, i: (i, h))],
        out_shape=outs,
        scratch_shapes=[pltpu.VMEM((3, hb + rows, LANES), F32)],
        compiler_params=_cparams(("parallel", "parallel")),
        name="dn_prep",
    )(proj, proj, proj, proj, proj, proj, conv_w, conv_w, conv_w, betab, gcb,
      gct.reshape(DN_HEADS, 1, t))


def _dn_scan_kernel(u_ref, w_ref, qe_ref, kd_ref, intra_ref, egl_ref, z_ref, nw_ref, o_ref, s_ref):
    i = pl.program_id(0)
    c = DN_CHUNK
    n_chunks = u_ref.shape[0] // c

    @pl.when(i == 0)
    def _():
        s_ref[...] = jnp.zeros_like(s_ref)

    nw = nw_ref[...]

    pw = 2 * LANES
    first = lax.broadcasted_iota(jnp.int32, (c, pw), 1) < LANES
    zeros_s = jnp.zeros((DN_HEAD_DIM, LANES), BF16)
    zeros_v = jnp.zeros((c, pw), BF16)

    def chunk(n, carry):
        r0 = pl.multiple_of(n * c, c)
        e0 = pl.multiple_of(n * SUBLANES, SUBLANES)
        for p in range(DN_HEADS // 2):
            ps = slice(p * pw, (p + 1) * pw)
            s_p = s_ref[p]
            s_b = s_p.astype(BF16)
            s_diag = jnp.concatenate(
                [jnp.concatenate([s_b[:, :LANES], zeros_s], axis=1),
                 jnp.concatenate([zeros_s, s_b[:, LANES:]], axis=1)], axis=0)
            wq = jnp.concatenate([w_ref[pl.ds(r0, c), ps], qe_ref[pl.ds(r0, c), ps]], axis=0)
            ws_qs = _dot(wq, s_diag)
            v_new = u_ref[pl.ds(r0, c), ps].astype(F32) - ws_qs[:c]
            v_b = v_new.astype(BF16)
            v_diag = jnp.concatenate([jnp.where(first, v_b, zeros_v), zeros_v,
                                      jnp.where(first, zeros_v, v_b), zeros_v], axis=0)
            o = ws_qs[c:] + _dot(intra_ref[pl.ds(r0, c), ps], v_diag)
            eg = egl_ref[pl.ds(e0, SUBLANES), ps]
            s_dec = (s_p.reshape(DN_HEAD_DIM // SUBLANES, SUBLANES, pw) * eg[None]
                     ).reshape(DN_HEAD_DIM, pw)
            kv = _dot_tn(kd_ref[pl.ds(r0, c), ps], v_b)
            s_ref[p] = s_dec + jnp.concatenate([kv[:LANES, :LANES], kv[LANES:, LANES:]], axis=1)
            z = z_ref[pl.ds(r0, c), ps].astype(F32)
            gate = z * _sigmoid(z)
            for hh in range(2):
                ls = slice(hh * LANES, (hh + 1) * LANES)
                o_h = o[:, ls]
                o_h = o_h * lax.rsqrt(jnp.mean(o_h * o_h, axis=-1, keepdims=True) + RMS_EPS)
                o_ref[pl.ds(r0, c), p * pw + hh * LANES:p * pw + (hh + 1) * LANES] = (
                    o_h * nw * gate[:, ls]).astype(o_ref.dtype)
        return carry

    lax.fori_loop(0, n_chunks, chunk, 0)


def _dn_scan(u, w, qe, kd, intra, egl, proj, norm_w, t):
    rows = min(512, t)
    full = pl.BlockSpec((rows, DN_WIDTH), lambda i: (i, 0))
    return pl.pallas_call(
        _dn_scan_kernel,
        grid=(t // rows,),
        in_specs=[full, full, full, full, full,
                  pl.BlockSpec((rows // DN_CHUNK * SUBLANES, DN_WIDTH), lambda i: (i, 0)),
                  pl.BlockSpec((rows, DN_WIDTH), lambda i: (i, 3)),
                  pl.BlockSpec((1, LANES), lambda i: (0, 0))],
        out_specs=full,
        out_shape=jax.ShapeDtypeStruct((t, DN_WIDTH), BF16),
        scratch_shapes=[pltpu.VMEM((DN_HEADS // 2, DN_HEAD_DIM, 2 * DN_HEAD_DIM), F32)],
        compiler_params=_cparams(("arbitrary",)),
        name="dn_scan",
    )(u, w, qe, kd, intra, egl, proj, norm_w.reshape(1, LANES).astype(F32))


def _attn_kernel(*refs, dilation, span, n_earlier):
    q_ref, k_ref, v_ref, cos_ref, sin_ref = refs[:5]
    earlier = refs[5:5 + 2 * n_earlier]
    out_refs = refs[5 + 2 * n_earlier:-5]
    kbuf, vbuf, qbuf, obuf, lbuf = refs[-5:]
    i = pl.program_id(1)
    bt, width = q_ref.shape
    d = dilation
    blk = ATT_BLOCK
    prev = d * blk
    half = ATT_HEAD_DIM // 2
    scale = ATT_HEAD_DIM ** -0.5

    heads = width // LANES

    @pl.when(i == 0)
    def _():
        kbuf[:, 0:prev, :] = jnp.zeros((heads, prev, LANES), F32)
        vbuf[:, 0:prev, :] = jnp.zeros((heads, prev, LANES), F32)

    cos, sin = cos_ref[...], sin_ref[...]

    def rope(a):
        a = a.astype(F32)
        return a * cos + pltpu.roll(a, half, 1) * sin

    for h in range(heads):
        ls = slice(h * LANES, (h + 1) * LANES)
        qbuf[h] = rope(q_ref[:, ls]) * scale
        kbuf[h, prev:prev + bt, :] = rope(k_ref[:, ls])
        vbuf[h, prev:prev + bt, :] = v_ref[:, ls].astype(F32)

    qi = lax.broadcasted_iota(jnp.int32, (blk, 2 * blk), 0) + blk
    ki = lax.broadcasted_iota(jnp.int32, (blk, 2 * blk), 1)
    dist = qi - ki
    band = (dist >= 0) & (dist <= span)
    band_first = band & ((ki >= blk) | (i > 0))

    def rows(start, size):
        return pl.ds(start, size, stride=d) if d > 1 else pl.ds(start, size)

    for j in range(bt // prev):
        for r in range(d):
            r0 = j * prev + r
            for h in range(heads):
                qj = qbuf[h, rows(r0, blk), :].astype(BF16)
                kj = kbuf[h, rows(r0, 2 * blk), :].astype(BF16)
                vj = vbuf[h, rows(r0, 2 * blk), :].astype(BF16)
                s = _dot_nt(qj, kj)
                s = jnp.where(band_first if j == 0 else band, s, NEG_BIG)
                m = jnp.max(s, axis=-1, keepdims=True)
                p = jnp.exp(s - m)
                l = jnp.sum(p, axis=-1, keepdims=True)
                obuf[h, rows(r0, blk), :] = _dot(p.astype(BF16), vj) / l
                lbuf[h, rows(r0, blk), :] = jnp.broadcast_to(m + jnp.log(l), (blk, LANES))

    for h in range(heads):
        ls = slice(h * LANES, (h + 1) * LANES)
        if n_earlier == 0:
            o_ref, lse_ref = out_refs
            o_ref[:, ls] = obuf[h].astype(o_ref.dtype)
            lse_ref[:, ls] = lbuf[h]
        else:
            (o_ref,) = out_refs
            outs = [earlier[2 * g][:, ls].astype(F32) for g in range(n_earlier)] + [obuf[h]]
            lses = [earlier[2 * g + 1][:, ls] for g in range(n_earlier)] + [lbuf[h]]
            m = functools.reduce(jnp.maximum, lses)
            es = [jnp.exp(l - m) for l in lses]
            num = functools.reduce(lambda a, b: a + b, [e * o for e, o in zip(es, outs)])
            den = functools.reduce(lambda a, b: a + b, es)
            o_ref[:, ls] = (num / den).astype(o_ref.dtype)
    kbuf[:, 0:prev, :] = kbuf[:, bt:bt + prev, :]
    vbuf[:, 0:prev, :] = vbuf[:, bt:bt + prev, :]


def _attn_group(proj, cos_t, sin_t, gi, dilation, span, t, earlier=()):
    prev = dilation * ATT_BLOCK
    bt = max(min(512, t), prev)
    width = 2 * LANES
    gw = ATT_GROUP_WIDTH
    per_g = gw // width
    q0 = gi * gw // width
    k0 = q0 + ATT_WIDTH // width
    v0 = k0 + ATT_WIDTH // width

    def cur(c0):
        return pl.BlockSpec((bt, width), lambda hg, i, c0=c0: (i, c0 + hg))

    tab = pl.BlockSpec((bt, LANES), lambda hg, i: (i, 0))
    out = pl.BlockSpec((bt, width), lambda hg, i: (i, hg))
    flat_earlier = [a for pair in earlier for a in pair]
    if earlier:
        out_specs, out_shape = [out], [jax.ShapeDtypeStruct((t, gw), BF16)]
    else:
        out_specs = [out, out]
        out_shape = [jax.ShapeDtypeStruct((t, gw), BF16), jax.ShapeDtypeStruct((t, gw), F32)]
    return pl.pallas_call(
        functools.partial(_attn_kernel, dilation=dilation, span=span, n_earlier=len(earlier)),
        grid=(per_g, t // bt),
        in_specs=[cur(q0), cur(k0), cur(v0), tab, tab] + [out] * len(flat_earlier),
        out_specs=out_specs,
        out_shape=out_shape,
        scratch_shapes=[pltpu.VMEM((width // LANES, prev + bt, LANES), F32),
                        pltpu.VMEM((width // LANES, prev + bt, LANES), F32),
                        pltpu.VMEM((width // LANES, bt, LANES), F32),
                        pltpu.VMEM((width // LANES, bt, LANES), F32),
                        pltpu.VMEM((width // LANES, bt, LANES), F32)],
        compiler_params=_cparams(("parallel", "arbitrary")),
        name=f"attn_g{gi}",
    )(proj, proj, proj, cos_t, sin_t, *flat_earlier)


def _attention(proj, cos_t, sin_t, t):
    earlier = []
    for gi, (window, dilation) in enumerate(ATT_GROUPS[:-1]):
        earlier.append(_attn_group(proj, cos_t, sin_t, gi, dilation, window // dilation, t))
    window, dilation = ATT_GROUPS[-1]
    (o_att,) = _attn_group(proj, cos_t, sin_t, len(ATT_GROUPS) - 1, dilation, window // dilation, t,
                           earlier=tuple(earlier))
    return o_att


def _mix_kernel(odn_ref, oatt_ref, wdn_ref, watt_ref, gdn_ref, gatt_ref, out_ref):
    y_dn = _dot(odn_ref[...], wdn_ref[...])
    y_att = _dot(oatt_ref[...], watt_ref[...])
    merged = (_sigmoid(gdn_ref[...].astype(F32)) * y_dn
              + _sigmoid(gatt_ref[...].astype(F32)) * y_att)
    out_ref[...] = merged.astype(out_ref.dtype)


def _mix(o_dn, o_att, w_dn, w_att, proj, t, d):
    tm = min(1024, t)
    tn = ATT_GROUP_WIDTH
    g0 = 3 * ATT_WIDTH // tn
    row_dn = pl.BlockSpec((tm, DN_WIDTH), lambda j, i: (i, 0))
    row_g = pl.BlockSpec((tm, ATT_GROUP_WIDTH), lambda j, i: (i, 0))
    return pl.pallas_call(
        _mix_kernel,
        grid=(d // tn, t // tm),
        in_specs=[row_dn, row_g,
                  pl.BlockSpec((DN_WIDTH, tn), lambda j, i: (0, j)),
                  pl.BlockSpec((ATT_GROUP_WIDTH, tn), lambda j, i: (0, j)),
                  pl.BlockSpec((tm, tn), lambda j, i: (i, g0 + j)),
                  pl.BlockSpec((tm, tn), lambda j, i: (i, g0 + d // tn + j))],
        out_specs=pl.BlockSpec((tm, tn), lambda j, i: (i, j)),
        out_shape=jax.ShapeDtypeStruct((t, d), BF16),
        compiler_params=_cparams(("parallel", "parallel")),
        name="mix",
    )(o_dn, o_att, w_dn, w_att, proj, proj)


def _layer_norm(y, g, b):
    mu = jnp.mean(y, axis=-1, keepdims=True)
    yc = y - mu
    var = jnp.mean(yc * yc, axis=-1, keepdims=True)
    return yc * lax.rsqrt(var + LN_EPS) * g + b


def _wo_kernel(m_ref, w_ref, x_ref, y_ref, *, alpha):
    y_ref[...] = alpha * x_ref[...] + _dot(m_ref[...], w_ref[...])


def _wo_residual(merged, w_o, x2, t, d, alpha):
    tm, tn = min(1024, t), min(1024, d)
    tile = pl.BlockSpec((tm, tn), lambda i, j: (i, j))
    return pl.pallas_call(
        functools.partial(_wo_kernel, alpha=alpha),
        grid=(t // tm, d // tn),
        in_specs=[pl.BlockSpec((tm, d), lambda i, j: (i, 0)),
                  pl.BlockSpec((d, tn), lambda i, j: (0, j)),
                  tile],
        out_specs=tile,
        out_shape=jax.ShapeDtypeStruct((t, d), F32),
        compiler_params=_cparams(("parallel", "parallel")),
        name="wo",
    )(merged, w_o, x2)


def _router_kernel(y_ref, g_ref, b_ref, w_ref, bias_ref,
                   x1_ref, x1p_ref, topi_ref, topw_ref, rank_ref, cnt_ref, run_ref):
    i = pl.program_id(0)
    tm, d = y_ref.shape

    @pl.when(i == 0)
    def _():
        run_ref[...] = jnp.zeros_like(run_ref)

    x1 = _layer_norm(y_ref[...], g_ref[...], b_ref[...])
    x1_ref[...] = x1
    x1p_ref[...] = _pack_halves(x1[:, :d // 2], x1[:, d // 2:])

    x_hi, x_mid = _split2(x1)
    w_hi, w_mid = _split2(w_ref[...])
    logits = _dot(x_hi, w_hi) + _dot(x_hi, w_mid) + _dot(x_mid, w_hi)
    s = _sigmoid(logits)
    sc = s + bias_ref[...]
    lane = lax.broadcasted_iota(jnp.int32, (tm, N_EXPERTS), 1)
    grp = lane // GROUP_SIZE
    neg = -jnp.inf

    def first_argmax(v):
        m = jnp.max(v, axis=-1, keepdims=True)
        idx = jnp.min(jnp.where(v == m, lane, N_EXPERTS), axis=-1, keepdims=True)
        return m, idx

    gscore = []
    for gi in range(N_GROUPS):
        vg = jnp.where(grp == gi, sc, neg)
        m1, i1 = first_argmax(vg)
        m2 = jnp.max(jnp.where(lane == i1, neg, vg), axis=-1, keepdims=True)
        gscore.append(m1 + m2)
    emask = jnp.zeros((tm, N_EXPERTS), jnp.bool_)
    for gi in range(N_GROUPS):
        ahead = jnp.zeros((tm, 1), jnp.int32)
        for gj in range(N_GROUPS):
            if gj == gi:
                continue
            beats = (gscore[gj] > gscore[gi]) | ((gscore[gj] == gscore[gi]) & (gj < gi))
            ahead = ahead + beats.astype(jnp.int32)
        emask = emask | ((grp == gi) & (ahead < TOPK_GROUPS))
    masked = jnp.where(emask, sc, neg)

    sel =jnp.zeros((tm, N_EXPERTS), jnp.bool_)
    idxs, vals = [], []
    for _ in range(TOP_K):
        _, ik = first_argmax(masked)
        hit = lane == ik
        sel = sel | hit
        masked = jnp.where(hit, neg, masked)
        idxs.append(ik)
        vals.append(jnp.sum(jnp.where(hit, s, 0.0), axis=-1, keepdims=True))
    wsum = vals[0]
    for v in vals[1:]:
        wsum = wsum + v

    sel_b = jnp.where(sel, 1.0, 0.0).astype(BF16)
    strict = jnp.where(lax.broadcasted_iota(jnp.int32, (tm, tm), 1)
                       < lax.broadcasted_iota(jnp.int32, (tm, tm), 0), 1.0, 0.0).astype(BF16)
    rank_excl = run_ref[0:1, :] + _dot(strict, sel_b)
    run_new = run_ref[0:1, :] + jnp.sum(sel_b.astype(F32), axis=0, keepdims=True)
    run_ref[...] = jnp.broadcast_to(run_new, run_ref.shape)
    cnt_ref[...] = jnp.broadcast_to(run_new, cnt_ref.shape).astype(jnp.int32)

    topi = jnp.zeros((tm, N_EXPERTS), jnp.int32)
    topw = jnp.zeros((tm, N_EXPERTS), F32)
    rnk = jnp.zeros((tm, N_EXPERTS), F32)
    for kk in range(TOP_K):
        hit = lane == idxs[kk]
        rk = jnp.sum(jnp.where(hit, rank_excl, 0.0), axis=-1, keepdims=True)
        topi = jnp.where(lane == kk, idxs[kk], topi)
        topw = jnp.where(lane == kk, vals[kk] / wsum * ROUTED_SCALE, topw)
        rnk = jnp.where(lane == kk, rk, rnk)
    topi_ref[...] = topi
    topw_ref[...] = topw
    rank_ref[...] = rnk.astype(jnp.int32)


def _ln_router(y, g, b, w_router, bias, t, d):
    tm = min(256, t)
    row = pl.BlockSpec((tm, N_EXPERTS), lambda i: (i, 0))
    full = pl.BlockSpec((tm, d), lambda i: (i, 0))
    vec = pl.BlockSpec((1, d), lambda i: (0, 0))
    return pl.pallas_call(
        _router_kernel,
        grid=(t // tm,),
        in_specs=[full, vec, vec,
                  pl.BlockSpec((d, N_EXPERTS), lambda i: (0, 0)),
                  pl.BlockSpec((1, N_EXPERTS), lambda i: (0, 0))],
        out_specs=[full, pl.BlockSpec((tm, d // 2), lambda i: (i, 0)), row, row, row,
                   pl.BlockSpec((SUBLANES, N_EXPERTS), lambda i: (0, 0))],
        out_shape=[jax.ShapeDtypeStruct((t, d), F32),
                   jax.ShapeDtypeStruct((t, d // 2), jnp.uint32),
                   jax.ShapeDtypeStruct((t, N_EXPERTS), jnp.int32),
                   jax.ShapeDtypeStruct((t, N_EXPERTS), F32),
                   jax.ShapeDtypeStruct((t, N_EXPERTS), jnp.int32),
                   jax.ShapeDtypeStruct((SUBLANES, N_EXPERTS), jnp.int32)],
        scratch_shapes=[pltpu.VMEM((SUBLANES, N_EXPERTS), F32)],
        compiler_params=_cparams(("arbitrary",)),
        name="ln_router",
    )(y, g.reshape(1, d).astype(F32), b.reshape(1, d).astype(F32), w_router.astype(F32),
      bias.reshape(1, N_EXPERTS).astype(F32))


def _slot_kernel(topi_ref, rank_ref, start_ref, pos_ref):
    topi = topi_ref[...]
    lane = lax.broadcasted_iota(jnp.int32, topi.shape, 1)
    start = jnp.broadcast_to(start_ref[...], topi.shape)
    pos = rank_ref[...]
    for kk in range(TOP_K):
        seg = jnp.sum(jnp.where(lane == topi[:, kk:kk + 1], start, 0.0), axis=-1, keepdims=True)
        pos = jnp.where(lane == kk, pos + seg.astype(jnp.int32), pos)
    pos_ref[...] = pos


def _slots(topi, rank, pad_start, t):
    tm = min(1024, t)
    row = pl.BlockSpec((tm, N_EXPERTS), lambda i: (i, 0))
    return pl.pallas_call(
        _slot_kernel,
        grid=(t // tm,),
        in_specs=[row, row, pl.BlockSpec((1, N_EXPERTS), lambda i: (0, 0))],
        out_specs=row,
        out_shape=jax.ShapeDtypeStruct((t, N_EXPERTS), jnp.int32),
        compiler_params=_cparams(("parallel",)),
        name="slots",
    )(topi, rank, pad_start.astype(F32).reshape(1, N_EXPERTS))


def _row_copy(src_ref, src_row, dst_ref, dst_row, sem):
    return pltpu.make_async_copy(src_ref.at[pl.ds(src_row, 1)], dst_ref.at[pl.ds(dst_row, 1)], sem)


def _dispatch_kernel(pos_ref, x_ref, xs_ref, sem):
    i = pl.program_id(0)
    tb = x_ref.shape[0]
    base = i * (tb * TOP_K)

    def start(r, carry):
        for kk in range(TOP_K):
            _row_copy(x_ref, r, xs_ref, pos_ref[base + r * TOP_K + kk], sem).start()
        return carry

    lax.fori_loop(0, tb, start, 0)
    for kk in range(TOP_K):
        pltpu.make_async_copy(x_ref, xs_ref.at[pl.ds(0, tb)], sem).wait()


def _dispatch(pos_flat, x1, n_slots, t, d):
    tb = min(256, t)
    return pl.pallas_call(
        _dispatch_kernel,
        grid_spec=pltpu.PrefetchScalarGridSpec(
            num_scalar_prefetch=1,
            grid=(t // tb,),
            in_specs=[pl.BlockSpec((tb, d), lambda i, pos: (i, 0))],
            out_specs=pl.BlockSpec(memory_space=pl.ANY),
            scratch_shapes=[pltpu.SemaphoreType.DMA],
        ),
        out_shape=jax.ShapeDtypeStruct((n_slots, d), x1.dtype),
        compiler_params=_cparams(("arbitrary",)),
        name="dispatch",
    )(pos_flat, x1)


def _expert_kernel(be_ref, nv_ref, first_ref, nxt_ref, nu_ref, xs_ref, wgu_hbm, wdn_hbm, y_ref,
                   gu_stage, dn_stage, gu_cache, dn_cache, sem):
    b = pl.program_id(0)

    def weight_copies(e):
        return (pltpu.make_async_copy(wgu_hbm.at[e], gu_stage, sem.at[0]),
                pltpu.make_async_copy(wdn_hbm.at[e], dn_stage, sem.at[1]))

    @pl.when(b < nu_ref[0])
    def _():
        rows, hd = xs_ref.shape
        kc = min(EXPERT_CHUNK, hd)
        e = be_ref[b]

        @pl.when(first_ref[b] == 1)
        def _():
            @pl.when(b == 0)
            def _():
                for cp in weight_copies(e):
                    cp.start()

            for cp in weight_copies(e):
                cp.wait()
            for c0 in range(0, 2 * hd, kc):
                for j in range(EXPERT_FF // LANES):
                    gu_cache[c0:c0 + kc, 2 * j * LANES:(2 * j + 1) * LANES] = (
                        gu_stage[c0:c0 + kc, j * LANES:(j + 1) * LANES].astype(BF16))
                    gu_cache[c0:c0 + kc, (2 * j + 1) * LANES:(2 * j + 2) * LANES] = (
                        gu_stage[c0:c0 + kc, EXPERT_FF + j * LANES:EXPERT_FF + (j + 1) * LANES]
                        .astype(BF16))
                dn_cache[:, c0:c0 + kc] = dn_stage[:, c0:c0 + kc].astype(BF16)

            @pl.when(nxt_ref[b] >= 0)
            def _():
                for cp in weight_copies(nxt_ref[b]):
                    cp.start()

        rid = lax.broadcasted_iota(jnp.int32, (rows, 1), 0)
        valid = rid < nv_ref[b]
        lo, hi = _unpack_halves(xs_ref[...])
        x_lo = jnp.where(valid, lo, 0.0).astype(BF16)
        x_hi = jnp.where(valid, hi, 0.0).astype(BF16)
        acts = []
        for j in range(EXPERT_FF // LANES):
            cs = slice(2 * j * LANES, (2 * j + 2) * LANES)
            hgu = _dot(x_lo, gu_cache[:hd, cs]) + _dot(x_hi, gu_cache[hd:, cs])
            gate, up = hgu[:, :LANES], hgu[:, LANES:]
            acts.append((gate * _sigmoid(gate) * up).astype(BF16))
        act = jnp.concatenate(acts, axis=1)
        nc = min(EXPERT_OUT_CHUNK, hd)
        for c0 in range(0, hd, nc):
            y_ref[:, c0:c0 + nc] = _pack_halves(_dot(act, dn_cache[:, c0:c0 + nc]),
                                                _dot(act, dn_cache[:, hd + c0:hd + c0 + nc]))


def _experts(blk_expert, blk_valid, blk_first, blk_next, n_used, xs, w_gu, w_dn, d):
    n_slots = xs.shape[0]
    nb = n_slots // MOE_ROWS

    def row_map(b, be, nv, fi, nx, nu):
        return (jnp.minimum(b, nu[0] - 1), 0)

    return pl.pallas_call(
        _expert_kernel,
        grid_spec=pltpu.PrefetchScalarGridSpec(
            num_scalar_prefetch=5,
            grid=(nb,),
            in_specs=[pl.BlockSpec((MOE_ROWS, d // 2), row_map),
                      pl.BlockSpec(memory_space=pl.ANY),
                      pl.BlockSpec(memory_space=pl.ANY)],
            out_specs=pl.BlockSpec((MOE_ROWS, d // 2), row_map),
            scratch_shapes=[pltpu.VMEM((d, 2 * EXPERT_FF), w_gu.dtype),
                            pltpu.VMEM((EXPERT_FF, d), w_dn.dtype),
                            pltpu.VMEM((d, 2 * EXPERT_FF), BF16),
                            pltpu.VMEM((EXPERT_FF, d), BF16),
                            pltpu.SemaphoreType.DMA((2,))],
        ),
        out_shape=jax.ShapeDtypeStruct((n_slots, d // 2), jnp.uint32),
        compiler_params=_cparams(("arbitrary",)),
        name="experts",
    )(blk_expert, blk_valid, blk_first, blk_next, n_used, xs, w_gu, w_dn)


def _shared_kernel(x_ref, wgu_ref, wdn_ref, y_ref):
    hd = x_ref.shape[1]
    lo, hi = _unpack_halves(x_ref[...])
    hgu = _dot(lo.astype(BF16), wgu_ref[:hd, :]) + _dot(hi.astype(BF16), wgu_ref[hd:, :])
    gate, up = hgu[:, :EXPERT_FF], hgu[:, EXPERT_FF:]
    act = (gate * _sigmoid(gate) * up).astype(BF16)
    y_ref[...] = _dot(act, wdn_ref[...]).astype(y_ref.dtype)


def _shared(x1p, w_gu, w_dn, t, d):
    tm = min(512, t)
    return pl.pallas_call(
        _shared_kernel,
        grid=(t // tm,),
        in_specs=[pl.BlockSpec((tm, d // 2), lambda i: (i, 0)),
                  pl.BlockSpec((d, 2 * EXPERT_FF), lambda i: (0, 0)),
                  pl.BlockSpec((EXPERT_FF, d), lambda i: (0, 0))],
        out_specs=pl.BlockSpec((tm, d), lambda i: (i, 0)),
        out_shape=jax.ShapeDtypeStruct((t, d), BF16),
        compiler_params=_cparams(("parallel",)),
        name="shared",
    )(x1p, w_gu, w_dn)


def _combine_kernel(pos_ref, x_ref, ysh_ref, topw_ref, g_ref, b_ref, y_hbm, out_ref, buf_ref, sem,
                    *, alpha):
    i = pl.program_id(0)
    n = pl.num_programs(0)
    tb = x_ref.shape[0]

    def gather(tile, slot):
        base = tile * (tb * TOP_K)

        def body(r, carry):
            for kk in range(TOP_K):
                _row_copy(y_hbm, pos_ref[base + r * TOP_K + kk], buf_ref.at[slot, kk], r,
                          sem.at[slot]).start()
            return carry

        lax.fori_loop(0, tb, body, 0)

    @pl.when(i == 0)
    def _():
        gather(0, 0)

    @pl.when(i + 1 < n)
    def _():
        gather(i + 1, (i + 1) % 2)

    slot = i % 2
    for kk in range(TOP_K):
        pltpu.make_async_copy(y_hbm.at[pl.ds(0, tb)], buf_ref.at[slot, kk], sem.at[slot]).wait()
    topw = topw_ref[...]
    hd = x_ref.shape[1] // 2
    acc_lo = alpha * x_ref[:, :hd] + ysh_ref[:, :hd].astype(F32)
    acc_hi = alpha * x_ref[:, hd:] + ysh_ref[:, hd:].astype(F32)
    for kk in range(TOP_K):
        lo, hi = _unpack_halves(buf_ref[slot, kk])
        acc_lo = acc_lo + topw[:, kk:kk + 1] * lo
        acc_hi = acc_hi + topw[:, kk:kk + 1] * hi
    inv_d = 1.0 / (2 * hd)
    mu = (jnp.sum(acc_lo, axis=-1, keepdims=True) + jnp.sum(acc_hi, axis=-1, keepdims=True)) * inv_d
    c_lo, c_hi = acc_lo - mu, acc_hi - mu
    var = (jnp.sum(c_lo * c_lo, axis=-1, keepdims=True)
           + jnp.sum(c_hi * c_hi, axis=-1, keepdims=True)) * inv_d
    rstd = lax.rsqrt(var + LN_EPS)
    out_ref[:, :hd] = c_lo * rstd * g_ref[:, :hd] + b_ref[:, :hd]
    out_ref[:, hd:] = c_hi * rstd * g_ref[:, hd:] + b_ref[:, hd:]


def _combine(pos_flat, x1, ysh, topw, g, b, y, t, d, alpha):
    tb = min(128, t)
    row = pl.BlockSpec((tb, d), lambda i, pos: (i, 0))
    vec = pl.BlockSpec((1, d), lambda i, pos: (0, 0))
    return pl.pallas_call(
        functools.partial(_combine_kernel, alpha=alpha),
        grid_spec=pltpu.PrefetchScalarGridSpec(
            num_scalar_prefetch=1,
            grid=(t // tb,),
            in_specs=[row, row, pl.BlockSpec((tb, N_EXPERTS), lambda i, pos: (i, 0)), vec, vec,
                      pl.BlockSpec(memory_space=pl.ANY)],
            out_specs=row,
            scratch_shapes=[pltpu.VMEM((2, TOP_K, tb, d // 2), jnp.uint32),
                            pltpu.SemaphoreType.DMA((2,))],
        ),
        out_shape=jax.ShapeDtypeStruct((t, d), F32),
        compiler_params=_cparams(("arbitrary",)),
        name="combine",
    )(pos_flat, x1, ysh, topw, g.reshape(1, d).astype(F32), b.reshape(1, d).astype(F32), y)


def _rope_tables(t):
    half = ATT_HEAD_DIM // 2
    inv_freq = ROPE_THETA ** (-jnp.arange(half, dtype=F32) / half)
    ang = jnp.arange(t, dtype=F32)[:, None] * inv_freq[None, :]
    cos, sin = jnp.cos(ang), jnp.sin(ang)
    return jnp.concatenate([cos, cos], -1), jnp.concatenate([-sin, sin], -1)


def _token_mixer(x2, w_in, conv_w, a_log, dt_bias, dn_norm_w, w_dn_branch, w_att_branch, w_o,
                 alpha):
    t, d = x2.shape
    n_main = 4 * DN_WIDTH
    n_ba = 2 * DN_HEADS
    x_b = x2.astype(BF16)
    proj_dn = _matmul(x_b, w_in[:, :n_main].astype(BF16), 1024, 1024, BF16, "proj_dn")
    proj_att = _matmul(x_b, w_in[:, n_main + n_ba:].astype(BF16), 1024, 1024, BF16, "proj_att")
    betab, gcb, gct = _dn_gates(x2, w_in, n_main, a_log, dt_bias)
    u, w, qe, kd, intra, egl = _dn_prep(proj_dn, conv_w.astype(F32), betab, gcb, gct, t)
    o_dn = _dn_scan(u, w, qe, kd, intra, egl, proj_dn, dn_norm_w, t)

    cos_t, sin_t = _rope_tables(t)
    o_att = _attention(proj_att, cos_t, sin_t, t)
    merged = _mix(o_dn, o_att, w_dn_branch.astype(BF16), w_att_branch.astype(BF16), proj_att, t, d)
    return _wo_residual(merged, w_o.astype(BF16), x2, t, d, alpha)


def _moe(y_mix, ln1_g, ln1_b, w_router, router_bias, w_exp_gate_up, w_exp_down, w_sh_gate_up,
         w_sh_down, ln_g, ln_b, alpha):
    t, d = y_mix.shape
    r = MOE_ROWS
    x1, x1p, topi, topw, rank, cnt = _ln_router(y_mix, ln1_g, ln1_b, w_router, router_bias, t, d)
    counts = cnt[0]
    padded = (counts + r - 1) // r * r
    pad_end = jnp.cumsum(padded)
    pad_start = pad_end - padded
    nb = (t * TOP_K) // r + N_EXPERTS
    blk_row0 = jnp.arange(nb, dtype=jnp.int32) * r
    in_blk = (blk_row0[:, None] >= pad_start[None, :]) & (blk_row0[:, None] < pad_end[None, :])
    expert_ids = jnp.arange(N_EXPERTS, dtype=jnp.int32)
    blk_expert = jnp.where(blk_row0 < pad_end[-1],
                           jnp.sum(jnp.where(in_blk, expert_ids[None, :], 0), axis=1),
                           N_EXPERTS - 1).astype(jnp.int32)
    blk_valid = jnp.sum(jnp.where(in_blk, jnp.clip(counts[None, :] - (blk_row0[:, None] - pad_start[None, :]), 0, r), 0),
                        axis=1).astype(jnp.int32)
    n_used = (pad_end[-1:] // r).astype(jnp.int32)
    blk_ids = jnp.arange(nb, dtype=jnp.int32)
    prev_expert = jnp.concatenate([jnp.full((1,), -1, jnp.int32), blk_expert[:-1]])
    blk_first = (blk_expert != prev_expert).astype(jnp.int32)
    after = jnp.sum(jnp.where(in_blk, pad_end[None, :] // r, 0), axis=1)
    after_expert = jnp.sum(jnp.where(after[:, None] == blk_ids[None, :], blk_expert[None, :], 0), axis=1)
    blk_next = jnp.where(after < n_used[0], after_expert, -1).astype(jnp.int32)
    pos_flat = _slots(topi, rank, pad_start, t)[:, :TOP_K].reshape(-1)

    xs = _dispatch(pos_flat, x1p, nb * r, t, d // 2)
    y = _experts(blk_expert, blk_valid, blk_first, blk_next, n_used, xs, w_exp_gate_up, w_exp_down, d)
    ysh = _shared(x1p, w_sh_gate_up.astype(BF16), w_sh_down.astype(BF16), t, d)
    return _combine(pos_flat, x1, ysh, topw, ln_g, ln_b, y, t, d, alpha)


def kernel(x, w_in, conv_w, a_log, dt_bias, dn_norm_w, w_dn_branch, w_att_branch, w_o, ln1_g, ln1_b,
           w_router, router_bias, w_exp_gate_up, w_exp_down, w_sh_gate_up, w_sh_down, ln2_g, ln2_b):
    bsz, t, d = x.shape
    depth = w_in.shape[0]
    alpha = (2.0 * depth) ** 0.25
    outs = []
    for bi in range(bsz):
        xb = x[bi]
        for l in range(depth):
            y_mix = _token_mixer(xb, w_in[l], conv_w[l], a_log[l], dt_bias[l], dn_norm_w[l],
                                 w_dn_branch[l], w_att_branch[l], w_o[l], alpha)
            xb = _moe(y_mix, ln1_g[l], ln1_b[l], w_router[l], router_bias[l], w_exp_gate_up[l],
                      w_exp_down[l], w_sh_gate_up[l], w_sh_down[l], ln2_g[l], ln2_b[l], alpha)
        outs.append(xb)
    return jnp.stack(outs, 0)
```

```python
import functools

import jax
import jax.numpy as jnp
from jax import lax
from jax.experimental import pallas as pl
from jax.experimental.pallas import tpu as pltpu

F32 = jnp.float32
BF16 = jnp.bfloat16

DN_HEADS = 16
DN_HEAD_DIM = 128
DN_WIDTH = DN_HEADS * DN_HEAD_DIM
DN_CONV = 4
DN_CHUNK = 64
ATT_GROUPS = ((128, 1), (512, 4), (2048, 16))
ATT_HEADS_PER_GROUP = 8
ATT_HEAD_DIM = 128
ATT_GROUP_WIDTH = ATT_HEADS_PER_GROUP * ATT_HEAD_DIM
ATT_WIDTH = len(ATT_GROUPS) * ATT_GROUP_WIDTH
ATT_BLOCK = 128
ROPE_THETA = 10000.0
N_EXPERTS = 128
TOP_K = 8
N_GROUPS = 8
GROUP_SIZE = N_EXPERTS // N_GROUPS
TOPK_GROUPS = 4
EXPERT_FF = 384
ROUTED_SCALE = 2.5
LN_EPS = 1e-5
RMS_EPS = 1e-6
L2_EPS = 1e-6

LANES = 128
SUBLANES = 8
BF16_SUBLANES = 16
VMEM_LIMIT = 56 * 1024 * 1024

DN_PREP_ROWS = 1024
MOE_ROWS = 256
EXPERT_CHUNK = 1024
EXPERT_OUT_CHUNK = 512
NEG_BIG = -1e30


def _cparams(sem, vmem=VMEM_LIMIT):
    return pltpu.CompilerParams(dimension_semantics=sem, vmem_limit_bytes=vmem)


def _split3(a):
    hi = a.astype(BF16)
    r1 = a - hi.astype(F32)
    mid = r1.astype(BF16)
    lo = (r1 - mid.astype(F32)).astype(BF16)
    return hi, mid, lo


def _split2(a):
    hi = a.astype(BF16)
    return hi, (a - hi.astype(F32)).astype(BF16)


def _dot(a, b):
    return jnp.dot(a, b, preferred_element_type=F32)


def _dot_nt(a, b):
    return lax.dot_general(a, b, (((1,), (1,)), ((), ())), preferred_element_type=F32)


def _dot_tn(a, b):
    return lax.dot_general(a, b, (((0,), (0,)), ((), ())), preferred_element_type=F32)


def _sigmoid(x):
    return 1.0 / (1.0 + jnp.exp(-x))


def _pack_halves(lo, hi):
    lo_b = lax.bitcast_convert_type(lo.astype(BF16).astype(F32), jnp.uint32)
    hi_b = lax.bitcast_convert_type(hi.astype(BF16).astype(F32), jnp.uint32)
    return (hi_b & jnp.uint32(0xFFFF0000)) | (lo_b >> 16)


def _unpack_halves(u):
    lo = lax.bitcast_convert_type(u << 16, F32)
    hi = lax.bitcast_convert_type(u & jnp.uint32(0xFFFF0000), F32)
    return lo, hi


def _proj_kernel(x_ref, wt_ref, o_ref):
    o_ref[...] = _dot_nt(x_ref[...], wt_ref[...].astype(BF16)).astype(o_ref.dtype)


def _proj(x_b, w_t, row0, n, name):
    m, k = x_b.shape
    tm, tn = min(1024, m), min(512, n)
    return pl.pallas_call(
        _proj_kernel,
        grid=(m // tm, n // tn),
        in_specs=[pl.BlockSpec((tm, k), lambda i, j: (i, 0)),
                  pl.BlockSpec((pl.Element(tn), pl.Element(k)),
                               lambda i, j: (pl.multiple_of(row0 + j * tn, SUBLANES), 0))],
        out_specs=pl.BlockSpec((tm, tn), lambda i, j: (i, j)),
        out_shape=jax.ShapeDtypeStruct((m, n), BF16),
        compiler_params=_cparams(("parallel", "parallel")),
        name=name,
    )(x_b, w_t)


def _gates_kernel(x_ref, w_ref, prm_ref, betab_ref, gcb_ref, gct_ref):
    tm = x_ref.shape[0]
    x_hi, x_mid = _split2(x_ref[...])
    w_hi, w_mid = _split2(w_ref[...])
    logits = _dot_nt(x_hi, w_hi) + _dot_nt(x_hi, w_mid) + _dot_nt(x_mid, w_hi)

    def softplus(v):
        return jnp.maximum(v, 0.0) + jnp.log(1.0 + jnp.exp(-jnp.abs(v)))

    prm = prm_ref[...]
    beta = _sigmoid(logits)
    g = prm[0:1, :] * softplus(logits + prm[1:2, :])

    ri = lax.broadcasted_iota(jnp.int32, (tm, tm), 0)
    ci = lax.broadcasted_iota(jnp.int32, (tm, tm), 1)
    same = (ri // DN_CHUNK) == (ci // DN_CHUNK)
    lower = jnp.where(same & (ci <= ri), 1.0, 0.0).astype(BF16)
    gc = jnp.zeros((tm, LANES), F32)
    for part in _split3(g):
        gc = gc + _dot(lower, part)
    gct_ref[...] = gc.T[DN_HEADS:2 * DN_HEADS, :]
    for h in range(DN_HEADS):
        sl = slice(h * LANES, (h + 1) * LANES)
        betab_ref[:, sl] = jnp.broadcast_to(beta[:, h:h + 1], (tm, LANES))
        gcb_ref[:, sl] = jnp.broadcast_to(gc[:, DN_HEADS + h:DN_HEADS + h + 1], (tm, LANES))


def _dn_gates(x2, w_t, col0, a_log, dt_bias):
    t, d = x2.shape
    tm = min(512, t)
    neg_a = -jnp.exp(a_log.astype(F32))
    prm = jnp.zeros((SUBLANES, LANES), F32)
    prm = prm.at[0, DN_HEADS:2 * DN_HEADS].set(neg_a).at[1, DN_HEADS:2 * DN_HEADS].set(dt_bias.astype(F32))
    return pl.pallas_call(
        _gates_kernel,
        grid=(t // tm,),
        in_specs=[pl.BlockSpec((tm, d), lambda i: (i, 0)),
                  pl.BlockSpec((LANES, d), lambda i: (col0 // LANES, 0)),
                  pl.BlockSpec((SUBLANES, LANES), lambda i: (0, 0))],
        out_specs=[pl.BlockSpec((tm, DN_WIDTH), lambda i: (i, 0)),
                   pl.BlockSpec((tm, DN_WIDTH), lambda i: (i, 0)),
                   pl.BlockSpec((DN_HEADS, tm), lambda i: (0, i))],
        out_shape=[jax.ShapeDtypeStruct((t, DN_WIDTH), F32),
                   jax.ShapeDtypeStruct((t, DN_WIDTH), F32),
                   jax.ShapeDtypeStruct((DN_HEADS, t), F32)],
        compiler_params=_cparams(("parallel",)),
        name="dn_gates",
    )(x2, w_t.astype(F32), prm)


def _dn_prep_kernel(q_ref, k_ref, v_ref, hq_ref, hk_ref, hv_ref, cq_ref, ck_ref, cv_ref,
                    betab_ref, gcb_ref, gct_ref,
                    u_ref, w_ref, qe_ref, kd_ref, intra_ref, egl_ref, cbuf):
    i = pl.program_id(1)
    rows = q_ref.shape[0]
    c = DN_CHUNK
    halo_rows = hq_ref.shape[0]

    def conv_silu(slot, x_ref, halo_ref, cw_ref):
        cbuf[slot, 0:halo_rows, :] = jnp.where(i > 0, halo_ref[...].astype(F32), 0.0)
        cbuf[slot, halo_rows:halo_rows + rows, :] = x_ref[...].astype(F32)
        cw = cw_ref[...]
        y = jnp.zeros((rows, LANES), F32)
        for j in range(DN_CONV):
            off = halo_rows - (DN_CONV - 1) + j
            y = y + cw[j:j + 1, :] * cbuf[slot, off:off + rows, :]
        return y * _sigmoid(y)

    def l2n(a):
        return a * lax.rsqrt(jnp.sum(a * a, axis=-1, keepdims=True) + L2_EPS)

    q = l2n(conv_silu(0, q_ref, hq_ref, cq_ref)) * (DN_HEAD_DIM ** -0.5)
    k = l2n(conv_silu(1, k_ref, hk_ref, ck_ref))
    v = conv_silu(2, v_ref, hv_ref, cv_ref)
    beta = betab_ref[...]
    gcol = gcb_ref[...]
    grow_all = gct_ref[...]

    ii = lax.broadcasted_iota(jnp.int32, (c, c), 0)
    jj = lax.broadcasted_iota(jnp.int32, (c, c), 1)
    eye = jnp.where(ii == jj, 1.0, 0.0)
    n_chunks = rows // c

    decays, a_mats = [], []
    kbs, egcs = [], []
    for n in range(n_chunks):
        rs = slice(n * c, (n + 1) * c)
        kc = k[rs]
        kb = kc * beta[rs]
        grow = grow_all[:, rs]
        diff = jnp.where(ii >= jj, gcol[rs, :c] - grow, 0.0)
        decay = jnp.where(ii >= jj, jnp.exp(diff), 0.0)
        kk = _dot_nt(kb.astype(BF16), kc.astype(BF16))
        a_mats.append(jnp.where(ii > jj, kk * decay, 0.0))
        decays.append(decay)
        kbs.append(kb)
        egcs.append(jnp.exp(gcol[rs]))

    invs = []
    for n in range(n_chunks):
        invs.append(eye - jnp.where((ii // 2 == jj // 2) & (ii > jj), a_mats[n], 0.0))
    s = 2
    while s < c:
        sel = ((ii // (2 * s)) == (jj // (2 * s))) & ((ii // s) > (jj // s))
        for n in range(n_chunks):
            d_b = invs[n].astype(BF16)
            x_b = jnp.where(sel, a_mats[n], 0.0).astype(BF16)
            t1 = _dot(d_b, x_b).astype(BF16)
            invs[n] = invs[n] - _dot(t1, d_b)
        s *= 2

    for n in range(n_chunks):
        rs = slice(n * c, (n + 1) * c)
        t_b = invs[n].astype(BF16)
        u_ref[rs, :] = _dot(t_b, (v[rs] * beta[rs]).astype(BF16)).astype(u_ref.dtype)
        w_ref[rs, :] = _dot(t_b, (kbs[n] * egcs[n]).astype(BF16)).astype(w_ref.dtype)
        qc = q[rs]
        kc = k[rs]
        qk = _dot_nt(qc.astype(BF16), kc.astype(BF16))
        intra = qk * decays[n]
        intra_ref[rs, :] = jnp.concatenate([intra, jnp.zeros_like(intra)], axis=1).astype(intra_ref.dtype)
        qe_ref[rs, :] = (qc * egcs[n]).astype(qe_ref.dtype)
        glast = gcol[(n + 1) * c - 1:(n + 1) * c, :]
        kd_ref[rs, :] = (kc * jnp.exp(glast - gcol[rs])).astype(kd_ref.dtype)
        egl_ref[n * SUBLANES:(n + 1) * SUBLANES, :] = jnp.broadcast_to(jnp.exp(glast),
                                                                       (SUBLANES, LANES))


def _dn_prep(proj, conv_w, betab, gcb, gct, t):
    rows = min(DN_PREP_ROWS, t)
    hb = BF16_SUBLANES
    nq = DN_WIDTH // LANES

    def blk(off):
        return pl.BlockSpec((rows, LANES), lambda h, i, off=off: (i, off + h))

    def halo(off):
        return pl.BlockSpec((hb, LANES),
                            lambda h, i, off=off: (jnp.maximum(i * (rows // hb) - 1, 0), off + h))

    def cw(off):
        return pl.BlockSpec((DN_CONV, LANES), lambda h, i, off=off: (0, off + h))

    per_head = pl.BlockSpec((rows, LANES), lambda h, i: (i, h))
    outs = [jax.ShapeDtypeStruct((t, DN_WIDTH), BF16)] * 5
    outs.append(jax.ShapeDtypeStruct((t // DN_CHUNK * SUBLANES, DN_WIDTH), F32))
    return pl.pallas_call(
        _dn_prep_kernel,
        grid=(DN_HEADS, t // rows),
        in_specs=[blk(0), blk(nq), blk(2 * nq), halo(0), halo(nq), halo(2 * nq),
                  cw(0), cw(nq), cw(2 * nq), per_head, per_head,
                  pl.BlockSpec((None, 1, rows), lambda h, i: (h, 0, i))],
        out_specs=[per_head] * 5 + [pl.BlockSpec((rows // DN_CHUNK * SUBLANES, LANES), lambda h, i: (i, h))],
        out_shape=outs,
        scratch_shapes=[pltpu.VMEM((3, hb + rows, LANES), F32)],
        compiler_params=_cparams(("parallel", "parallel")),
        name="dn_prep",
    )(proj, proj, proj, proj, proj, proj, conv_w, conv_w, conv_w, betab, gcb,
      gct.reshape(DN_HEADS, 1, t))


def _dn_scan_kernel(u_ref, w_ref, qe_ref, kd_ref, intra_ref, egl_ref, z_ref, nw_ref, o_ref, s_ref):
    i = pl.program_id(0)
    c = DN_CHUNK
    n_chunks = u_ref.shape[0] // c

    @pl.when(i == 0)
    def _():
        s_ref[...] = jnp.zeros_like(s_ref)

    nw = nw_ref[...]

    pw = 2 * LANES
    first = lax.broadcasted_iota(jnp.int32, (c, pw), 1) < LANES
    zeros_s = jnp.zeros((DN_HEAD_DIM, LANES), BF16)
    zeros_v = jnp.zeros((c, pw), BF16)

    def chunk(n, carry):
        r0 = pl.multiple_of(n * c, c)
        e0 = pl.multiple_of(n * SUBLANES, SUBLANES)
        for p in range(DN_HEADS // 2):
            ps = slice(p * pw, (p + 1) * pw)
            s_p = s_ref[p]
            s_b = s_p.astype(BF16)
            s_diag = jnp.concatenate(
                [jnp.concatenate([s_b[:, :LANES], zeros_s], axis=1),
                 jnp.concatenate([zeros_s, s_b[:, LANES:]], axis=1)], axis=0)
            wq = jnp.concatenate([w_ref[pl.ds(r0, c), ps], qe_ref[pl.ds(r0, c), ps]], axis=0)
            ws_qs = _dot(wq, s_diag)
            v_new = u_ref[pl.ds(r0, c), ps].astype(F32) - ws_qs[:c]
            v_b = v_new.astype(BF16)
            v_diag = jnp.concatenate([jnp.where(first, v_b, zeros_v), zeros_v,
                                      jnp.where(first, zeros_v, v_b), zeros_v], axis=0)
            o = ws_qs[c:] + _dot(intra_ref[pl.ds(r0, c), ps], v_diag)
            eg = egl_ref[pl.ds(e0, SUBLANES), ps]
            s_dec = (s_p.reshape(DN_HEAD_DIM // SUBLANES, SUBLANES, pw) * eg[None]
                     ).reshape(DN_HEAD_DIM, pw)
            kv = _dot_tn(kd_ref[pl.ds(r0, c), ps], v_b)
            s_ref[p] = s_dec + jnp.concatenate([kv[:LANES, :LANES], kv[LANES:, LANES:]], axis=1)
            z = z_ref[pl.ds(r0, c), ps].astype(F32)
            gate = z * _sigmoid(z)
            for hh in range(2):
                ls = slice(hh * LANES, (hh + 1) * LANES)
                o_h = o[:, ls]
                o_h = o_h * lax.rsqrt(jnp.mean(o_h * o_h, axis=-1, keepdims=True) + RMS_EPS)
                o_ref[pl.ds(r0, c), p * pw + hh * LANES:p * pw + (hh + 1) * LANES] = (
                    o_h * nw * gate[:, ls]).astype(o_ref.dtype)
        return carry

    lax.fori_loop(0, n_chunks, chunk, 0)


def _dn_scan(u, w, qe, kd, intra, egl, proj, norm_w, t):
    rows = min(512, t)
    full = pl.BlockSpec((rows, DN_WIDTH), lambda i: (i, 0))
    return pl.pallas_call(
        _dn_scan_kernel,
        grid=(t // rows,),
        in_specs=[full, full, full, full, full,
                  pl.BlockSpec((rows // DN_CHUNK * SUBLANES, DN_WIDTH), lambda i: (i, 0)),
                  pl.BlockSpec((rows, DN_WIDTH), lambda i: (i, 3)),
                  pl.BlockSpec((1, LANES), lambda i: (0, 0))],
        out_specs=full,
        out_shape=jax.ShapeDtypeStruct((t, DN_WIDTH), BF16),
        scratch_shapes=[pltpu.VMEM((DN_HEADS // 2, DN_HEAD_DIM, 2 * DN_HEAD_DIM), F32)],
        compiler_params=_cparams(("arbitrary",)),
        name="dn_scan",
    )(u, w, qe, kd, intra, egl, proj, norm_w.reshape(1, LANES).astype(F32))


def _attn_kernel(*refs, dilation, span, n_earlier):
    q_ref, k_ref, v_ref, cos_ref, sin_ref = refs[:5]
    earlier = refs[5:5 + 2 * n_earlier]
    out_refs = refs[5 + 2 * n_earlier:-5]
    kbuf, vbuf, qbuf, obuf, lbuf = refs[-5:]
    i = pl.program_id(1)
    bt, width = q_ref.shape
    d = dilation
    blk = ATT_BLOCK
    prev = d * blk
    half = ATT_HEAD_DIM // 2
    scale = ATT_HEAD_DIM ** -0.5

    heads = width // LANES

    @pl.when(i == 0)
    def _():
        kbuf[:, 0:prev, :] = jnp.zeros((heads, prev, LANES), F32)
        vbuf[:, 0:prev, :] = jnp.zeros((heads, prev, LANES), F32)

    cos, sin = cos_ref[...], sin_ref[...]

    def rope(a):
        a = a.astype(F32)
        return a * cos + pltpu.roll(a, half, 1) * sin

    for h in range(heads):
        ls = slice(h * LANES, (h + 1) * LANES)
        qbuf[h] = rope(q_ref[:, ls]) * scale
        kbuf[h, prev:prev + bt, :] = rope(k_ref[:, ls])
        vbuf[h, prev:prev + bt, :] = v_ref[:, ls].astype(F32)

    qi = lax.broadcasted_iota(jnp.int32, (blk, 2 * blk), 0) + blk
    ki = lax.broadcasted_iota(jnp.int32, (blk, 2 * blk), 1)
    dist = qi - ki
    band = (dist >= 0) & (dist <= span)
    band_first = band & ((ki >= blk) | (i > 0))

    def rows(start, size):
        return pl.ds(start, size, stride=d) if d > 1 else pl.ds(start, size)

    for j in range(bt // prev):
        for r in range(d):
            r0 = j * prev + r
            for h in range(heads):
                qj = qbuf[h, rows(r0, blk), :].astype(BF16)
                kj = kbuf[h, rows(r0, 2 * blk), :].astype(BF16)
                vj = vbuf[h, rows(r0, 2 * blk), :].astype(BF16)
                s = _dot_nt(qj, kj)
                s = jnp.where(band_first if j == 0 else band, s, NEG_BIG)
                m = jnp.max(s, axis=-1, keepdims=True)
                p = jnp.exp(s - m)
                l = jnp.sum(p, axis=-1, keepdims=True)
                obuf[h, rows(r0, blk), :] = _dot(p.astype(BF16), vj) / l
                lbuf[h, rows(r0, blk), :] = jnp.broadcast_to(m + jnp.log(l), (blk, LANES))

    for h in range(heads):
        ls = slice(h * LANES, (h + 1) * LANES)
        if n_earlier == 0:
            o_ref, lse_ref = out_refs
            o_ref[:, ls] = obuf[h].astype(o_ref.dtype)
            lse_ref[:, ls] = lbuf[h]
        else:
            (o_ref,) = out_refs
            outs = [earlier[2 * g][:, ls].astype(F32) for g in range(n_earlier)] + [obuf[h]]
            lses = [earlier[2 * g + 1][:, ls] for g in range(n_earlier)] + [lbuf[h]]
            m = functools.reduce(jnp.maximum, lses)
            es = [jnp.exp(l - m) for l in lses]
            num = functools.reduce(lambda a, b: a + b, [e * o for e, o in zip(es, outs)])
            den = functools.reduce(lambda a, b: a + b, es)
            o_ref[:, ls] = (num / den).astype(o_ref.dtype)
    kbuf[:, 0:prev, :] = kbuf[:, bt:bt + prev, :]
    vbuf[:, 0:prev, :] = vbuf[:, bt:bt + prev, :]


def _attn_group(proj, cos_t, sin_t, gi, dilation, span, t, earlier=()):
    prev = dilation * ATT_BLOCK
    bt = max(min(512, t), prev)
    width = 2 * LANES
    gw = ATT_GROUP_WIDTH
    per_g = gw // width
    q0 = gi * gw // width
    k0 = q0 + ATT_WIDTH // width
    v0 = k0 + ATT_WIDTH // width

    def cur(c0):
        return pl.BlockSpec((bt, width), lambda hg, i, c0=c0: (i, c0 + hg))

    tab = pl.BlockSpec((bt, LANES), lambda hg, i: (i, 0))
    out = pl.BlockSpec((bt, width), lambda hg, i: (i, hg))
    flat_earlier = [a for pair in earlier for a in pair]
    if earlier:
        out_specs, out_shape = [out], [jax.ShapeDtypeStruct((t, gw), BF16)]
    else:
        out_specs = [out, out]
        out_shape = [jax.ShapeDtypeStruct((t, gw), BF16), jax.ShapeDtypeStruct((t, gw), F32)]
    return pl.pallas_call(
        functools.partial(_attn_kernel, dilation=dilation, span=span, n_earlier=len(earlier)),
        grid=(per_g, t // bt),
        in_specs=[cur(q0), cur(k0), cur(v0), tab, tab] + [out] * len(flat_earlier),
        out_specs=out_specs,
        out_shape=out_shape,
        scratch_shapes=[pltpu.VMEM((width // LANES, prev + bt, LANES), F32),
                        pltpu.VMEM((width // LANES, prev + bt, LANES), F32),
                        pltpu.VMEM((width // LANES, bt, LANES), F32),
                        pltpu.VMEM((width // LANES, bt, LANES), F32),
                        pltpu.VMEM((width // LANES, bt, LANES), F32)],
        compiler_params=_cparams(("parallel", "arbitrary")),
        name=f"attn_g{gi}",
    )(proj, proj, proj, cos_t, sin_t, *flat_earlier)


def _attention(proj, cos_t, sin_t, t):
    earlier = []
    for gi, (window, dilation) in enumerate(ATT_GROUPS[:-1]):
        earlier.append(_attn_group(proj, cos_t, sin_t, gi, dilation, window // dilation, t))
    window, dilation = ATT_GROUPS[-1]
    (o_att,) = _attn_group(proj, cos_t, sin_t, len(ATT_GROUPS) - 1, dilation, window // dilation, t,
                           earlier=tuple(earlier))
    return o_att


def _mix_kernel(odn_ref, oatt_ref, wdn_ref, watt_ref, gdn_ref, gatt_ref, out_ref):
    y_dn = _dot(odn_ref[...], wdn_ref[...])
    y_att = _dot(oatt_ref[...], watt_ref[...])
    merged = (_sigmoid(gdn_ref[...].astype(F32)) * y_dn
              + _sigmoid(gatt_ref[...].astype(F32)) * y_att)
    out_ref[...] = merged.astype(out_ref.dtype)


def _mix(o_dn, o_att, w_dn, w_att, proj, t, d):
    tm = min(1024, t)
    tn = ATT_GROUP_WIDTH
    g0 = 3 * ATT_WIDTH // tn
    row_dn = pl.BlockSpec((tm, DN_WIDTH), lambda j, i: (i, 0))
    row_g = pl.BlockSpec((tm, ATT_GROUP_WIDTH), lambda j, i: (i, 0))
    return pl.pallas_call(
        _mix_kernel,
        grid=(d // tn, t // tm),
        in_specs=[row_dn, row_g,
                  pl.BlockSpec((DN_WIDTH, tn), lambda j, i: (0, j)),
                  pl.BlockSpec((ATT_GROUP_WIDTH, tn), lambda j, i: (0, j)),
                  pl.BlockSpec((tm, tn), lambda j, i: (i, g0 + j)),
                  pl.BlockSpec((tm, tn), lambda j, i: (i, g0 + d // tn + j))],
        out_specs=pl.BlockSpec((tm, tn), lambda j, i: (i, j)),
        out_shape=jax.ShapeDtypeStruct((t, d), BF16),
        compiler_params=_cparams(("parallel", "parallel")),
        name="mix",
    )(o_dn, o_att, w_dn, w_att, proj, proj)


def _layer_norm(y, g, b):
    mu = jnp.mean(y, axis=-1, keepdims=True)
    yc = y - mu
    var = jnp.mean(yc * yc, axis=-1, keepdims=True)
    return yc * lax.rsqrt(var + LN_EPS) * g + b


def _wo_kernel(m_ref, w_ref, x_ref, y_ref, *, alpha):
    y_ref[...] = alpha * x_ref[...] + _dot(m_ref[...], w_ref[...])


def _wo_residual(merged, w_o, x2, t, d, alpha):
    tm, tn = min(1024, t), min(1024, d)
    tile = pl.BlockSpec((tm, tn), lambda i, j: (i, j))
    return pl.pallas_call(
        functools.partial(_wo_kernel, alpha=alpha),
        grid=(t // tm, d // tn),
        in_specs=[pl.BlockSpec((tm, d), lambda i, j: (i, 0)),
                  pl.BlockSpec((d, tn), lambda i, j: (0, j)),
                  tile],
        out_specs=tile,
        out_shape=jax.ShapeDtypeStruct((t, d), F32),
        compiler_params=_cparams(("parallel", "parallel")),
        name="wo",
    )(merged, w_o, x2)


def _router_kernel(y_ref, g_ref, b_ref, w_ref, bias_ref,
                   x1_ref, x1p_ref, topi_ref, topw_ref, rank_ref, cnt_ref, run_ref):
    i = pl.program_id(0)
    tm, d = y_ref.shape

    @pl.when(i == 0)
    def _():
        run_ref[...] = jnp.zeros_like(run_ref)

    x1 = _layer_norm(y_ref[...], g_ref[...], b_ref[...])
    x1_ref[...] = x1
    x1p_ref[...] = _pack_halves(x1[:, :d // 2], x1[:, d // 2:])

    x_hi, x_mid = _split2(x1)
    w_hi, w_mid = _split2(w_ref[...])
    logits = _dot(x_hi, w_hi) + _dot(x_hi, w_mid) + _dot(x_mid, w_hi)
    s = _sigmoid(logits)
    sc = s + bias_ref[...]
    lane = lax.broadcasted_iota(jnp.int32, (tm, N_EXPERTS), 1)
    grp = lane // GROUP_SIZE
    neg = -jnp.inf

    def first_argmax(v):
        m = jnp.max(v, axis=-1, keepdims=True)
        idx = jnp.min(jnp.where(v == m, lane, N_EXPERTS), axis=-1, keepdims=True)
        return m, idx

    gscore = []
    for gi in range(N_GROUPS):
        vg = jnp.where(grp == gi, sc, neg)
        m1, i1 = first_argmax(vg)
        m2 = jnp.max(jnp.where(lane == i1, neg, vg), axis=-1, keepdims=True)
        gscore.append(m1 + m2)
    emask = jnp.zeros((tm, N_EXPERTS), jnp.bool_)
    for gi in range(N_GROUPS):
        ahead = jnp.zeros((tm, 1), jnp.int32)
        for gj in range(N_GROUPS):
            if gj == gi:
                continue
            beats = (gscore[gj] > gscore[gi]) | ((gscore[gj] == gscore[gi]) & (gj < gi))
            ahead = ahead + beats.astype(jnp.int32)
        emask = emask | ((grp == gi) & (ahead < TOPK_GROUPS))
    masked = jnp.where(emask, sc, neg)

    sel =jnp.zeros((tm, N_EXPERTS), jnp.bool_)
    idxs, vals = [], []
    for _ in range(TOP_K):
        _, ik = first_argmax(masked)
        hit = lane == ik
        sel = sel | hit
        masked = jnp.where(hit, neg, masked)
        idxs.append(ik)
        vals.append(jnp.sum(jnp.where(hit, s, 0.0), axis=-1, keepdims=True))
    wsum = vals[0]
    for v in vals[1:]:
        wsum = wsum + v

    sel_b = jnp.where(sel, 1.0, 0.0).astype(BF16)
    strict = jnp.where(lax.broadcasted_iota(jnp.int32, (tm, tm), 1)
                       < lax.broadcasted_iota(jnp.int32, (tm, tm), 0), 1.0, 0.0).astype(BF16)
    rank_excl = run_ref[0:1, :] + _dot(strict, sel_b)
    run_new = run_ref[0:1, :] + jnp.sum(sel_b.astype(F32), axis=0, keepdims=True)
    run_ref[...] = jnp.broadcast_to(run_new, run_ref.shape)
    cnt_ref[...] = jnp.broadcast_to(run_new, cnt_ref.shape).astype(jnp.int32)

    topi = jnp.zeros((tm, N_EXPERTS), jnp.int32)
    topw = jnp.zeros((tm, N_EXPERTS), F32)
    rnk = jnp.zeros((tm, N_EXPERTS), F32)
    for kk in range(TOP_K):
        hit = lane == idxs[kk]
        rk = jnp.sum(jnp.where(hit, rank_excl, 0.0), axis=-1, keepdims=True)
        topi = jnp.where(lane == kk, idxs[kk], topi)
        topw = jnp.where(lane == kk, vals[kk] / wsum * ROUTED_SCALE, topw)
        rnk = jnp.where(lane == kk, rk, rnk)
    topi_ref[...] = topi
    topw_ref[...] = topw
    rank_ref[...] = rnk.astype(jnp.int32)


def _ln_router(y, g, b, w_router, bias, t, d):
    tm = min(256, t)
    row = pl.BlockSpec((tm, N_EXPERTS), lambda i: (i, 0))
    full = pl.BlockSpec((tm, d), lambda i: (i, 0))
    vec = pl.BlockSpec((1, d), lambda i: (0, 0))
    return pl.pallas_call(
        _router_kernel,
        grid=(t // tm,),
        in_specs=[full, vec, vec,
                  pl.BlockSpec((d, N_EXPERTS), lambda i: (0, 0)),
                  pl.BlockSpec((1, N_EXPERTS), lambda i: (0, 0))],
        out_specs=[full, pl.BlockSpec((tm, d // 2), lambda i: (i, 0)), row, row, row,
                   pl.BlockSpec((SUBLANES, N_EXPERTS), lambda i: (0, 0))],
        out_shape=[jax.ShapeDtypeStruct((t, d), F32),
                   jax.ShapeDtypeStruct((t, d // 2), jnp.uint32),
                   jax.ShapeDtypeStruct((t, N_EXPERTS), jnp.int32),
                   jax.ShapeDtypeStruct((t, N_EXPERTS), F32),
                   jax.ShapeDtypeStruct((t, N_EXPERTS), jnp.int32),
                   jax.ShapeDtypeStruct((SUBLANES, N_EXPERTS), jnp.int32)],
        scratch_shapes=[pltpu.VMEM((SUBLANES, N_EXPERTS), F32)],
        compiler_params=_cparams(("arbitrary",)),
        name="ln_router",
    )(y, g.reshape(1, d).astype(F32), b.reshape(1, d).astype(F32), w_router.astype(F32),
      bias.reshape(1, N_EXPERTS).astype(F32))


def _slot_kernel(topi_ref, rank_ref, start_ref, pos_ref):
    topi = topi_ref[...]
    lane = lax.broadcasted_iota(jnp.int32, topi.shape, 1)
    start = jnp.broadcast_to(start_ref[...], topi.shape)
    pos = rank_ref[...]
    for kk in range(TOP_K):
        seg = jnp.sum(jnp.where(lane == topi[:, kk:kk + 1], start, 0.0), axis=-1, keepdims=True)
        pos = jnp.where(lane == kk, pos + seg.astype(jnp.int32), pos)
    pos_ref[...] = pos


def _slots(topi, rank, pad_start, t):
    tm = min(1024, t)
    row = pl.BlockSpec((tm, N_EXPERTS), lambda i: (i, 0))
    return pl.pallas_call(
        _slot_kernel,
        grid=(t // tm,),
        in_specs=[row, row, pl.BlockSpec((1, N_EXPERTS), lambda i: (0, 0))],
        out_specs=row,
        out_shape=jax.ShapeDtypeStruct((t, N_EXPERTS), jnp.int32),
        compiler_params=_cparams(("parallel",)),
        name="slots",
    )(topi, rank, pad_start.astype(F32).reshape(1, N_EXPERTS))


def _row_copy(src_ref, src_row, dst_ref, dst_row, sem):
    return pltpu.make_async_copy(src_ref.at[pl.ds(src_row, 1)], dst_ref.at[pl.ds(dst_row, 1)], sem)


def _dispatch_kernel(pos_ref, x_ref, xs_ref, sem):
    i = pl.program_id(0)
    tb = x_ref.shape[0]
    base = i * (tb * TOP_K)

    def start(r, carry):
        for kk in range(TOP_K):
            _row_copy(x_ref, r, xs_ref, pos_ref[base + r * TOP_K + kk], sem).start()
        return carry

    lax.fori_loop(0, tb, start, 0)
    for kk in range(TOP_K):
        pltpu.make_async_copy(x_ref, xs_ref.at[pl.ds(0, tb)], sem).wait()


def _dispatch(pos_flat, x1, n_slots, t, d):
    tb = min(256, t)
    return pl.pallas_call(
        _dispatch_kernel,
        grid_spec=pltpu.PrefetchScalarGridSpec(
            num_scalar_prefetch=1,
            grid=(t // tb,),
            in_specs=[pl.BlockSpec((tb, d), lambda i, pos: (i, 0))],
            out_specs=pl.BlockSpec(memory_space=pl.ANY),
            scratch_shapes=[pltpu.SemaphoreType.DMA],
        ),
        out_shape=jax.ShapeDtypeStruct((n_slots, d), x1.dtype),
        compiler_params=_cparams(("arbitrary",)),
        name="dispatch",
    )(pos_flat, x1)


def _expert_kernel(be_ref, nv_ref, first_ref, nxt_ref, nu_ref, xs_ref, wgu_hbm, wdn_hbm, y_ref,
                   gu_stage, dn_stage, gu_cache, dn_cache, sem):
    b = pl.program_id(0)

    def weight_copies(e):
        return (pltpu.make_async_copy(wgu_hbm.at[e], gu_stage, sem.at[0]),
                pltpu.make_async_copy(wdn_hbm.at[e], dn_stage, sem.at[1]))

    @pl.when(b < nu_ref[0])
    def _():
        rows, hd = xs_ref.shape
        kc = min(EXPERT_CHUNK, hd)
        e = be_ref[b]

        @pl.when(first_ref[b] == 1)
        def _():
            @pl.when(b == 0)
            def _():
                for cp in weight_copies(e):
                    cp.start()

            for cp in weight_copies(e):
                cp.wait()
            for c0 in range(0, 2 * hd, kc):
                for j in range(EXPERT_FF // LANES):
                    gu_cache[c0:c0 + kc, 2 * j * LANES:(2 * j + 1) * LANES] = (
                        gu_stage[c0:c0 + kc, j * LANES:(j + 1) * LANES].astype(BF16))
                    gu_cache[c0:c0 + kc, (2 * j + 1) * LANES:(2 * j + 2) * LANES] = (
                        gu_stage[c0:c0 + kc, EXPERT_FF + j * LANES:EXPERT_FF + (j + 1) * LANES]
                        .astype(BF16))
                dn_cache[:, c0:c0 + kc] = dn_stage[:, c0:c0 + kc].astype(BF16)

            @pl.when(nxt_ref[b] >= 0)
            def _():
                for cp in weight_copies(nxt_ref[b]):
                    cp.start()

        rid = lax.broadcasted_iota(jnp.int32, (rows, 1), 0)
        valid = rid < nv_ref[b]
        lo, hi = _unpack_halves(xs_ref[...])
        x_lo = jnp.where(valid, lo, 0.0).astype(BF16)
        x_hi = jnp.where(valid, hi, 0.0).astype(BF16)
        acts = []
        for j in range(EXPERT_FF // LANES):
            cs = slice(2 * j * LANES, (2 * j + 2) * LANES)
            hgu = _dot(x_lo, gu_cache[:hd, cs]) + _dot(x_hi, gu_cache[hd:, cs])
            gate, up = hgu[:, :LANES], hgu[:, LANES:]
            acts.append((gate * _sigmoid(gate) * up).astype(BF16))
        act = jnp.concatenate(acts, axis=1)
        nc = min(EXPERT_OUT_CHUNK, hd)
        for c0 in range(0, hd, nc):
            y_ref[:, c0:c0 + nc] = _pack_halves(_dot(act, dn_cache[:, c0:c0 + nc]),
                                                _dot(act, dn_cache[:, hd + c0:hd + c0 + nc]))


def _experts(blk_expert, blk_valid, blk_first, blk_next, n_used, xs, w_gu, w_dn, d):
    n_slots = xs.shape[0]
    nb = n_slots // MOE_ROWS

    def row_map(b, be, nv, fi, nx, nu):
        return (jnp.minimum(b, nu[0] - 1), 0)

    return pl.pallas_call(
        _expert_kernel,
        grid_spec=pltpu.PrefetchScalarGridSpec(
            num_scalar_prefetch=5,
            grid=(nb,),
            in_specs=[pl.BlockSpec((MOE_ROWS, d // 2), row_map),
                      pl.BlockSpec(memory_space=pl.ANY),
                      pl.BlockSpec(memory_space=pl.ANY)],
            out_specs=pl.BlockSpec((MOE_ROWS, d // 2), row_map),
            scratch_shapes=[pltpu.VMEM((d, 2 * EXPERT_FF), w_gu.dtype),
                            pltpu.VMEM((EXPERT_FF, d), w_dn.dtype),
                            pltpu.VMEM((d, 2 * EXPERT_FF), BF16),
                            pltpu.VMEM((EXPERT_FF, d), BF16),
                            pltpu.SemaphoreType.DMA((2,))],
        ),
        out_shape=jax.ShapeDtypeStruct((n_slots, d // 2), jnp.uint32),
        compiler_params=_cparams(("arbitrary",)),
        name="experts",
    )(blk_expert, blk_valid, blk_first, blk_next, n_used, xs, w_gu, w_dn)


def _shared_kernel(x_ref, wgu_ref, wdn_ref, y_ref):
    hd = x_ref.shape[1]
    lo, hi = _unpack_halves(x_ref[...])
    hgu = _dot(lo.astype(BF16), wgu_ref[:hd, :]) + _dot(hi.astype(BF16), wgu_ref[hd:, :])
    gate, up = hgu[:, :EXPERT_FF], hgu[:, EXPERT_FF:]
    act = (gate * _sigmoid(gate) * up).astype(BF16)
    y_ref[...] = _dot(act, wdn_ref[...]).astype(y_ref.dtype)


def _shared(x1p, w_gu, w_dn, t, d):
    tm = min(512, t)
    return pl.pallas_call(
        _shared_kernel,
        grid=(t // tm,),
        in_specs=[pl.BlockSpec((tm, d // 2), lambda i: (i, 0)),
                  pl.BlockSpec((d, 2 * EXPERT_FF), lambda i: (0, 0)),
                  pl.BlockSpec((EXPERT_FF, d), lambda i: (0, 0))],
        out_specs=pl.BlockSpec((tm, d), lambda i: (i, 0)),
        out_shape=jax.ShapeDtypeStruct((t, d), BF16),
        compiler_params=_cparams(("parallel",)),
        name="shared",
    )(x1p, w_gu, w_dn)


def _combine_kernel(pos_ref, x_ref, ysh_ref, topw_ref, g_ref, b_ref, y_hbm, out_ref, buf_ref, sem,
                    *, alpha):
    i = pl.program_id(0)
    n = pl.num_programs(0)
    tb = x_ref.shape[0]

    def gather(tile, slot):
        base = tile * (tb * TOP_K)

        def body(r, carry):
            for kk in range(TOP_K):
                _row_copy(y_hbm, pos_ref[base + r * TOP_K + kk], buf_ref.at[slot, kk], r,
                          sem.at[slot]).start()
            return carry

        lax.fori_loop(0, tb, body, 0)

    @pl.when(i == 0)
    def _():
        gather(0, 0)

    @pl.when(i + 1 < n)
    def _():
        gather(i + 1, (i + 1) % 2)

    slot = i % 2
    for kk in range(TOP_K):
        pltpu.make_async_copy(y_hbm.at[pl.ds(0, tb)], buf_ref.at[slot, kk], sem.at[slot]).wait()
    topw = topw_ref[...]
    hd = x_ref.shape[1] // 2
    acc_lo = alpha * x_ref[:, :hd] + ysh_ref[:, :hd].astype(F32)
    acc_hi = alpha * x_ref[:, hd:] + ysh_ref[:, hd:].astype(F32)
    for kk in range(TOP_K):
        lo, hi = _unpack_halves(buf_ref[slot, kk])
        acc_lo = acc_lo + topw[:, kk:kk + 1] * lo
        acc_hi = acc_hi + topw[:, kk:kk + 1] * hi
    inv_d = 1.0 / (2 * hd)
    mu = (jnp.sum(acc_lo, axis=-1, keepdims=True) + jnp.sum(acc_hi, axis=-1, keepdims=True)) * inv_d
    c_lo, c_hi = acc_lo - mu, acc_hi - mu
    var = (jnp.sum(c_lo * c_lo, axis=-1, keepdims=True)
           + jnp.sum(c_hi * c_hi, axis=-1, keepdims=True)) * inv_d
    rstd = lax.rsqrt(var + LN_EPS)
    out_ref[:, :hd] = c_lo * rstd * g_ref[:, :hd] + b_ref[:, :hd]
    out_ref[:, hd:] = c_hi * rstd * g_ref[:, hd:] + b_ref[:, hd:]


def _combine(pos_flat, x1, ysh, topw, g, b, y, t, d, alpha):
    tb = min(128, t)
    row = pl.BlockSpec((tb, d), lambda i, pos: (i, 0))
    vec = pl.BlockSpec((1, d), lambda i, pos: (0, 0))
    return pl.pallas_call(
        functools.partial(_combine_kernel, alpha=alpha),
        grid_spec=pltpu.PrefetchScalarGridSpec(
            num_scalar_prefetch=1,
            grid=(t // tb,),
            in_specs=[row, row, pl.BlockSpec((tb, N_EXPERTS), lambda i, pos: (i, 0)), vec, vec,
                      pl.BlockSpec(memory_space=pl.ANY)],
            out_specs=row,
            scratch_shapes=[pltpu.VMEM((2, TOP_K, tb, d // 2), jnp.uint32),
                            pltpu.SemaphoreType.DMA((2,))],
        ),
        out_shape=jax.ShapeDtypeStruct((t, d), F32),
        compiler_params=_cparams(("arbitrary",)),
        name="combine",
    )(pos_flat, x1, ysh, topw, g.reshape(1, d).astype(F32), b.reshape(1, d).astype(F32), y)


def _rope_tables(t):
    half = ATT_HEAD_DIM // 2
    inv_freq = ROPE_THETA ** (-jnp.arange(half, dtype=F32) / half)
    ang = jnp.arange(t, dtype=F32)[:, None] * inv_freq[None, :]
    cos, sin = jnp.cos(ang), jnp.sin(ang)
    return jnp.concatenate([cos, cos], -1), jnp.concatenate([-sin, sin], -1)


def _token_mixer(x2, w_in, conv_w, a_log, dt_bias, dn_norm_w, w_dn_branch, w_att_branch, w_o,
                 alpha):
    t, d = x2.shape
    n_main = 4 * DN_WIDTH
    n_ba = 2 * DN_HEADS
    x_b = x2.astype(BF16)
    w_t = w_in.T
    proj_dn = _proj(x_b, w_t, 0, n_main, "proj_dn")
    proj_att = _proj(x_b, w_t, n_main + n_ba, w_t.shape[0] - n_main - n_ba, "proj_att")
    betab, gcb, gct = _dn_gates(x2, w_t, n_main, a_log, dt_bias)
    u, w, qe, kd, intra, egl = _dn_prep(proj_dn, conv_w.astype(F32), betab, gcb, gct, t)
    o_dn = _dn_scan(u, w, qe, kd, intra, egl, proj_dn, dn_norm_w, t)

    cos_t, sin_t = _rope_tables(t)
    o_att = _attention(proj_att, cos_t, sin_t, t)
    merged = _mix(o_dn, o_att, w_dn_branch.astype(BF16), w_att_branch.astype(BF16), proj_att, t, d)
    return _wo_residual(merged, w_o.astype(BF16), x2, t, d, alpha)


def _moe(y_mix, ln1_g, ln1_b, w_router, router_bias, w_exp_gate_up, w_exp_down, w_sh_gate_up,
         w_sh_down, ln_g, ln_b, alpha):
    t, d = y_mix.shape
    r = MOE_ROWS
    x1, x1p, topi, topw, rank, cnt = _ln_router(y_mix, ln1_g, ln1_b, w_router, router_bias, t, d)
    counts = cnt[0]
    padded = (counts + r - 1) // r * r
    pad_end = jnp.cumsum(padded)
    pad_start = pad_end - padded
    nb = (t * TOP_K) // r + N_EXPERTS
    blk_row0 = jnp.arange(nb, dtype=jnp.int32) * r
    in_blk = (blk_row0[:, None] >= pad_start[None, :]) & (blk_row0[:, None] < pad_end[None, :])
    expert_ids = jnp.arange(N_EXPERTS, dtype=jnp.int32)
    blk_expert = jnp.where(blk_row0 < pad_end[-1],
                           jnp.sum(jnp.where(in_blk, expert_ids[None, :], 0), axis=1),
                           N_EXPERTS - 1).astype(jnp.int32)
    blk_valid = jnp.sum(jnp.where(in_blk, jnp.clip(counts[None, :] - (blk_row0[:, None] - pad_start[None, :]), 0, r), 0),
                        axis=1).astype(jnp.int32)
    n_used = (pad_end[-1:] // r).astype(jnp.int32)
    blk_ids = jnp.arange(nb, dtype=jnp.int32)
    prev_expert = jnp.concatenate([jnp.full((1,), -1, jnp.int32), blk_expert[:-1]])
    blk_first = (blk_expert != prev_expert).astype(jnp.int32)
    after = jnp.sum(jnp.where(in_blk, pad_end[None, :] // r, 0), axis=1)
    after_expert = jnp.sum(jnp.where(after[:, None] == blk_ids[None, :], blk_expert[None, :], 0), axis=1)
    blk_next = jnp.where(after < n_used[0], after_expert, -1).astype(jnp.int32)
    pos_flat = _slots(topi, rank, pad_start, t)[:, :TOP_K].reshape(-1)

    xs = _dispatch(pos_flat, x1p, nb * r, t, d // 2)
    y = _experts(blk_expert, blk_valid, blk_first, blk_next, n_used, xs, w_exp_gate_up, w_exp_down, d)
    ysh = _shared(x1p, w_sh_gate_up.astype(BF16), w_sh_down.astype(BF16), t, d)
    return _combine(pos_flat, x1, ysh, topw, ln_g, ln_b, y, t, d, alpha)


def kernel(x, w_in, conv_w, a_log, dt_bias, dn_norm_w, w_dn_branch, w_att_branch, w_o, ln1_g, ln1_b,
           w_router, router_bias, w_exp_gate_up, w_exp_down, w_sh_gate_up, w_sh_down, ln2_g, ln2_b):
    bsz, t, d = x.shape
    depth = w_in.shape[0]
    alpha = (2.0 * depth) ** 0.25
    outs = []
    for bi in range(bsz):
        xb = x[bi]
        for l in range(depth):
            y_mix = _token_mixer(xb, w_in[l], conv_w[l], a_log[l], dt_bias[l], dn_norm_w[l],
                                 w_dn_branch[l], w_att_branch[l], w_o[l], alpha)
            xb = _moe(y_mix, ln1_g[l], ln1_b[l], w_router[l], router_bias[l], w_exp_gate_up[l],
                      w_exp_down[l], w_sh_gate_up[l], w_sh_down[l], ln2_g[l], ln2_b[l], alpha)
        outs.append(xb)
    return jnp.stack(outs, 0)
```

```python
import functools

import jax
import jax.numpy as jnp
from jax import lax
from jax.experimental import pallas as pl
from jax.experimental.pallas import tpu as pltpu

F32 = jnp.float32
BF16 = jnp.bfloat16

DN_HEADS = 16
DN_HEAD_DIM = 128
DN_WIDTH = DN_HEADS * DN_HEAD_DIM
DN_CONV = 4
DN_CHUNK = 64
ATT_GROUPS = ((128, 1), (512, 4), (2048, 16))
ATT_HEADS_PER_GROUP = 8
ATT_HEAD_DIM = 128
ATT_GROUP_WIDTH = ATT_HEADS_PER_GROUP * ATT_HEAD_DIM
ATT_WIDTH = len(ATT_GROUPS) * ATT_GROUP_WIDTH
ATT_BLOCK = 128
ROPE_THETA = 10000.0
N_EXPERTS = 128
TOP_K = 8
N_GROUPS = 8
GROUP_SIZE = N_EXPERTS // N_GROUPS
TOPK_GROUPS = 4
EXPERT_FF = 384
ROUTED_SCALE = 2.5
LN_EPS = 1e-5
RMS_EPS = 1e-6
L2_EPS = 1e-6

LANES = 128
SUBLANES = 8
BF16_SUBLANES = 16
VMEM_LIMIT = 56 * 1024 * 1024

DN_PREP_ROWS = 1024
MOE_ROWS = 256
EXPERT_CHUNK = 1024
EXPERT_OUT_CHUNK = 512
NEG_BIG = -1e30


def _cparams(sem, vmem=VMEM_LIMIT):
    return pltpu.CompilerParams(dimension_semantics=sem, vmem_limit_bytes=vmem)


def _split3(a):
    hi = a.astype(BF16)
    r1 = a - hi.astype(F32)
    mid = r1.astype(BF16)
    lo = (r1 - mid.astype(F32)).astype(BF16)
    return hi, mid, lo


def _split2(a):
    hi = a.astype(BF16)
    return hi, (a - hi.astype(F32)).astype(BF16)


def _dot(a, b):
    return jnp.dot(a, b, preferred_element_type=F32)


def _dot_nt(a, b):
    return lax.dot_general(a, b, (((1,), (1,)), ((), ())), preferred_element_type=F32)


def _dot_tn(a, b):
    return lax.dot_general(a, b, (((0,), (0,)), ((), ())), preferred_element_type=F32)


def _sigmoid(x):
    return 1.0 / (1.0 + jnp.exp(-x))


def _pack_halves(lo, hi):
    lo_b = lax.bitcast_convert_type(lo.astype(BF16).astype(F32), jnp.uint32)
    hi_b = lax.bitcast_convert_type(hi.astype(BF16).astype(F32), jnp.uint32)
    return (hi_b & jnp.uint32(0xFFFF0000)) | (lo_b >> 16)


def _unpack_halves(u):
    lo = lax.bitcast_convert_type(u << 16, F32)
    hi = lax.bitcast_convert_type(u & jnp.uint32(0xFFFF0000), F32)
    return lo, hi


def _proj_kernel(x_ref, wt_ref, o_ref):
    o_ref[...] = _dot_nt(x_ref[...], wt_ref[...].astype(BF16)).astype(o_ref.dtype)


def _proj(x_b, w_t, row0, n, name):
    m, k = x_b.shape
    tm, tn = min(1024, m), min(512, n)
    return pl.pallas_call(
        _proj_kernel,
        grid=(m // tm, n // tn),
        in_specs=[pl.BlockSpec((tm, k), lambda i, j: (i, 0)),
                  pl.BlockSpec((pl.Element(tn), pl.Element(k)),
                               lambda i, j: (pl.multiple_of(row0 + j * tn, SUBLANES), 0))],
        out_specs=pl.BlockSpec((tm, tn), lambda i, j: (i, j)),
        out_shape=jax.ShapeDtypeStruct((m, n), BF16),
        compiler_params=_cparams(("parallel", "parallel")),
        name=name,
    )(x_b, w_t)


def _gates_kernel(x_ref, w_ref, prm_ref, betab_ref, gcb_ref, gct_ref):
    tm = x_ref.shape[0]
    x_hi, x_mid = _split2(x_ref[...])
    w_hi, w_mid = _split2(w_ref[...])
    logits = _dot_nt(x_hi, w_hi) + _dot_nt(x_hi, w_mid) + _dot_nt(x_mid, w_hi)

    def softplus(v):
        return jnp.maximum(v, 0.0) + jnp.log(1.0 + jnp.exp(-jnp.abs(v)))

    prm = prm_ref[...]
    beta = _sigmoid(logits)
    g = prm[0:1, :] * softplus(logits + prm[1:2, :])

    ri = lax.broadcasted_iota(jnp.int32, (tm, tm), 0)
    ci = lax.broadcasted_iota(jnp.int32, (tm, tm), 1)
    same = (ri // DN_CHUNK) == (ci // DN_CHUNK)
    lower = jnp.where(same & (ci <= ri), 1.0, 0.0).astype(BF16)
    gc = jnp.zeros((tm, LANES), F32)
    for part in _split3(g):
        gc = gc + _dot(lower, part)
    gct_ref[...] = gc.T[DN_HEADS:2 * DN_HEADS, :]
    for h in range(DN_HEADS):
        sl = slice(h * LANES, (h + 1) * LANES)
        betab_ref[:, sl] = jnp.broadcast_to(beta[:, h:h + 1], (tm, LANES))
        gcb_ref[:, sl] = jnp.broadcast_to(gc[:, DN_HEADS + h:DN_HEADS + h + 1], (tm, LANES))


def _dn_gates(x2, w_t, col0, a_log, dt_bias):
    t, d = x2.shape
    tm = min(512, t)
    neg_a = -jnp.exp(a_log.astype(F32))
    prm = jnp.zeros((SUBLANES, LANES), F32)
    prm = prm.at[0, DN_HEADS:2 * DN_HEADS].set(neg_a).at[1, DN_HEADS:2 * DN_HEADS].set(dt_bias.astype(F32))
    return pl.pallas_call(
        _gates_kernel,
        grid=(t // tm,),
        in_specs=[pl.BlockSpec((tm, d), lambda i: (i, 0)),
                  pl.BlockSpec((LANES, d), lambda i: (col0 // LANES, 0)),
                  pl.BlockSpec((SUBLANES, LANES), lambda i: (0, 0))],
        out_specs=[pl.BlockSpec((tm, DN_WIDTH), lambda i: (i, 0)),
                   pl.BlockSpec((tm, DN_WIDTH), lambda i: (i, 0)),
                   pl.BlockSpec((DN_HEADS, tm), lambda i: (0, i))],
        out_shape=[jax.ShapeDtypeStruct((t, DN_WIDTH), F32),
                   jax.ShapeDtypeStruct((t, DN_WIDTH), F32),
                   jax.ShapeDtypeStruct((DN_HEADS, t), F32)],
        compiler_params=_cparams(("parallel",)),
        name="dn_gates",
    )(x2, w_t.astype(F32), prm)


def _dn_prep_kernel(q_ref, k_ref, v_ref, hq_ref, hk_ref, hv_ref, cq_ref, ck_ref, cv_ref,
                    betab_ref, gcb_ref, gct_ref,
                    u_ref, w_ref, qe_ref, kd_ref, intra_ref, egl_ref, cbuf):
    i = pl.program_id(1)
    rows = q_ref.shape[0]
    c = DN_CHUNK
    halo_rows = hq_ref.shape[0]

    def conv_silu(slot, x_ref, halo_ref, cw_ref):
        cbuf[slot, 0:halo_rows, :] = jnp.where(i > 0, halo_ref[...].astype(F32), 0.0)
        cbuf[slot, halo_rows:halo_rows + rows, :] = x_ref[...].astype(F32)
        cw = cw_ref[...]
        y = jnp.zeros((rows, LANES), F32)
        for j in range(DN_CONV):
            off = halo_rows - (DN_CONV - 1) + j
            y = y + cw[j:j + 1, :] * cbuf[slot, off:off + rows, :]
        return y * _sigmoid(y)

    def l2n(a):
        return a * lax.rsqrt(jnp.sum(a * a, axis=-1, keepdims=True) + L2_EPS)

    q = l2n(conv_silu(0, q_ref, hq_ref, cq_ref)) * (DN_HEAD_DIM ** -0.5)
    k = l2n(conv_silu(1, k_ref, hk_ref, ck_ref))
    v = conv_silu(2, v_ref, hv_ref, cv_ref)
    beta = betab_ref[...]
    gcol = gcb_ref[...]
    grow_all = gct_ref[...]

    ii = lax.broadcasted_iota(jnp.int32, (c, c), 0)
    jj = lax.broadcasted_iota(jnp.int32, (c, c), 1)
    eye = jnp.where(ii == jj, 1.0, 0.0)
    n_chunks = rows // c

    decays, a_mats = [], []
    kbs, egcs = [], []
    for n in range(n_chunks):
        rs = slice(n * c, (n + 1) * c)
        kc = k[rs]
        kb = kc * beta[rs]
        grow = grow_all[:, rs]
        diff = jnp.where(ii >= jj, gcol[rs, :c] - grow, 0.0)
        decay = jnp.where(ii >= jj, jnp.exp(diff), 0.0)
        kk = _dot_nt(kb.astype(BF16), kc.astype(BF16))
        a_mats.append(jnp.where(ii > jj, kk * decay, 0.0))
        decays.append(decay)
        kbs.append(kb)
        egcs.append(jnp.exp(gcol[rs]))

    invs = []
    for n in range(n_chunks):
        invs.append(eye - jnp.where((ii // 2 == jj // 2) & (ii > jj), a_mats[n], 0.0))
    s = 2
    while s < c:
        sel = ((ii // (2 * s)) == (jj // (2 * s))) & ((ii // s) > (jj // s))
        for n in range(n_chunks):
            d_b = invs[n].astype(BF16)
            x_b = jnp.where(sel, a_mats[n], 0.0).astype(BF16)
            t1 = _dot(d_b, x_b).astype(BF16)
            invs[n] = invs[n] - _dot(t1, d_b)
        s *= 2

    for n in range(n_chunks):
        rs = slice(n * c, (n + 1) * c)
        t_b = invs[n].astype(BF16)
        u_ref[rs, :] = _dot(t_b, (v[rs] * beta[rs]).astype(BF16)).astype(u_ref.dtype)
        w_ref[rs, :] = _dot(t_b, (kbs[n] * egcs[n]).astype(BF16)).astype(w_ref.dtype)
        qc = q[rs]
        kc = k[rs]
        qk = _dot_nt(qc.astype(BF16), kc.astype(BF16))
        intra = qk * decays[n]
        intra_ref[rs, :] = jnp.concatenate([intra, jnp.zeros_like(intra)], axis=1).astype(intra_ref.dtype)
        qe_ref[rs, :] = (qc * egcs[n]).astype(qe_ref.dtype)
        glast = gcol[(n + 1) * c - 1:(n + 1) * c, :]
        kd_ref[rs, :] = (kc * jnp.exp(glast - gcol[rs])).astype(kd_ref.dtype)
        egl_ref[n * SUBLANES:(n + 1) * SUBLANES, :] = jnp.broadcast_to(jnp.exp(glast),
                                                                       (SUBLANES, LANES))


def _dn_prep(proj, conv_w, betab, gcb, gct, t):
    rows = min(DN_PREP_ROWS, t)
    hb = BF16_SUBLANES
    nq = DN_WIDTH // LANES

    def blk(off):
        return pl.BlockSpec((rows, LANES), lambda h, i, off=off: (i, off + h))

    def halo(off):
        return pl.BlockSpec((hb, LANES),
                            lambda h, i, off=off: (jnp.maximum(i * (rows // hb) - 1, 0), off + h))

    def cw(off):
        return pl.BlockSpec((DN_CONV, LANES), lambda h, i, off=off: (0, off + h))

    per_head = pl.BlockSpec((rows, LANES), lambda h, i: (i, h))
    outs = [jax.ShapeDtypeStruct((t, DN_WIDTH), BF16)] * 5
    outs.append(jax.ShapeDtypeStruct((t // DN_CHUNK * SUBLANES, DN_WIDTH), F32))
    return pl.pallas_call(
        _dn_prep_kernel,
        grid=(DN_HEADS, t // rows),
        in_specs=[blk(0), blk(nq), blk(2 * nq), halo(0), halo(nq), halo(2 * nq),
                  cw(0), cw(nq), cw(2 * nq), per_head, per_head,
                  pl.BlockSpec((None, 1, rows), lambda h, i: (h, 0, i))],
        out_specs=[per_head] * 5 + [pl.BlockSpec((rows // DN_CHUNK * SUBLANES, LANES), lambda h, i: (i, h))],
        out_shape=outs,
        scratch_shapes=[pltpu.VMEM((3, hb + rows, LANES), F32)],
        compiler_params=_cparams(("parallel", "parallel")),
        name="dn_prep",
    )(proj, proj, proj, proj, proj, proj, conv_w, conv_w, conv_w, betab, gcb,
      gct.reshape(DN_HEADS, 1, t))


def _dn_scan_kernel(u_ref, w_ref, qe_ref, kd_ref, intra_ref, egl_ref, z_ref, nw_ref, o_ref, s_ref):
    i = pl.program_id(0)
    c = DN_CHUNK
    n_chunks = u_ref.shape[0] // c

    @pl.when(i == 0)
    def _():
        s_ref[...] = jnp.zeros_like(s_ref)

    nw = nw_ref[...]

    pw = 2 * LANES
    first = lax.broadcasted_iota(jnp.int32, (c, pw), 1) < LANES
    zeros_s = jnp.zeros((DN_HEAD_DIM, LANES), BF16)
    zeros_v = jnp.zeros((c, pw), BF16)

    def chunk(n, carry):
        r0 = pl.multiple_of(n * c, c)
        e0 = pl.multiple_of(n * SUBLANES, SUBLANES)
        for p in range(DN_HEADS // 2):
            ps = slice(p * pw, (p + 1) * pw)
            s_p = s_ref[p]
            s_b = s_p.astype(BF16)
            s_diag = jnp.concatenate(
                [jnp.concatenate([s_b[:, :LANES], zeros_s], axis=1),
                 jnp.concatenate([zeros_s, s_b[:, LANES:]], axis=1)], axis=0)
            wq = jnp.concatenate([w_ref[pl.ds(r0, c), ps], qe_ref[pl.ds(r0, c), ps]], axis=0)
            ws_qs = _dot(wq, s_diag)
            v_new = u_ref[pl.ds(r0, c), ps].astype(F32) - ws_qs[:c]
            v_b = v_new.astype(BF16)
            v_diag = jnp.concatenate([jnp.where(first, v_b, zeros_v), zeros_v,
                                      jnp.where(first, zeros_v, v_b), zeros_v], axis=0)
            o = ws_qs[c:] + _dot(intra_ref[pl.ds(r0, c), ps], v_diag)
            eg = egl_ref[pl.ds(e0, SUBLANES), ps]
            s_dec = (s_p.reshape(DN_HEAD_DIM // SUBLANES, SUBLANES, pw) * eg[None]
                     ).reshape(DN_HEAD_DIM, pw)
            kv = _dot_tn(kd_ref[pl.ds(r0, c), ps], v_b)
            s_ref[p] = s_dec + jnp.concatenate([kv[:LANES, :LANES], kv[LANES:, LANES:]], axis=1)
            z = z_ref[pl.ds(r0, c), ps].astype(F32)
            gate = z * _sigmoid(z)
            for hh in range(2):
                ls = slice(hh * LANES, (hh + 1) * LANES)
                o_h = o[:, ls]
                o_h = o_h * lax.rsqrt(jnp.mean(o_h * o_h, axis=-1, keepdims=True) + RMS_EPS)
                o_ref[pl.ds(r0, c), p * pw + hh * LANES:p * pw + (hh + 1) * LANES] = (
                    o_h * nw * gate[:, ls]).astype(o_ref.dtype)
        return carry

    lax.fori_loop(0, n_chunks, chunk, 0)


def _dn_scan(u, w, qe, kd, intra, egl, proj, norm_w, t):
    rows = min(512, t)
    full = pl.BlockSpec((rows, DN_WIDTH), lambda i: (i, 0))
    return pl.pallas_call(
        _dn_scan_kernel,
        grid=(t // rows,),
        in_specs=[full, full, full, full, full,
                  pl.BlockSpec((rows // DN_CHUNK * SUBLANES, DN_WIDTH), lambda i: (i, 0)),
                  pl.BlockSpec((rows, DN_WIDTH), lambda i: (i, 3)),
                  pl.BlockSpec((1, LANES), lambda i: (0, 0))],
        out_specs=full,
        out_shape=jax.ShapeDtypeStruct((t, DN_WIDTH), BF16),
        scratch_shapes=[pltpu.VMEM((DN_HEADS // 2, DN_HEAD_DIM, 2 * DN_HEAD_DIM), F32)],
        compiler_params=_cparams(("arbitrary",)),
        name="dn_scan",
    )(u, w, qe, kd, intra, egl, proj, norm_w.reshape(1, LANES).astype(F32))


def _attn_kernel(*refs, dilation, span, n_earlier):
    q_ref, k_ref, v_ref, cos_ref, sin_ref = refs[:5]
    earlier = refs[5:5 + 2 * n_earlier]
    out_refs = refs[5 + 2 * n_earlier:-5]
    kbuf, vbuf, qbuf, obuf, lbuf = refs[-5:]
    i = pl.program_id(1)
    bt, width = q_ref.shape
    d = dilation
    blk = ATT_BLOCK
    prev = d * blk
    half = ATT_HEAD_DIM // 2
    scale = ATT_HEAD_DIM ** -0.5

    heads = width // LANES

    @pl.when(i == 0)
    def _():
        kbuf[:, 0:prev, :] = jnp.zeros((heads, prev, LANES), F32)
        vbuf[:, 0:prev, :] = jnp.zeros((heads, prev, LANES), F32)

    cos, sin = cos_ref[...], sin_ref[...]

    def rope(a):
        a = a.astype(F32)
        return a * cos + pltpu.roll(a, half, 1) * sin

    for h in range(heads):
        ls = slice(h * LANES, (h + 1) * LANES)
        qbuf[h] = rope(q_ref[:, ls]) * scale
        kbuf[h, prev:prev + bt, :] = rope(k_ref[:, ls])
        vbuf[h, prev:prev + bt, :] = v_ref[:, ls].astype(F32)

    qi = lax.broadcasted_iota(jnp.int32, (blk, 2 * blk), 0) + blk
    ki = lax.broadcasted_iota(jnp.int32, (blk, 2 * blk), 1)
    dist = qi - ki
    band = (dist >= 0) & (dist <= span)
    band_first = band & ((ki >= blk) | (i > 0))

    def rows(start, size):
        return pl.ds(start, size, stride=d) if d > 1 else pl.ds(start, size)

    for j in range(bt // prev):
        for r in range(d):
            r0 = j * prev + r
            for h in range(heads):
                qj = qbuf[h, rows(r0, blk), :].astype(BF16)
                kj = kbuf[h, rows(r0, 2 * blk), :].astype(BF16)
                vj = vbuf[h, rows(r0, 2 * blk), :].astype(BF16)
                s = _dot_nt(qj, kj)
                s = jnp.where(band_first if j == 0 else band, s, NEG_BIG)
                m = jnp.max(s, axis=-1, keepdims=True)
                p = jnp.exp(s - m)
                l = jnp.sum(p, axis=-1, keepdims=True)
                obuf[h, rows(r0, blk), :] = _dot(p.astype(BF16), vj) / l
                lbuf[h, rows(r0, blk), :] = jnp.broadcast_to(m + jnp.log(l), (blk, LANES))

    for h in range(heads):
        ls = slice(h * LANES, (h + 1) * LANES)
        if n_earlier == 0:
            o_ref, lse_ref = out_refs
            o_ref[:, ls] = obuf[h].astype(o_ref.dtype)
            lse_ref[:, ls] = lbuf[h]
        else:
            (o_ref,) = out_refs
            outs = [earlier[2 * g][:, ls].astype(F32) for g in range(n_earlier)] + [obuf[h]]
            lses = [earlier[2 * g + 1][:, ls] for g in range(n_earlier)] + [lbuf[h]]
            m = functools.reduce(jnp.maximum, lses)
            es = [jnp.exp(l - m) for l in lses]
            num = functools.reduce(lambda a, b: a + b, [e * o for e, o in zip(es, outs)])
            den = functools.reduce(lambda a, b: a + b, es)
            o_ref[:, ls] = (num / den).astype(o_ref.dtype)
    kbuf[:, 0:prev, :] = kbuf[:, bt:bt + prev, :]
    vbuf[:, 0:prev, :] = vbuf[:, bt:bt + prev, :]


def _attn_group(proj, cos_t, sin_t, gi, dilation, span, t, earlier=()):
    prev = dilation * ATT_BLOCK
    bt = max(min(512, t), prev)
    width = 2 * LANES
    gw = ATT_GROUP_WIDTH
    per_g = gw // width
    q0 = gi * gw // width
    k0 = q0 + ATT_WIDTH // width
    v0 = k0 + ATT_WIDTH // width

    def cur(c0):
        return pl.BlockSpec((bt, width), lambda hg, i, c0=c0: (i, c0 + hg))

    tab = pl.BlockSpec((bt, LANES), lambda hg, i: (i, 0))
    out = pl.BlockSpec((bt, width), lambda hg, i: (i, hg))
    flat_earlier = [a for pair in earlier for a in pair]
    if earlier:
        out_specs, out_shape = [out], [jax.ShapeDtypeStruct((t, gw), BF16)]
    else:
        out_specs = [out, out]
        out_shape = [jax.ShapeDtypeStruct((t, gw), BF16), jax.ShapeDtypeStruct((t, gw), F32)]
    return pl.pallas_call(
        functools.partial(_attn_kernel, dilation=dilation, span=span, n_earlier=len(earlier)),
        grid=(per_g, t // bt),
        in_specs=[cur(q0), cur(k0), cur(v0), tab, tab] + [out] * len(flat_earlier),
        out_specs=out_specs,
        out_shape=out_shape,
        scratch_shapes=[pltpu.VMEM((width // LANES, prev + bt, LANES), F32),
                        pltpu.VMEM((width // LANES, prev + bt, LANES), F32),
                        pltpu.VMEM((width // LANES, bt, LANES), F32),
                        pltpu.VMEM((width // LANES, bt, LANES), F32),
                        pltpu.VMEM((width // LANES, bt, LANES), F32)],
        compiler_params=_cparams(("parallel", "arbitrary")),
        name=f"attn_g{gi}",
    )(proj, proj, proj, cos_t, sin_t, *flat_earlier)


def _attention(proj, cos_t, sin_t, t):
    earlier = []
    for gi, (window, dilation) in enumerate(ATT_GROUPS[:-1]):
        earlier.append(_attn_group(proj, cos_t, sin_t, gi, dilation, window // dilation, t))
    window, dilation = ATT_GROUPS[-1]
    (o_att,) = _attn_group(proj, cos_t, sin_t, len(ATT_GROUPS) - 1, dilation, window // dilation, t,
                           earlier=tuple(earlier))
    return o_att


def _mix_kernel(odn_ref, oatt_ref, wdn_ref, watt_ref, gdn_ref, gatt_ref, out_ref):
    y_dn = _dot(odn_ref[...], wdn_ref[...])
    y_att = _dot(oatt_ref[...], watt_ref[...])
    merged = (_sigmoid(gdn_ref[...].astype(F32)) * y_dn
              + _sigmoid(gatt_ref[...].astype(F32)) * y_att)
    out_ref[...] = merged.astype(out_ref.dtype)


def _mix(o_dn, o_att, w_dn, w_att, proj, t, d):
    tm = min(1024, t)
    tn = ATT_GROUP_WIDTH
    g0 = 3 * ATT_WIDTH // tn
    row_dn = pl.BlockSpec((tm, DN_WIDTH), lambda j, i: (i, 0))
    row_g = pl.BlockSpec((tm, ATT_GROUP_WIDTH), lambda j, i: (i, 0))
    return pl.pallas_call(
        _mix_kernel,
        grid=(d // tn, t // tm),
        in_specs=[row_dn, row_g,
                  pl.BlockSpec((DN_WIDTH, tn), lambda j, i: (0, j)),
                  pl.BlockSpec((ATT_GROUP_WIDTH, tn), lambda j, i: (0, j)),
                  pl.BlockSpec((tm, tn), lambda j, i: (i, g0 + j)),
                  pl.BlockSpec((tm, tn), lambda j, i: (i, g0 + d // tn + j))],
        out_specs=pl.BlockSpec((tm, tn), lambda j, i: (i, j)),
        out_shape=jax.ShapeDtypeStruct((t, d), BF16),
        compiler_params=_cparams(("parallel", "parallel")),
        name="mix",
    )(o_dn, o_att, w_dn, w_att, proj, proj)


def _layer_norm(y, g, b):
    mu = jnp.mean(y, axis=-1, keepdims=True)
    yc = y - mu
    var = jnp.mean(yc * yc, axis=-1, keepdims=True)
    return yc * lax.rsqrt(var + LN_EPS) * g + b


def _wo_kernel(m_ref, w_ref, x_ref, y_ref, *, alpha):
    y_ref[...] = alpha * x_ref[...] + _dot(m_ref[...], w_ref[...])


def _wo_residual(merged, w_o, x2, t, d, alpha):
    tm, tn = min(1024, t), min(1024, d)
    tile = pl.BlockSpec((tm, tn), lambda i, j: (i, j))
    return pl.pallas_call(
        functools.partial(_wo_kernel, alpha=alpha),
        grid=(t // tm, d // tn),
        in_specs=[pl.BlockSpec((tm, d), lambda i, j: (i, 0)),
                  pl.BlockSpec((d, tn), lambda i, j: (0, j)),
                  tile],
        out_specs=tile,
        out_shape=jax.ShapeDtypeStruct((t, d), F32),
        compiler_params=_cparams(("parallel", "parallel")),
        name="wo",
    )(merged, w_o, x2)


def _router_kernel(y_ref, g_ref, b_ref, w_ref, bias_ref,
                   x1_ref, x1p_ref, topi_ref, topw_ref, rank_ref, cnt_ref, run_ref):
    i = pl.program_id(0)
    tm, d = y_ref.shape

    @pl.when(i == 0)
    def _():
        run_ref[...] = jnp.zeros_like(run_ref)

    x1 = _layer_norm(y_ref[...], g_ref[...], b_ref[...])
    x1_ref[...] = x1
    x1p_ref[...] = _pack_halves(x1[:, :d // 2], x1[:, d // 2:])

    x_hi, x_mid = _split2(x1)
    w_hi, w_mid = _split2(w_ref[...])
    logits = _dot(x_hi, w_hi) + _dot(x_hi, w_mid) + _dot(x_mid, w_hi)
    s = _sigmoid(logits)
    sc = s + bias_ref[...]
    lane = lax.broadcasted_iota(jnp.int32, (tm, N_EXPERTS), 1)
    grp = lane // GROUP_SIZE
    neg = -jnp.inf

    def first_argmax(v):
        m = jnp.max(v, axis=-1, keepdims=True)
        idx = jnp.min(jnp.where(v == m, lane, N_EXPERTS), axis=-1, keepdims=True)
        return m, idx

    gscore = []
    for gi in range(N_GROUPS):
        vg = jnp.where(grp == gi, sc, neg)
        m1, i1 = first_argmax(vg)
        m2 = jnp.max(jnp.where(lane == i1, neg, vg), axis=-1, keepdims=True)
        gscore.append(m1 + m2)
    emask = jnp.zeros((tm, N_EXPERTS), jnp.bool_)
    for gi in range(N_GROUPS):
        ahead = jnp.zeros((tm, 1), jnp.int32)
        for gj in range(N_GROUPS):
            if gj == gi:
                continue
            beats = (gscore[gj] > gscore[gi]) | ((gscore[gj] == gscore[gi]) & (gj < gi))
            ahead = ahead + beats.astype(jnp.int32)
        emask = emask | ((grp == gi) & (ahead < TOPK_GROUPS))
    masked = jnp.where(emask, sc, neg)

    sel =jnp.zeros((tm, N_EXPERTS), jnp.bool_)
    idxs, vals = [], []
    for _ in range(TOP_K):
        _, ik = first_argmax(masked)
        hit = lane == ik
        sel = sel | hit
        masked = jnp.where(hit, neg, masked)
        idxs.append(ik)
        vals.append(jnp.sum(jnp.where(hit, s, 0.0), axis=-1, keepdims=True))
    wsum = vals[0]
    for v in vals[1:]:
        wsum = wsum + v

    sel_b = jnp.where(sel, 1.0, 0.0).astype(BF16)
    strict = jnp.where(lax.broadcasted_iota(jnp.int32, (tm, tm), 1)
                       < lax.broadcasted_iota(jnp.int32, (tm, tm), 0), 1.0, 0.0).astype(BF16)
    rank_excl = run_ref[0:1, :] + _dot(strict, sel_b)
    run_new = run_ref[0:1, :] + jnp.sum(sel_b.astype(F32), axis=0, keepdims=True)
    run_ref[...] = jnp.broadcast_to(run_new, run_ref.shape)
    cnt_ref[...] = jnp.broadcast_to(run_new, cnt_ref.shape).astype(jnp.int32)

    topi = jnp.zeros((tm, N_EXPERTS), jnp.int32)
    topw = jnp.zeros((tm, N_EXPERTS), F32)
    rnk = jnp.zeros((tm, N_EXPERTS), F32)
    for kk in range(TOP_K):
        hit = lane == idxs[kk]
        rk = jnp.sum(jnp.where(hit, rank_excl, 0.0), axis=-1, keepdims=True)
        topi = jnp.where(lane == kk, idxs[kk], topi)
        topw = jnp.where(lane == kk, vals[kk] / wsum * ROUTED_SCALE, topw)
        rnk = jnp.where(lane == kk, rk, rnk)
    topi_ref[...] = topi
    topw_ref[...] = topw
    rank_ref[...] = rnk.astype(jnp.int32)


def _ln_router(y, g, b, w_router, bias, t, d):
    tm = min(256, t)
    row = pl.BlockSpec((tm, N_EXPERTS), lambda i: (i, 0))
    full = pl.BlockSpec((tm, d), lambda i: (i, 0))
    vec = pl.BlockSpec((1, d), lambda i: (0, 0))
    return pl.pallas_call(
        _router_kernel,
        grid=(t // tm,),
        in_specs=[full, vec, vec,
                  pl.BlockSpec((d, N_EXPERTS), lambda i: (0, 0)),
                  pl.BlockSpec((1, N_EXPERTS), lambda i: (0, 0))],
        out_specs=[full, pl.BlockSpec((tm, d // 2), lambda i: (i, 0)), row, row, row,
                   pl.BlockSpec((SUBLANES, N_EXPERTS), lambda i: (0, 0))],
        out_shape=[jax.ShapeDtypeStruct((t, d), F32),
                   jax.ShapeDtypeStruct((t, d // 2), jnp.uint32),
                   jax.ShapeDtypeStruct((t, N_EXPERTS), jnp.int32),
                   jax.ShapeDtypeStruct((t, N_EXPERTS), F32),
                   jax.ShapeDtypeStruct((t, N_EXPERTS), jnp.int32),
                   jax.ShapeDtypeStruct((SUBLANES, N_EXPERTS), jnp.int32)],
        scratch_shapes=[pltpu.VMEM((SUBLANES, N_EXPERTS), F32)],
        compiler_params=_cparams(("arbitrary",)),
        name="ln_router",
    )(y, g.reshape(1, d).astype(F32), b.reshape(1, d).astype(F32), w_router.astype(F32),
      bias.reshape(1, N_EXPERTS).astype(F32))


def _slot_kernel(topi_ref, rank_ref, start_ref, pos_ref):
    topi = topi_ref[...]
    lane = lax.broadcasted_iota(jnp.int32, topi.shape, 1)
    start = jnp.broadcast_to(start_ref[...], topi.shape)
    pos = rank_ref[...]
    for kk in range(TOP_K):
        seg = jnp.sum(jnp.where(lane == topi[:, kk:kk + 1], start, 0.0), axis=-1, keepdims=True)
        pos = jnp.where(lane == kk, pos + seg.astype(jnp.int32), pos)
    pos_ref[...] = pos


def _slots(topi, rank, pad_start, t):
    tm = min(1024, t)
    row = pl.BlockSpec((tm, N_EXPERTS), lambda i: (i, 0))
    return pl.pallas_call(
        _slot_kernel,
        grid=(t // tm,),
        in_specs=[row, row, pl.BlockSpec((1, N_EXPERTS), lambda i: (0, 0))],
        out_specs=row,
        out_shape=jax.ShapeDtypeStruct((t, N_EXPERTS), jnp.int32),
        compiler_params=_cparams(("parallel",)),
        name="slots",
    )(topi, rank, pad_start.astype(F32).reshape(1, N_EXPERTS))


def _row_copy(src_ref, src_row, dst_ref, dst_row, sem):
    return pltpu.make_async_copy(src_ref.at[pl.ds(src_row, 1)], dst_ref.at[pl.ds(dst_row, 1)], sem)


def _dispatch_kernel(pos_ref, x_ref, xs_ref, sem):
    i = pl.program_id(0)
    tb = x_ref.shape[0]
    base = i * (tb * TOP_K)

    def start(r, carry):
        for kk in range(TOP_K):
            _row_copy(x_ref, r, xs_ref, pos_ref[base + r * TOP_K + kk], sem).start()
        return carry

    lax.fori_loop(0, tb, start, 0)
    for kk in range(TOP_K):
        pltpu.make_async_copy(x_ref, xs_ref.at[pl.ds(0, tb)], sem).wait()


def _dispatch(pos_flat, x1, n_slots, t, d):
    tb = min(256, t)
    return pl.pallas_call(
        _dispatch_kernel,
        grid_spec=pltpu.PrefetchScalarGridSpec(
            num_scalar_prefetch=1,
            grid=(t // tb,),
            in_specs=[pl.BlockSpec((tb, d), lambda i, pos: (i, 0))],
            out_specs=pl.BlockSpec(memory_space=pl.ANY),
            scratch_shapes=[pltpu.SemaphoreType.DMA],
        ),
        out_shape=jax.ShapeDtypeStruct((n_slots, d), x1.dtype),
        compiler_params=_cparams(("arbitrary",)),
        name="dispatch",
    )(pos_flat, x1)


def _expert_kernel(be_ref, nv_ref, first_ref, nxt_ref, ord_ref, nu_ref, xs_ref, wgu_hbm, wdn_hbm,
                   y_ref, gu_stage, dn_stage, gu_cache, dn_cache, act_ref, sem):
    b = pl.program_id(0)
    n_used = nu_ref[0]

    def weight_copies(e):
        return (pltpu.make_async_copy(wgu_hbm.at[e], gu_stage, sem.at[0]),
                pltpu.make_async_copy(wdn_hbm.at[e], dn_stage, sem.at[1]))

    @pl.when(b == 0)
    def _():
        act_ref[...] = jnp.zeros_like(act_ref)

    @pl.when(b <= n_used)
    def _():
        rows, hd = xs_ref.shape
        kc = min(EXPERT_CHUNK, hd)
        cur = jnp.minimum(b, n_used - 1)
        e = be_ref[cur]

        @pl.when((first_ref[cur] == 1) & (b < n_used))
        def _():
            @pl.when(b == 0)
            def _():
                for cp in weight_copies(e):
                    cp.start()

            for cp in weight_copies(e):
                cp.wait()
            for c0 in range(0, 2 * hd, kc):
                for j in range(EXPERT_FF // LANES):
                    gu_cache[c0:c0 + kc, 2 * j * LANES:(2 * j + 1) * LANES] = (
                        gu_stage[c0:c0 + kc, j * LANES:(j + 1) * LANES].astype(BF16))
                    gu_cache[c0:c0 + kc, (2 * j + 1) * LANES:(2 * j + 2) * LANES] = (
                        gu_stage[c0:c0 + kc, EXPERT_FF + j * LANES:EXPERT_FF + (j + 1) * LANES]
                        .astype(BF16))
                dn_cache[ord_ref[cur] % 2, :, c0:c0 + kc] = dn_stage[:, c0:c0 + kc].astype(BF16)

            @pl.when(nxt_ref[cur] >= 0)
            def _():
                for cp in weight_copies(nxt_ref[cur]):
                    cp.start()

        prev_slot = ord_ref[jnp.maximum(b - 1, 0)] % 2
        act_prev = act_ref[...]
        nc = min(EXPERT_OUT_CHUNK, hd)
        for c0 in range(0, hd, nc):
            y_ref[:, c0:c0 + nc] = _pack_halves(
                _dot(act_prev, dn_cache[prev_slot, :, c0:c0 + nc]),
                _dot(act_prev, dn_cache[prev_slot, :, hd + c0:hd + c0 + nc]))

        rid = lax.broadcasted_iota(jnp.int32, (rows, 1), 0)
        valid = rid < nv_ref[cur]
        lo, hi = _unpack_halves(xs_ref[...])
        x_lo = jnp.where(valid, lo, 0.0).astype(BF16)
        x_hi = jnp.where(valid, hi, 0.0).astype(BF16)
        for j in range(EXPERT_FF // LANES):
            cs = slice(2 * j * LANES, (2 * j + 2) * LANES)
            hgu = _dot(x_lo, gu_cache[:hd, cs]) + _dot(x_hi, gu_cache[hd:, cs])
            gate, up = hgu[:, :LANES], hgu[:, LANES:]
            act_ref[:, j * LANES:(j + 1) * LANES] = (gate * _sigmoid(gate) * up).astype(BF16)


def _experts(blk_expert, blk_valid, blk_first, blk_next, blk_ord, n_used, xs, w_gu, w_dn, d):
    n_slots = xs.shape[0]
    nb = n_slots // MOE_ROWS

    def in_map(b, be, nv, fi, nx, od, nu):
        return (jnp.minimum(b, nu[0] - 1), 0)

    def out_map(b, be, nv, fi, nx, od, nu):
        return (jnp.minimum(jnp.maximum(b - 1, 0), nu[0] - 1), 0)

    return pl.pallas_call(
        _expert_kernel,
        grid_spec=pltpu.PrefetchScalarGridSpec(
            num_scalar_prefetch=6,
            grid=(nb + 1,),
            in_specs=[pl.BlockSpec((MOE_ROWS, d // 2), in_map),
                      pl.BlockSpec(memory_space=pl.ANY),
                      pl.BlockSpec(memory_space=pl.ANY)],
            out_specs=pl.BlockSpec((MOE_ROWS, d // 2), out_map),
            scratch_shapes=[pltpu.VMEM((d, 2 * EXPERT_FF), w_gu.dtype),
                            pltpu.VMEM((EXPERT_FF, d), w_dn.dtype),
                            pltpu.VMEM((d, 2 * EXPERT_FF), BF16),
                            pltpu.VMEM((2, EXPERT_FF, d), BF16),
                            pltpu.VMEM((MOE_ROWS, EXPERT_FF), BF16),
                            pltpu.SemaphoreType.DMA((2,))],
        ),
        out_shape=jax.ShapeDtypeStruct((n_slots, d // 2), jnp.uint32),
        compiler_params=_cparams(("arbitrary",)),
        name="experts",
    )(blk_expert, blk_valid, blk_first, blk_next, blk_ord, n_used, xs, w_gu, w_dn)


def _shared_kernel(x_ref, wgu_ref, wdn_ref, y_ref):
    hd = x_ref.shape[1]
    lo, hi = _unpack_halves(x_ref[...])
    hgu = _dot(lo.astype(BF16), wgu_ref[:hd, :]) + _dot(hi.astype(BF16), wgu_ref[hd:, :])
    gate, up = hgu[:, :EXPERT_FF], hgu[:, EXPERT_FF:]
    act = (gate * _sigmoid(gate) * up).astype(BF16)
    y_ref[...] = _dot(act, wdn_ref[...]).astype(y_ref.dtype)


def _shared(x1p, w_gu, w_dn, t, d):
    tm = min(512, t)
    return pl.pallas_call(
        _shared_kernel,
        grid=(t // tm,),
        in_specs=[pl.BlockSpec((tm, d // 2), lambda i: (i, 0)),
                  pl.BlockSpec((d, 2 * EXPERT_FF), lambda i: (0, 0)),
                  pl.BlockSpec((EXPERT_FF, d), lambda i: (0, 0))],
        out_specs=pl.BlockSpec((tm, d), lambda i: (i, 0)),
        out_shape=jax.ShapeDtypeStruct((t, d), BF16),
        compiler_params=_cparams(("parallel",)),
        name="shared",
    )(x1p, w_gu, w_dn)


def _combine_kernel(pos_ref, x_ref, ysh_ref, topw_ref, g_ref, b_ref, y_hbm, out_ref, buf_ref, sem,
                    *, alpha):
    i = pl.program_id(0)
    n = pl.num_programs(0)
    tb = x_ref.shape[0]

    def gather(tile, slot):
        base = tile * (tb * TOP_K)

        def body(r, carry):
            for kk in range(TOP_K):
                _row_copy(y_hbm, pos_ref[base + r * TOP_K + kk], buf_ref.at[slot, kk], r,
                          sem.at[slot]).start()
            return carry

        lax.fori_loop(0, tb, body, 0)

    @pl.when(i == 0)
    def _():
        gather(0, 0)

    @pl.when(i + 1 < n)
    def _():
        gather(i + 1, (i + 1) % 2)

    slot = i % 2
    for kk in range(TOP_K):
        pltpu.make_async_copy(y_hbm.at[pl.ds(0, tb)], buf_ref.at[slot, kk], sem.at[slot]).wait()
    topw = topw_ref[...]
    hd = x_ref.shape[1] // 2
    acc_lo = alpha * x_ref[:, :hd] + ysh_ref[:, :hd].astype(F32)
    acc_hi = alpha * x_ref[:, hd:] + ysh_ref[:, hd:].astype(F32)
    for kk in range(TOP_K):
        lo, hi = _unpack_halves(buf_ref[slot, kk])
        acc_lo = acc_lo + topw[:, kk:kk + 1] * lo
        acc_hi = acc_hi + topw[:, kk:kk + 1] * hi
    inv_d = 1.0 / (2 * hd)
    mu = (jnp.sum(acc_lo, axis=-1, keepdims=True) + jnp.sum(acc_hi, axis=-1, keepdims=True)) * inv_d
    c_lo, c_hi = acc_lo - mu, acc_hi - mu
    var = (jnp.sum(c_lo * c_lo, axis=-1, keepdims=True)
           + jnp.sum(c_hi * c_hi, axis=-1, keepdims=True)) * inv_d
    rstd = lax.rsqrt(var + LN_EPS)
    out_ref[:, :hd] = c_lo * rstd * g_ref[:, :hd] + b_ref[:, :hd]
    out_ref[:, hd:] = c_hi * rstd * g_ref[:, hd:] + b_ref[:, hd:]


def _combine(pos_flat, x1, ysh, topw, g, b, y, t, d, alpha):
    tb = min(128, t)
    row = pl.BlockSpec((tb, d), lambda i, pos: (i, 0))
    vec = pl.BlockSpec((1, d), lambda i, pos: (0, 0))
    return pl.pallas_call(
        functools.partial(_combine_kernel, alpha=alpha),
        grid_spec=pltpu.PrefetchScalarGridSpec(
            num_scalar_prefetch=1,
            grid=(t // tb,),
            in_specs=[row, row, pl.BlockSpec((tb, N_EXPERTS), lambda i, pos: (i, 0)), vec, vec,
                      pl.BlockSpec(memory_space=pl.ANY)],
            out_specs=row,
            scratch_shapes=[pltpu.VMEM((2, TOP_K, tb, d // 2), jnp.uint32),
                            pltpu.SemaphoreType.DMA((2,))],
        ),
        out_shape=jax.ShapeDtypeStruct((t, d), F32),
        compiler_params=_cparams(("arbitrary",)),
        name="combine",
    )(pos_flat, x1, ysh, topw, g.reshape(1, d).astype(F32), b.reshape(1, d).astype(F32), y)


def _rope_tables(t):
    half = ATT_HEAD_DIM // 2
    inv_freq = ROPE_THETA ** (-jnp.arange(half, dtype=F32) / half)
    ang = jnp.arange(t, dtype=F32)[:, None] * inv_freq[None, :]
    cos, sin = jnp.cos(ang), jnp.sin(ang)
    return jnp.concatenate([cos, cos], -1), jnp.concatenate([-sin, sin], -1)


def _token_mixer(x2, w_in, conv_w, a_log, dt_bias, dn_norm_w, w_dn_branch, w_att_branch, w_o,
                 alpha):
    t, d = x2.shape
    n_main = 4 * DN_WIDTH
    n_ba = 2 * DN_HEADS
    x_b = x2.astype(BF16)
    w_t = w_in.T
    proj_dn = _proj(x_b, w_t, 0, n_main, "proj_dn")
    proj_att = _proj(x_b, w_t, n_main + n_ba, w_t.shape[0] - n_main - n_ba, "proj_att")
    betab, gcb, gct = _dn_gates(x2, w_t, n_main, a_log, dt_bias)
    u, w, qe, kd, intra, egl = _dn_prep(proj_dn, conv_w.astype(F32), betab, gcb, gct, t)
    o_dn = _dn_scan(u, w, qe, kd, intra, egl, proj_dn, dn_norm_w, t)

    cos_t, sin_t = _rope_tables(t)
    o_att = _attention(proj_att, cos_t, sin_t, t)
    merged = _mix(o_dn, o_att, w_dn_branch.astype(BF16), w_att_branch.astype(BF16), proj_att, t, d)
    return _wo_residual(merged, w_o.astype(BF16), x2, t, d, alpha)


def _moe(y_mix, ln1_g, ln1_b, w_router, router_bias, w_exp_gate_up, w_exp_down, w_sh_gate_up,
         w_sh_down, ln_g, ln_b, alpha):
    t, d = y_mix.shape
    r = MOE_ROWS
    x1, x1p, topi, topw, rank, cnt = _ln_router(y_mix, ln1_g, ln1_b, w_router, router_bias, t, d)
    counts = cnt[0]
    padded = (counts + r - 1) // r * r
    pad_end = jnp.cumsum(padded)
    pad_start = pad_end - padded
    nb = (t * TOP_K) // r + N_EXPERTS
    blk_row0 = jnp.arange(nb, dtype=jnp.int32) * r
    in_blk = (blk_row0[:, None] >= pad_start[None, :]) & (blk_row0[:, None] < pad_end[None, :])
    expert_ids = jnp.arange(N_EXPERTS, dtype=jnp.int32)
    blk_expert = jnp.where(blk_row0 < pad_end[-1],
                           jnp.sum(jnp.where(in_blk, expert_ids[None, :], 0), axis=1),
                           N_EXPERTS - 1).astype(jnp.int32)
    blk_valid = jnp.sum(jnp.where(in_blk, jnp.clip(counts[None, :] - (blk_row0[:, None] - pad_start[None, :]), 0, r), 0),
                        axis=1).astype(jnp.int32)
    n_used = (pad_end[-1:] // r).astype(jnp.int32)
    blk_ids = jnp.arange(nb, dtype=jnp.int32)
    prev_expert = jnp.concatenate([jnp.full((1,), -1, jnp.int32), blk_expert[:-1]])
    blk_first = (blk_expert != prev_expert).astype(jnp.int32)
    blk_ord = jnp.cumsum(blk_first) - 1
    after = jnp.sum(jnp.where(in_blk, pad_end[None, :] // r, 0), axis=1)
    after_expert = jnp.sum(jnp.where(after[:, None] == blk_ids[None, :], blk_expert[None, :], 0), axis=1)
    blk_next = jnp.where(after < n_used[0], after_expert, -1).astype(jnp.int32)
    pos_flat = _slots(topi, rank, pad_start, t)[:, :TOP_K].reshape(-1)

    xs = _dispatch(pos_flat, x1p, nb * r, t, d // 2)
    y = _experts(blk_expert, blk_valid, blk_first, blk_next, blk_ord.astype(jnp.int32), n_used, xs,
                 w_exp_gate_up, w_exp_down, d)
    ysh = _shared(x1p, w_sh_gate_up.astype(BF16), w_sh_down.astype(BF16), t, d)
    return _combine(pos_flat, x1, ysh, topw, ln_g, ln_b, y, t, d, alpha)


def kernel(x, w_in, conv_w, a_log, dt_bias, dn_norm_w, w_dn_branch, w_att_branch, w_o, ln1_g, ln1_b,
           w_router, router_bias, w_exp_gate_up, w_exp_down, w_sh_gate_up, w_sh_down, ln2_g, ln2_b):
    bsz, t, d = x.shape
    depth = w_in.shape[0]
    alpha = (2.0 * depth) ** 0.25
    outs = []
    for bi in range(bsz):
        xb = x[bi]
        for l in range(depth):
            y_mix = _token_mixer(xb, w_in[l], conv_w[l], a_log[l], dt_bias[l], dn_norm_w[l],
                                 w_dn_branch[l], w_att_branch[l], w_o[l], alpha)
            xb = _moe(y_mix, ln1_g[l], ln1_b[l], w_router[l], router_bias[l], w_exp_gate_up[l],
                      w_exp_down[l], w_sh_gate_up[l], w_sh_down[l], ln2_g[l], ln2_b[l], alpha)
        outs.append(xb)
    return jnp.stack(outs, 0)
```

```python
import functools

import jax
import jax.numpy as jnp
from jax import lax
from jax.experimental import pallas as pl
from jax.experimental.pallas import tpu as pltpu

F32 = jnp.float32
BF16 = jnp.bfloat16

DN_HEADS = 16
DN_HEAD_DIM = 128
DN_WIDTH = DN_HEADS * DN_HEAD_DIM
DN_CONV = 4
DN_CHUNK = 64
ATT_GROUPS = ((128, 1), (512, 4), (2048, 16))
ATT_HEADS_PER_GROUP = 8
ATT_HEAD_DIM = 128
ATT_GROUP_WIDTH = ATT_HEADS_PER_GROUP * ATT_HEAD_DIM
ATT_WIDTH = len(ATT_GROUPS) * ATT_GROUP_WIDTH
ATT_BLOCK = 128
ROPE_THETA = 10000.0
N_EXPERTS = 128
TOP_K = 8
N_GROUPS = 8
GROUP_SIZE = N_EXPERTS // N_GROUPS
TOPK_GROUPS = 4
EXPERT_FF = 384
ROUTED_SCALE = 2.5
LN_EPS = 1e-5
RMS_EPS = 1e-6
L2_EPS = 1e-6

LANES = 128
SUBLANES = 8
BF16_SUBLANES = 16
VMEM_LIMIT = 56 * 1024 * 1024

DN_PREP_ROWS = 1024
MOE_ROWS = 256
EXPERT_CHUNK = 1024
EXPERT_OUT_CHUNK = 512
NEG_BIG = -1e30


def _cparams(sem, vmem=VMEM_LIMIT):
    return pltpu.CompilerParams(dimension_semantics=sem, vmem_limit_bytes=vmem)


def _split3(a):
    hi = a.astype(BF16)
    r1 = a - hi.astype(F32)
    mid = r1.astype(BF16)
    lo = (r1 - mid.astype(F32)).astype(BF16)
    return hi, mid, lo


def _split2(a):
    hi = a.astype(BF16)
    return hi, (a - hi.astype(F32)).astype(BF16)


def _dot(a, b):
    return jnp.dot(a, b, preferred_element_type=F32)


def _dot_nt(a, b):
    return lax.dot_general(a, b, (((1,), (1,)), ((), ())), preferred_element_type=F32)


def _dot_tn(a, b):
    return lax.dot_general(a, b, (((0,), (0,)), ((), ())), preferred_element_type=F32)


def _sigmoid(x):
    return 1.0 / (1.0 + jnp.exp(-x))


def _pack_halves(lo, hi):
    lo_b = lax.bitcast_convert_type(lo.astype(BF16).astype(F32), jnp.uint32)
    hi_b = lax.bitcast_convert_type(hi.astype(BF16).astype(F32), jnp.uint32)
    return (hi_b & jnp.uint32(0xFFFF0000)) | (lo_b >> 16)


def _unpack_halves(u):
    lo = lax.bitcast_convert_type(u << 16, F32)
    hi = lax.bitcast_convert_type(u & jnp.uint32(0xFFFF0000), F32)
    return lo, hi


def _proj_kernel(x_ref, wt_ref, o_ref):
    o_ref[...] = _dot_nt(x_ref[...], wt_ref[...].astype(BF16)).astype(o_ref.dtype)


def _proj(x_b, w_t, row0, n, name):
    m, k = x_b.shape
    tm, tn = min(1024, m), min(512, n)
    return pl.pallas_call(
        _proj_kernel,
        grid=(m // tm, n // tn),
        in_specs=[pl.BlockSpec((tm, k), lambda i, j: (i, 0)),
                  pl.BlockSpec((pl.Element(tn), pl.Element(k)),
                               lambda i, j: (pl.multiple_of(row0 + j * tn, SUBLANES), 0))],
        out_specs=pl.BlockSpec((tm, tn), lambda i, j: (i, j)),
        out_shape=jax.ShapeDtypeStruct((m, n), BF16),
        compiler_params=_cparams(("parallel", "parallel")),
        name=name,
    )(x_b, w_t)


def _gates_kernel(x_ref, w_ref, prm_ref, betab_ref, gcb_ref, gct_ref, xb_ref):
    tm = x_ref.shape[0]
    x_hi, x_mid = _split2(x_ref[...])
    xb_ref[...] = x_hi
    w_hi, w_mid = _split2(w_ref[...])
    logits = _dot_nt(x_hi, w_hi) + _dot_nt(x_hi, w_mid) + _dot_nt(x_mid, w_hi)

    def softplus(v):
        return jnp.maximum(v, 0.0) + jnp.log(1.0 + jnp.exp(-jnp.abs(v)))

    prm = prm_ref[...]
    beta = _sigmoid(logits)
    g = prm[0:1, :] * softplus(logits + prm[1:2, :])

    ri = lax.broadcasted_iota(jnp.int32, (tm, tm), 0)
    ci = lax.broadcasted_iota(jnp.int32, (tm, tm), 1)
    same = (ri // DN_CHUNK) == (ci // DN_CHUNK)
    lower = jnp.where(same & (ci <= ri), 1.0, 0.0).astype(BF16)
    gc = jnp.zeros((tm, LANES), F32)
    for part in _split3(g):
        gc = gc + _dot(lower, part)
    gct_ref[...] = gc.T[DN_HEADS:2 * DN_HEADS, :]
    for h in range(DN_HEADS):
        sl = slice(h * LANES, (h + 1) * LANES)
        betab_ref[:, sl] = jnp.broadcast_to(beta[:, h:h + 1], (tm, LANES))
        gcb_ref[:, sl] = jnp.broadcast_to(gc[:, DN_HEADS + h:DN_HEADS + h + 1], (tm, LANES))


def _dn_gates(x2, w_t, col0, a_log, dt_bias):
    t, d = x2.shape
    tm = min(512, t)
    neg_a = -jnp.exp(a_log.astype(F32))
    prm = jnp.zeros((SUBLANES, LANES), F32)
    prm = prm.at[0, DN_HEADS:2 * DN_HEADS].set(neg_a).at[1, DN_HEADS:2 * DN_HEADS].set(dt_bias.astype(F32))
    return pl.pallas_call(
        _gates_kernel,
        grid=(t // tm,),
        in_specs=[pl.BlockSpec((tm, d), lambda i: (i, 0)),
                  pl.BlockSpec((LANES, d), lambda i: (col0 // LANES, 0)),
                  pl.BlockSpec((SUBLANES, LANES), lambda i: (0, 0))],
        out_specs=[pl.BlockSpec((tm, DN_WIDTH), lambda i: (i, 0)),
                   pl.BlockSpec((tm, DN_WIDTH), lambda i: (i, 0)),
                   pl.BlockSpec((DN_HEADS, tm), lambda i: (0, i)),
                   pl.BlockSpec((tm, d), lambda i: (i, 0))],
        out_shape=[jax.ShapeDtypeStruct((t, DN_WIDTH), F32),
                   jax.ShapeDtypeStruct((t, DN_WIDTH), F32),
                   jax.ShapeDtypeStruct((DN_HEADS, t), F32),
                   jax.ShapeDtypeStruct((t, d), BF16)],
        compiler_params=_cparams(("parallel",)),
        name="dn_gates",
    )(x2, w_t.astype(F32), prm)


def _dn_prep_kernel(q_ref, k_ref, v_ref, hq_ref, hk_ref, hv_ref, cq_ref, ck_ref, cv_ref,
                    betab_ref, gcb_ref, gct_ref,
                    u_ref, w_ref, qe_ref, kd_ref, intra_ref, egl_ref, cbuf):
    i = pl.program_id(1)
    rows = q_ref.shape[0]
    c = DN_CHUNK
    halo_rows = hq_ref.shape[0]

    def conv_silu(slot, x_ref, halo_ref, cw_ref):
        cbuf[slot, 0:halo_rows, :] = jnp.where(i > 0, halo_ref[...].astype(F32), 0.0)
        cbuf[slot, halo_rows:halo_rows + rows, :] = x_ref[...].astype(F32)
        cw = cw_ref[...]
        y = jnp.zeros((rows, LANES), F32)
        for j in range(DN_CONV):
            off = halo_rows - (DN_CONV - 1) + j
            y = y + cw[j:j + 1, :] * cbuf[slot, off:off + rows, :]
        return y * _sigmoid(y)

    def l2n(a):
        return a * lax.rsqrt(jnp.sum(a * a, axis=-1, keepdims=True) + L2_EPS)

    q = l2n(conv_silu(0, q_ref, hq_ref, cq_ref)) * (DN_HEAD_DIM ** -0.5)
    k = l2n(conv_silu(1, k_ref, hk_ref, ck_ref))
    v = conv_silu(2, v_ref, hv_ref, cv_ref)
    beta = betab_ref[...]
    gcol = gcb_ref[...]
    grow_all = gct_ref[...]

    ii = lax.broadcasted_iota(jnp.int32, (c, c), 0)
    jj = lax.broadcasted_iota(jnp.int32, (c, c), 1)
    eye = jnp.where(ii == jj, 1.0, 0.0)
    n_chunks = rows // c

    decays, a_mats = [], []
    kbs, egcs = [], []
    for n in range(n_chunks):
        rs = slice(n * c, (n + 1) * c)
        kc = k[rs]
        kb = kc * beta[rs]
        grow = grow_all[:, rs]
        diff = jnp.where(ii >= jj, gcol[rs, :c] - grow, 0.0)
        decay = jnp.where(ii >= jj, jnp.exp(diff), 0.0)
        kk = _dot_nt(kb.astype(BF16), kc.astype(BF16))
        a_mats.append(jnp.where(ii > jj, kk * decay, 0.0))
        decays.append(decay)
        kbs.append(kb)
        egcs.append(jnp.exp(gcol[rs]))

    invs = []
    for n in range(n_chunks):
        invs.append(eye - jnp.where((ii // 2 == jj // 2) & (ii > jj), a_mats[n], 0.0))
    s = 2
    while s < c:
        sel = ((ii // (2 * s)) == (jj // (2 * s))) & ((ii // s) > (jj // s))
        for n in range(n_chunks):
            d_b = invs[n].astype(BF16)
            x_b = jnp.where(sel, a_mats[n], 0.0).astype(BF16)
            t1 = _dot(d_b, x_b).astype(BF16)
            invs[n] = invs[n] - _dot(t1, d_b)
        s *= 2

    for n in range(n_chunks):
        rs = slice(n * c, (n + 1) * c)
        t_b = invs[n].astype(BF16)
        u_ref[rs, :] = _dot(t_b, (v[rs] * beta[rs]).astype(BF16)).astype(u_ref.dtype)
        w_ref[rs, :] = _dot(t_b, (kbs[n] * egcs[n]).astype(BF16)).astype(w_ref.dtype)
        qc = q[rs]
        kc = k[rs]
        qk = _dot_nt(qc.astype(BF16), kc.astype(BF16))
        intra = qk * decays[n]
        intra_ref[rs, :] = jnp.concatenate([intra, jnp.zeros_like(intra)], axis=1).astype(intra_ref.dtype)
        qe_ref[rs, :] = (qc * egcs[n]).astype(qe_ref.dtype)
        glast = gcol[(n + 1) * c - 1:(n + 1) * c, :]
        kd_t = (kc * jnp.exp(glast - gcol[rs])).T
        kd_ref[2 * n * c:(2 * n + 2) * c, :] = jnp.concatenate(
            [kd_t, jnp.zeros_like(kd_t)], axis=1).astype(kd_ref.dtype)
        egl_ref[n * SUBLANES:(n + 1) * SUBLANES, :] = jnp.broadcast_to(jnp.exp(glast),
                                                                       (SUBLANES, LANES))


def _dn_prep(proj, conv_w, betab, gcb, gct, t):
    rows = min(DN_PREP_ROWS, t)
    hb = BF16_SUBLANES
    nq = DN_WIDTH // LANES

    def blk(off):
        return pl.BlockSpec((rows, LANES), lambda h, i, off=off: (i, off + h))

    def halo(off):
        return pl.BlockSpec((hb, LANES),
                            lambda h, i, off=off: (jnp.maximum(i * (rows // hb) - 1, 0), off + h))

    def cw(off):
        return pl.BlockSpec((DN_CONV, LANES), lambda h, i, off=off: (0, off + h))

    per_head = pl.BlockSpec((rows, LANES), lambda h, i: (i, h))
    kd_t_spec = pl.BlockSpec((2 * rows, LANES), lambda h, i: (i, h))
    outs = [jax.ShapeDtypeStruct((t, DN_WIDTH), BF16)] * 3
    outs.append(jax.ShapeDtypeStruct((2 * t, DN_WIDTH), BF16))
    outs.append(jax.ShapeDtypeStruct((t, DN_WIDTH), BF16))
    outs.append(jax.ShapeDtypeStruct((t // DN_CHUNK * SUBLANES, DN_WIDTH), F32))
    return pl.pallas_call(
        _dn_prep_kernel,
        grid=(DN_HEADS, t // rows),
        in_specs=[blk(0), blk(nq), blk(2 * nq), halo(0), halo(nq), halo(2 * nq),
                  cw(0), cw(nq), cw(2 * nq), per_head, per_head,
                  pl.BlockSpec((None, 1, rows), lambda h, i: (h, 0, i))],
        out_specs=[per_head] * 3 + [kd_t_spec, per_head]
                  + [pl.BlockSpec((rows // DN_CHUNK * SUBLANES, LANES), lambda h, i: (i, h))],
        out_shape=outs,
        scratch_shapes=[pltpu.VMEM((3, hb + rows, LANES), F32)],
        compiler_params=_cparams(("parallel", "parallel")),
        name="dn_prep",
    )(proj, proj, proj, proj, proj, proj, conv_w, conv_w, conv_w, betab, gcb,
      gct.reshape(DN_HEADS, 1, t))


def _dn_scan_kernel(u_ref, w_ref, qe_ref, kd_ref, intra_ref, egl_ref, z_ref, nw_ref, o_ref, s_ref):
    i = pl.program_id(0)
    c = DN_CHUNK
    n_chunks = u_ref.shape[0] // c

    @pl.when(i == 0)
    def _():
        s_ref[...] = jnp.zeros_like(s_ref)

    nw = nw_ref[...]

    pw = 2 * LANES
    first = lax.broadcasted_iota(jnp.int32, (c, pw), 1) < LANES
    zeros_s = jnp.zeros((DN_HEAD_DIM, LANES), BF16)
    zeros_v = jnp.zeros((c, pw), BF16)
    first_k = lax.broadcasted_iota(jnp.int32, (DN_HEAD_DIM, pw), 1) < LANES
    zeros_k = jnp.zeros((DN_HEAD_DIM, pw), BF16)

    def chunk(n, carry):
        r0 = pl.multiple_of(n * c, c)
        k0 = pl.multiple_of(n * 2 * c, 2 * c)
        e0 = pl.multiple_of(n * SUBLANES, SUBLANES)
        for p in range(DN_HEADS // 2):
            ps = slice(p * pw, (p + 1) * pw)
            s_p = s_ref[p]
            s_b = s_p.astype(BF16)
            s_diag = jnp.concatenate(
                [jnp.concatenate([s_b[:, :LANES], zeros_s], axis=1),
                 jnp.concatenate([zeros_s, s_b[:, LANES:]], axis=1)], axis=0)
            wq = jnp.concatenate([w_ref[pl.ds(r0, c), ps], qe_ref[pl.ds(r0, c), ps]], axis=0)
            ws_qs = _dot(wq, s_diag)
            v_new = u_ref[pl.ds(r0, c), ps].astype(F32) - ws_qs[:c]
            v_b = v_new.astype(BF16)
            v_diag = jnp.concatenate([jnp.where(first, v_b, zeros_v), zeros_v,
                                      jnp.where(first, zeros_v, v_b), zeros_v], axis=0)
            kd_t = kd_ref[pl.ds(k0, 2 * c), ps]
            lhs = jnp.concatenate([intra_ref[pl.ds(r0, c), ps],
                                   jnp.where(first_k, kd_t, zeros_k),
                                   jnp.where(first_k, zeros_k, kd_t)], axis=0)
            iv = _dot(lhs, v_diag)
            o = ws_qs[c:] + iv[:c]
            eg = egl_ref[pl.ds(e0, SUBLANES), ps]
            s_dec = (s_p.reshape(DN_HEAD_DIM // SUBLANES, SUBLANES, pw) * eg[None]
                     ).reshape(DN_HEAD_DIM, pw)
            s_ref[p] = s_dec + iv[c:c + DN_HEAD_DIM] + iv[c + DN_HEAD_DIM:]
            z = z_ref[pl.ds(r0, c), ps].astype(F32)
            gate = z * _sigmoid(z)
            for hh in range(2):
                ls = slice(hh * LANES, (hh + 1) * LANES)
                o_h = o[:, ls]
                o_h = o_h * lax.rsqrt(jnp.mean(o_h * o_h, axis=-1, keepdims=True) + RMS_EPS)
                o_ref[pl.ds(r0, c), p * pw + hh * LANES:p * pw + (hh + 1) * LANES] = (
                    o_h * nw * gate[:, ls]).astype(o_ref.dtype)
        return carry

    lax.fori_loop(0, n_chunks, chunk, 0)


def _dn_scan(u, w, qe, kd, intra, egl, proj, norm_w, t):
    rows = min(512, t)
    full = pl.BlockSpec((rows, DN_WIDTH), lambda i: (i, 0))
    return pl.pallas_call(
        _dn_scan_kernel,
        grid=(t // rows,),
        in_specs=[full, full, full, pl.BlockSpec((2 * rows, DN_WIDTH), lambda i: (i, 0)), full,
                  pl.BlockSpec((rows // DN_CHUNK * SUBLANES, DN_WIDTH), lambda i: (i, 0)),
                  pl.BlockSpec((rows, DN_WIDTH), lambda i: (i, 3)),
                  pl.BlockSpec((1, LANES), lambda i: (0, 0))],
        out_specs=full,
        out_shape=jax.ShapeDtypeStruct((t, DN_WIDTH), BF16),
        scratch_shapes=[pltpu.VMEM((DN_HEADS // 2, DN_HEAD_DIM, 2 * DN_HEAD_DIM), F32)],
        compiler_params=_cparams(("arbitrary",)),
        name="dn_scan",
    )(u, w, qe, kd, intra, egl, proj, norm_w.reshape(1, LANES).astype(F32))


def _attn_kernel(*refs, dilation, span, n_earlier):
    q_ref, k_ref, v_ref, cos_ref, sin_ref = refs[:5]
    earlier = refs[5:5 + 2 * n_earlier]
    out_refs = refs[5 + 2 * n_earlier:-5]
    kbuf, vbuf, qbuf, obuf, lbuf = refs[-5:]
    i = pl.program_id(1)
    bt, width = q_ref.shape
    d = dilation
    blk = ATT_BLOCK
    prev = d * blk
    half = ATT_HEAD_DIM // 2
    scale = ATT_HEAD_DIM ** -0.5

    heads = width // LANES

    @pl.when(i == 0)
    def _():
        kbuf[:, 0:prev, :] = jnp.zeros((heads, prev, LANES), F32)
        vbuf[:, 0:prev, :] = jnp.zeros((heads, prev, LANES), F32)

    cos, sin = cos_ref[...], sin_ref[...]

    def rope(a):
        a = a.astype(F32)
        return a * cos + pltpu.roll(a, half, 1) * sin

    for h in range(heads):
        ls = slice(h * LANES, (h + 1) * LANES)
        qbuf[h] = rope(q_ref[:, ls]) * scale
        kbuf[h, prev:prev + bt, :] = rope(k_ref[:, ls])
        vbuf[h, prev:prev + bt, :] = v_ref[:, ls].astype(F32)

    qi = lax.broadcasted_iota(jnp.int32, (blk, 2 * blk), 0) + blk
    ki = lax.broadcasted_iota(jnp.int32, (blk, 2 * blk), 1)
    dist = qi - ki
    band = (dist >= 0) & (dist <= span)
    band_first = band & ((ki >= blk) | (i > 0))

    def rows(start, size):
        return pl.ds(start, size, stride=d) if d > 1 else pl.ds(start, size)

    for j in range(bt // prev):
        for r in range(d):
            r0 = j * prev + r
            for h in range(heads):
                qj = qbuf[h, rows(r0, blk), :].astype(BF16)
                kj = kbuf[h, rows(r0, 2 * blk), :].astype(BF16)
                vj = vbuf[h, rows(r0, 2 * blk), :].astype(BF16)
                s = _dot_nt(qj, kj)
                s = jnp.where(band_first if j == 0 else band, s, NEG_BIG)
                m = jnp.max(s, axis=-1, keepdims=True)
                p = jnp.exp(s - m)
                l = jnp.sum(p, axis=-1, keepdims=True)
                obuf[h, rows(r0, blk), :] = _dot(p.astype(BF16), vj) / l
                lbuf[h, rows(r0, blk), :] = jnp.broadcast_to(m + jnp.log(l), (blk, LANES))

    for h in range(heads):
        ls = slice(h * LANES, (h + 1) * LANES)
        if n_earlier == 0:
            o_ref, lse_ref = out_refs
            o_ref[:, ls] = obuf[h].astype(o_ref.dtype)
            lse_ref[:, ls] = lbuf[h]
        else:
            (o_ref,) = out_refs
            outs = [earlier[2 * g][:, ls].astype(F32) for g in range(n_earlier)] + [obuf[h]]
            lses = [earlier[2 * g + 1][:, ls] for g in range(n_earlier)] + [lbuf[h]]
            m = functools.reduce(jnp.maximum, lses)
            es = [jnp.exp(l - m) for l in lses]
            num = functools.reduce(lambda a, b: a + b, [e * o for e, o in zip(es, outs)])
            den = functools.reduce(lambda a, b: a + b, es)
            o_ref[:, ls] = (num / den).astype(o_ref.dtype)
    kbuf[:, 0:prev, :] = kbuf[:, bt:bt + prev, :]
    vbuf[:, 0:prev, :] = vbuf[:, bt:bt + prev, :]


def _attn_group(proj, cos_t, sin_t, gi, dilation, span, t, earlier=()):
    prev = dilation * ATT_BLOCK
    bt = max(min(512, t), prev)
    width = 2 * LANES
    gw = ATT_GROUP_WIDTH
    per_g = gw // width
    q0 = gi * gw // width
    k0 = q0 + ATT_WIDTH // width
    v0 = k0 + ATT_WIDTH // width

    def cur(c0):
        return pl.BlockSpec((bt, width), lambda hg, i, c0=c0: (i, c0 + hg))

    tab = pl.BlockSpec((bt, LANES), lambda hg, i: (i, 0))
    out = pl.BlockSpec((bt, width), lambda hg, i: (i, hg))
    flat_earlier = [a for pair in earlier for a in pair]
    if earlier:
        out_specs, out_shape = [out], [jax.ShapeDtypeStruct((t, gw), BF16)]
    else:
        out_specs = [out, out]
        out_shape = [jax.ShapeDtypeStruct((t, gw), BF16), jax.ShapeDtypeStruct((t, gw), F32)]
    return pl.pallas_call(
        functools.partial(_attn_kernel, dilation=dilation, span=span, n_earlier=len(earlier)),
        grid=(per_g, t // bt),
        in_specs=[cur(q0), cur(k0), cur(v0), tab, tab] + [out] * len(flat_earlier),
        out_specs=out_specs,
        out_shape=out_shape,
        scratch_shapes=[pltpu.VMEM((width // LANES, prev + bt, LANES), F32),
                        pltpu.VMEM((width // LANES, prev + bt, LANES), F32),
                        pltpu.VMEM((width // LANES, bt, LANES), F32),
                        pltpu.VMEM((width // LANES, bt, LANES), F32),
                        pltpu.VMEM((width // LANES, bt, LANES), F32)],
        compiler_params=_cparams(("parallel", "arbitrary")),
        name=f"attn_g{gi}",
    )(proj, proj, proj, cos_t, sin_t, *flat_earlier)


def _attention(proj, cos_t, sin_t, t):
    earlier = []
    for gi, (window, dilation) in enumerate(ATT_GROUPS[:-1]):
        earlier.append(_attn_group(proj, cos_t, sin_t, gi, dilation, window // dilation, t))
    window, dilation = ATT_GROUPS[-1]
    (o_att,) = _attn_group(proj, cos_t, sin_t, len(ATT_GROUPS) - 1, dilation, window // dilation, t,
                           earlier=tuple(earlier))
    return o_att


def _mix_kernel(odn_ref, oatt_ref, wdn_ref, watt_ref, gdn_ref, gatt_ref, out_ref):
    y_dn = _dot(odn_ref[...], wdn_ref[...])
    y_att = _dot(oatt_ref[...], watt_ref[...])
    merged = (_sigmoid(gdn_ref[...].astype(F32)) * y_dn
              + _sigmoid(gatt_ref[...].astype(F32)) * y_att)
    out_ref[...] = merged.astype(out_ref.dtype)


def _mix(o_dn, o_att, w_dn, w_att, proj, t, d):
    tm = min(1024, t)
    tn = ATT_GROUP_WIDTH
    g0 = 3 * ATT_WIDTH // tn
    row_dn = pl.BlockSpec((tm, DN_WIDTH), lambda j, i: (i, 0))
    row_g = pl.BlockSpec((tm, ATT_GROUP_WIDTH), lambda j, i: (i, 0))
    return pl.pallas_call(
        _mix_kernel,
        grid=(d // tn, t // tm),
        in_specs=[row_dn, row_g,
                  pl.BlockSpec((DN_WIDTH, tn), lambda j, i: (0, j)),
                  pl.BlockSpec((ATT_GROUP_WIDTH, tn), lambda j, i: (0, j)),
                  pl.BlockSpec((tm, tn), lambda j, i: (i, g0 + j)),
                  pl.BlockSpec((tm, tn), lambda j, i: (i, g0 + d // tn + j))],
        out_specs=pl.BlockSpec((tm, tn), lambda j, i: (i, j)),
        out_shape=jax.ShapeDtypeStruct((t, d), BF16),
        compiler_params=_cparams(("parallel", "parallel")),
        name="mix",
    )(o_dn, o_att, w_dn, w_att, proj, proj)


def _layer_norm(y, g, b):
    mu = jnp.mean(y, axis=-1, keepdims=True)
    yc = y - mu
    var = jnp.mean(yc * yc, axis=-1, keepdims=True)
    return yc * lax.rsqrt(var + LN_EPS) * g + b


def _wo_kernel(m_ref, w_ref, x_ref, y_ref, *, alpha):
    y_ref[...] = alpha * x_ref[...] + _dot(m_ref[...], w_ref[...])


def _wo_residual(merged, w_o, x2, t, d, alpha):
    tm, tn = min(1024, t), min(1024, d)
    tile = pl.BlockSpec((tm, tn), lambda i, j: (i, j))
    return pl.pallas_call(
        functools.partial(_wo_kernel, alpha=alpha),
        grid=(t // tm, d // tn),
        in_specs=[pl.BlockSpec((tm, d), lambda i, j: (i, 0)),
                  pl.BlockSpec((d, tn), lambda i, j: (0, j)),
                  tile],
        out_specs=tile,
        out_shape=jax.ShapeDtypeStruct((t, d), F32),
        compiler_params=_cparams(("parallel", "parallel")),
        name="wo",
    )(merged, w_o, x2)


def _router_kernel(y_ref, g_ref, b_ref, w_ref, bias_ref,
                   x1_ref, x1p_ref, topi_ref, topw_ref, rank_ref, cnt_ref, run_ref):
    i = pl.program_id(0)
    tm, d = y_ref.shape

    @pl.when(i == 0)
    def _():
        run_ref[...] = jnp.zeros_like(run_ref)

    x1 = _layer_norm(y_ref[...], g_ref[...], b_ref[...])
    x1_ref[...] = x1
    x1p_ref[...] = _pack_halves(x1[:, :d // 2], x1[:, d // 2:])

    x_hi, x_mid = _split2(x1)
    w_hi, w_mid = _split2(w_ref[...])
    logits = _dot(x_hi, w_hi) + _dot(x_hi, w_mid) + _dot(x_mid, w_hi)
    s = _sigmoid(logits)
    sc = s + bias_ref[...]
    lane = lax.broadcasted_iota(jnp.int32, (tm, N_EXPERTS), 1)
    grp = lane // GROUP_SIZE
    neg = -jnp.inf

    def first_argmax(v):
        m = jnp.max(v, axis=-1, keepdims=True)
        idx = jnp.min(jnp.where(v == m, lane, N_EXPERTS), axis=-1, keepdims=True)
        return m, idx

    gscore = []
    for gi in range(N_GROUPS):
        vg = jnp.where(grp == gi, sc, neg)
        m1, i1 = first_argmax(vg)
        m2 = jnp.max(jnp.where(lane == i1, neg, vg), axis=-1, keepdims=True)
        gscore.append(m1 + m2)
    emask = jnp.zeros((tm, N_EXPERTS), jnp.bool_)
    for gi in range(N_GROUPS):
        ahead = jnp.zeros((tm, 1), jnp.int32)
        for gj in range(N_GROUPS):
            if gj == gi:
                continue
            beats = (gscore[gj] > gscore[gi]) | ((gscore[gj] == gscore[gi]) & (gj < gi))
            ahead = ahead + beats.astype(jnp.int32)
        emask = emask | ((grp == gi) & (ahead < TOPK_GROUPS))
    masked = jnp.where(emask, sc, neg)

    sel =jnp.zeros((tm, N_EXPERTS), jnp.bool_)
    idxs, vals = [], []
    for _ in range(TOP_K):
        _, ik = first_argmax(masked)
        hit = lane == ik
        sel = sel | hit
        masked = jnp.where(hit, neg, masked)
        idxs.append(ik)
        vals.append(jnp.sum(jnp.where(hit, s, 0.0), axis=-1, keepdims=True))
    wsum = vals[0]
    for v in vals[1:]:
        wsum = wsum + v

    sel_b = jnp.where(sel, 1.0, 0.0).astype(BF16)
    strict = jnp.where(lax.broadcasted_iota(jnp.int32, (tm, tm), 1)
                       < lax.broadcasted_iota(jnp.int32, (tm, tm), 0), 1.0, 0.0).astype(BF16)
    rank_excl = run_ref[0:1, :] + _dot(strict, sel_b)
    run_new = run_ref[0:1, :] + jnp.sum(sel_b.astype(F32), axis=0, keepdims=True)
    run_ref[...] = jnp.broadcast_to(run_new, run_ref.shape)
    cnt_ref[...] = jnp.broadcast_to(run_new, cnt_ref.shape).astype(jnp.int32)

    topi = jnp.zeros((tm, N_EXPERTS), jnp.int32)
    topw = jnp.zeros((tm, N_EXPERTS), F32)
    rnk = jnp.zeros((tm, N_EXPERTS), F32)
    for kk in range(TOP_K):
        hit = lane == idxs[kk]
        rk = jnp.sum(jnp.where(hit, rank_excl, 0.0), axis=-1, keepdims=True)
        topi = jnp.where(lane == kk, idxs[kk], topi)
        topw = jnp.where(lane == kk, vals[kk] / wsum * ROUTED_SCALE, topw)
        rnk = jnp.where(lane == kk, rk, rnk)
    topi_ref[...] = topi
    topw_ref[...] = topw
    rank_ref[...] = rnk.astype(jnp.int32)


def _ln_router(y, g, b, w_router, bias, t, d):
    tm = min(256, t)
    row = pl.BlockSpec((tm, N_EXPERTS), lambda i: (i, 0))
    full = pl.BlockSpec((tm, d), lambda i: (i, 0))
    vec = pl.BlockSpec((1, d), lambda i: (0, 0))
    return pl.pallas_call(
        _router_kernel,
        grid=(t // tm,),
        in_specs=[full, vec, vec,
                  pl.BlockSpec((d, N_EXPERTS), lambda i: (0, 0)),
                  pl.BlockSpec((1, N_EXPERTS), lambda i: (0, 0))],
        out_specs=[full, pl.BlockSpec((tm, d // 2), lambda i: (i, 0)), row, row, row,
                   pl.BlockSpec((SUBLANES, N_EXPERTS), lambda i: (0, 0))],
        out_shape=[jax.ShapeDtypeStruct((t, d), F32),
                   jax.ShapeDtypeStruct((t, d // 2), jnp.uint32),
                   jax.ShapeDtypeStruct((t, N_EXPERTS), jnp.int32),
                   jax.ShapeDtypeStruct((t, N_EXPERTS), F32),
                   jax.ShapeDtypeStruct((t, N_EXPERTS), jnp.int32),
                   jax.ShapeDtypeStruct((SUBLANES, N_EXPERTS), jnp.int32)],
        scratch_shapes=[pltpu.VMEM((SUBLANES, N_EXPERTS), F32)],
        compiler_params=_cparams(("arbitrary",)),
        name="ln_router",
    )(y, g.reshape(1, d).astype(F32), b.reshape(1, d).astype(F32), w_router.astype(F32),
      bias.reshape(1, N_EXPERTS).astype(F32))


def _slot_kernel(topi_ref, rank_ref, start_ref, pos_ref):
    topi = topi_ref[...]
    lane = lax.broadcasted_iota(jnp.int32, topi.shape, 1)
    start = jnp.broadcast_to(start_ref[...], topi.shape)
    pos = rank_ref[...]
    for kk in range(TOP_K):
        seg = jnp.sum(jnp.where(lane == topi[:, kk:kk + 1], start, 0.0), axis=-1, keepdims=True)
        pos = jnp.where(lane == kk, pos + seg.astype(jnp.int32), pos)
    pos_ref[...] = pos


def _slots(topi, rank, pad_start, t):
    tm = min(1024, t)
    row = pl.BlockSpec((tm, N_EXPERTS), lambda i: (i, 0))
    return pl.pallas_call(
        _slot_kernel,
        grid=(t // tm,),
        in_specs=[row, row, pl.BlockSpec((1, N_EXPERTS), lambda i: (0, 0))],
        out_specs=row,
        out_shape=jax.ShapeDtypeStruct((t, N_EXPERTS), jnp.int32),
        compiler_params=_cparams(("parallel",)),
        name="slots",
    )(topi, rank, pad_start.astype(F32).reshape(1, N_EXPERTS))


def _row_copy(src_ref, src_row, dst_ref, dst_row, sem):
    return pltpu.make_async_copy(src_ref.at[pl.ds(src_row, 1)], dst_ref.at[pl.ds(dst_row, 1)], sem)


def _dispatch_kernel(pos_ref, x_ref, xs_ref, sem):
    i = pl.program_id(0)
    tb = x_ref.shape[0]
    base = i * (tb * TOP_K)

    def start(r, carry):
        for kk in range(TOP_K):
            _row_copy(x_ref, r, xs_ref, pos_ref[base + r * TOP_K + kk], sem).start()
        return carry

    lax.fori_loop(0, tb, start, 0)
    for kk in range(TOP_K):
        pltpu.make_async_copy(x_ref, xs_ref.at[pl.ds(0, tb)], sem).wait()


def _dispatch(pos_flat, x1, n_slots, t, d):
    tb = min(256, t)
    return pl.pallas_call(
        _dispatch_kernel,
        grid_spec=pltpu.PrefetchScalarGridSpec(
            num_scalar_prefetch=1,
            grid=(t // tb,),
            in_specs=[pl.BlockSpec((tb, d), lambda i, pos: (i, 0))],
            out_specs=pl.BlockSpec(memory_space=pl.ANY),
            scratch_shapes=[pltpu.SemaphoreType.DMA],
        ),
        out_shape=jax.ShapeDtypeStruct((n_slots, d), x1.dtype),
        compiler_params=_cparams(("arbitrary",)),
        name="dispatch",
    )(pos_flat, x1)


def _expert_kernel(be_ref, nv_ref, first_ref, nxt_ref, ord_ref, nu_ref, xs_ref, wgu_hbm, wdn_hbm,
                   y_ref, gu_stage, dn_stage, gu_cache, dn_cache, act_ref, sem):
    b = pl.program_id(0)
    n_used = nu_ref[0]

    def weight_copies(e):
        return (pltpu.make_async_copy(wgu_hbm.at[e], gu_stage, sem.at[0]),
                pltpu.make_async_copy(wdn_hbm.at[e], dn_stage, sem.at[1]))

    @pl.when(b == 0)
    def _():
        act_ref[...] = jnp.zeros_like(act_ref)

    @pl.when(b <= n_used)
    def _():
        rows, hd = xs_ref.shape
        kc = min(EXPERT_CHUNK, hd)
        cur = jnp.minimum(b, n_used - 1)
        e = be_ref[cur]

        @pl.when((first_ref[cur] == 1) & (b < n_used))
        def _():
            @pl.when(b == 0)
            def _():
                for cp in weight_copies(e):
                    cp.start()

            for cp in weight_copies(e):
                cp.wait()
            for c0 in range(0, 2 * hd, kc):
                for j in range(EXPERT_FF // LANES):
                    gu_cache[c0:c0 + kc, 2 * j * LANES:(2 * j + 1) * LANES] = (
                        gu_stage[c0:c0 + kc, j * LANES:(j + 1) * LANES].astype(BF16))
                    gu_cache[c0:c0 + kc, (2 * j + 1) * LANES:(2 * j + 2) * LANES] = (
                        gu_stage[c0:c0 + kc, EXPERT_FF + j * LANES:EXPERT_FF + (j + 1) * LANES]
                        .astype(BF16))
                dn_cache[ord_ref[cur] % 2, :, c0:c0 + kc] = dn_stage[:, c0:c0 + kc].astype(BF16)

            @pl.when(nxt_ref[cur] >= 0)
            def _():
                for cp in weight_copies(nxt_ref[cur]):
                    cp.start()

        prev_slot = ord_ref[jnp.maximum(b - 1, 0)] % 2
        act_prev = act_ref[...]
        nc = min(EXPERT_OUT_CHUNK, hd)
        for c0 in range(0, hd, nc):
            y_ref[:, c0:c0 + nc] = _pack_halves(
                _dot(act_prev, dn_cache[prev_slot, :, c0:c0 + nc]),
                _dot(act_prev, dn_cache[prev_slot, :, hd + c0:hd + c0 + nc]))

        rid = lax.broadcasted_iota(jnp.int32, (rows, 1), 0)
        valid = rid < nv_ref[cur]
        lo, hi = _unpack_halves(xs_ref[...])
        x_lo = jnp.where(valid, lo, 0.0).astype(BF16)
        x_hi = jnp.where(valid, hi, 0.0).astype(BF16)
        for j in range(EXPERT_FF // LANES):
            cs = slice(2 * j * LANES, (2 * j + 2) * LANES)
            hgu = _dot(x_lo, gu_cache[:hd, cs]) + _dot(x_hi, gu_cache[hd:, cs])
            gate, up = hgu[:, :LANES], hgu[:, LANES:]
            act_ref[:, j * LANES:(j + 1) * LANES] = (gate * _sigmoid(gate) * up).astype(BF16)


def _experts(blk_expert, blk_valid, blk_first, blk_next, blk_ord, n_used, xs, w_gu, w_dn, d):
    n_slots = xs.shape[0]
    nb = n_slots // MOE_ROWS

    def in_map(b, be, nv, fi, nx, od, nu):
        return (jnp.minimum(b, nu[0] - 1), 0)

    def out_map(b, be, nv, fi, nx, od, nu):
        return (jnp.minimum(jnp.maximum(b - 1, 0), nu[0] - 1), 0)

    return pl.pallas_call(
        _expert_kernel,
        grid_spec=pltpu.PrefetchScalarGridSpec(
            num_scalar_prefetch=6,
            grid=(nb + 1,),
            in_specs=[pl.BlockSpec((MOE_ROWS, d // 2), in_map),
                      pl.BlockSpec(memory_space=pl.ANY),
                      pl.BlockSpec(memory_space=pl.ANY)],
            out_specs=pl.BlockSpec((MOE_ROWS, d // 2), out_map),
            scratch_shapes=[pltpu.VMEM((d, 2 * EXPERT_FF), w_gu.dtype),
                            pltpu.VMEM((EXPERT_FF, d), w_dn.dtype),
                            pltpu.VMEM((d, 2 * EXPERT_FF), BF16),
                            pltpu.VMEM((2, EXPERT_FF, d), BF16),
                            pltpu.VMEM((MOE_ROWS, EXPERT_FF), BF16),
                            pltpu.SemaphoreType.DMA((2,))],
        ),
        out_shape=jax.ShapeDtypeStruct((n_slots, d // 2), jnp.uint32),
        compiler_params=_cparams(("arbitrary",)),
        name="experts",
    )(blk_expert, blk_valid, blk_first, blk_next, blk_ord, n_used, xs, w_gu, w_dn)


def _shared_kernel(x_ref, wgu_ref, wdn_ref, y_ref):
    hd = x_ref.shape[1]
    lo, hi = _unpack_halves(x_ref[...])
    hgu = _dot(lo.astype(BF16), wgu_ref[:hd, :]) + _dot(hi.astype(BF16), wgu_ref[hd:, :])
    gate, up = hgu[:, :EXPERT_FF], hgu[:, EXPERT_FF:]
    act = (gate * _sigmoid(gate) * up).astype(BF16)
    y_ref[...] = _dot(act, wdn_ref[...]).astype(y_ref.dtype)


def _shared(x1p, w_gu, w_dn, t, d):
    tm = min(512, t)
    return pl.pallas_call(
        _shared_kernel,
        grid=(t // tm,),
        in_specs=[pl.BlockSpec((tm, d // 2), lambda i: (i, 0)),
                  pl.BlockSpec((d, 2 * EXPERT_FF), lambda i: (0, 0)),
                  pl.BlockSpec((EXPERT_FF, d), lambda i: (0, 0))],
        out_specs=pl.BlockSpec((tm, d), lambda i: (i, 0)),
        out_shape=jax.ShapeDtypeStruct((t, d), BF16),
        compiler_params=_cparams(("parallel",)),
        name="shared",
    )(x1p, w_gu, w_dn)


def _combine_kernel(pos_ref, x_ref, ysh_ref, topw_ref, g_ref, b_ref, y_hbm, out_ref, buf_ref, sem,
                    *, alpha):
    i = pl.program_id(0)
    n = pl.num_programs(0)
    tb = x_ref.shape[0]

    def gather(tile, slot):
        base = tile * (tb * TOP_K)

        def body(r, carry):
            for kk in range(TOP_K):
                _row_copy(y_hbm, pos_ref[base + r * TOP_K + kk], buf_ref.at[slot, kk], r,
                          sem.at[slot]).start()
            return carry

        lax.fori_loop(0, tb, body, 0)

    @pl.when(i == 0)
    def _():
        gather(0, 0)

    @pl.when(i + 1 < n)
    def _():
        gather(i + 1, (i + 1) % 2)

    slot = i % 2
    for kk in range(TOP_K):
        pltpu.make_async_copy(y_hbm.at[pl.ds(0, tb)], buf_ref.at[slot, kk], sem.at[slot]).wait()
    topw = topw_ref[...]
    hd = x_ref.shape[1] // 2
    acc_lo = alpha * x_ref[:, :hd] + ysh_ref[:, :hd].astype(F32)
    acc_hi = alpha * x_ref[:, hd:] + ysh_ref[:, hd:].astype(F32)
    for kk in range(TOP_K):
        lo, hi = _unpack_halves(buf_ref[slot, kk])
        acc_lo = acc_lo + topw[:, kk:kk + 1] * lo
        acc_hi = acc_hi + topw[:, kk:kk + 1] * hi
    inv_d = 1.0 / (2 * hd)
    mu = (jnp.sum(acc_lo, axis=-1, keepdims=True) + jnp.sum(acc_hi, axis=-1, keepdims=True)) * inv_d
    c_lo, c_hi = acc_lo - mu, acc_hi - mu
    var = (jnp.sum(c_lo * c_lo, axis=-1, keepdims=True)
           + jnp.sum(c_hi * c_hi, axis=-1, keepdims=True)) * inv_d
    rstd = lax.rsqrt(var + LN_EPS)
    out_ref[:, :hd] = c_lo * rstd * g_ref[:, :hd] + b_ref[:, :hd]
    out_ref[:, hd:] = c_hi * rstd * g_ref[:, hd:] + b_ref[:, hd:]


def _combine(pos_flat, x1, ysh, topw, g, b, y, t, d, alpha):
    tb = min(128, t)
    row = pl.BlockSpec((tb, d), lambda i, pos: (i, 0))
    vec = pl.BlockSpec((1, d), lambda i, pos: (0, 0))
    return pl.pallas_call(
        functools.partial(_combine_kernel, alpha=alpha),
        grid_spec=pltpu.PrefetchScalarGridSpec(
            num_scalar_prefetch=1,
            grid=(t // tb,),
            in_specs=[row, row, pl.BlockSpec((tb, N_EXPERTS), lambda i, pos: (i, 0)), vec, vec,
                      pl.BlockSpec(memory_space=pl.ANY)],
            out_specs=row,
            scratch_shapes=[pltpu.VMEM((2, TOP_K, tb, d // 2), jnp.uint32),
                            pltpu.SemaphoreType.DMA((2,))],
        ),
        out_shape=jax.ShapeDtypeStruct((t, d), F32),
        compiler_params=_cparams(("arbitrary",)),
        name="combine",
    )(pos_flat, x1, ysh, topw, g.reshape(1, d).astype(F32), b.reshape(1, d).astype(F32), y)


def _rope_tables(t):
    half = ATT_HEAD_DIM // 2
    inv_freq = ROPE_THETA ** (-jnp.arange(half, dtype=F32) / half)
    ang = jnp.arange(t, dtype=F32)[:, None] * inv_freq[None, :]
    cos, sin = jnp.cos(ang), jnp.sin(ang)
    return jnp.concatenate([cos, cos], -1), jnp.concatenate([-sin, sin], -1)


def _token_mixer(x2, w_in, conv_w, a_log, dt_bias, dn_norm_w, w_dn_branch, w_att_branch, w_o,
                 alpha):
    t, d = x2.shape
    n_main = 4 * DN_WIDTH
    n_ba = 2 * DN_HEADS
    w_t = w_in.T
    betab, gcb, gct, x_b = _dn_gates(x2, w_t, n_main, a_log, dt_bias)
    proj_dn = _proj(x_b, w_t, 0, n_main, "proj_dn")
    proj_att = _proj(x_b, w_t, n_main + n_ba, w_t.shape[0] - n_main - n_ba, "proj_att")
    u, w, qe, kd, intra, egl = _dn_prep(proj_dn, conv_w.astype(F32), betab, gcb, gct, t)
    o_dn = _dn_scan(u, w, qe, kd, intra, egl, proj_dn, dn_norm_w, t)

    cos_t, sin_t = _rope_tables(t)
    o_att = _attention(proj_att, cos_t, sin_t, t)
    merged = _mix(o_dn, o_att, w_dn_branch.astype(BF16), w_att_branch.astype(BF16), proj_att, t, d)
    return _wo_residual(merged, w_o.astype(BF16), x2, t, d, alpha)


def _moe(y_mix, ln1_g, ln1_b, w_router, router_bias, w_exp_gate_up, w_exp_down, w_sh_gate_up,
         w_sh_down, ln_g, ln_b, alpha):
    t, d = y_mix.shape
    r = MOE_ROWS
    x1, x1p, topi, topw, rank, cnt = _ln_router(y_mix, ln1_g, ln1_b, w_router, router_bias, t, d)
    counts = cnt[0]
    padded = (counts + r - 1) // r * r
    pad_end = jnp.cumsum(padded)
    pad_start = pad_end - padded
    nb = (t * TOP_K) // r + N_EXPERTS
    blk_row0 = jnp.arange(nb, dtype=jnp.int32) * r
    in_blk = (blk_row0[:, None] >= pad_start[None, :]) & (blk_row0[:, None] < pad_end[None, :])
    expert_ids = jnp.arange(N_EXPERTS, dtype=jnp.int32)
    blk_expert = jnp.where(blk_row0 < pad_end[-1],
                           jnp.sum(jnp.where(in_blk, expert_ids[None, :], 0), axis=1),
                           N_EXPERTS - 1).astype(jnp.int32)
    blk_valid = jnp.sum(jnp.where(in_blk, jnp.clip(counts[None, :] - (blk_row0[:, None] - pad_start[None, :]), 0, r), 0),
                        axis=1).astype(jnp.int32)
    n_used = (pad_end[-1:] // r).astype(jnp.int32)
    blk_ids = jnp.arange(nb, dtype=jnp.int32)
    prev_expert = jnp.concatenate([jnp.full((1,), -1, jnp.int32), blk_expert[:-1]])
    blk_first = (blk_expert != prev_expert).astype(jnp.int32)
    blk_ord = jnp.cumsum(blk_first) - 1
    after = jnp.sum(jnp.where(in_blk, pad_end[None, :] // r, 0), axis=1)
    after_expert = jnp.sum(jnp.where(after[:, None] == blk_ids[None, :], blk_expert[None, :], 0), axis=1)
    blk_next = jnp.where(after < n_used[0], after_expert, -1).astype(jnp.int32)
    pos_flat = _slots(topi, rank, pad_start, t)[:, :TOP_K].reshape(-1)

    xs = _dispatch(pos_flat, x1p, nb * r, t, d // 2)
    y = _experts(blk_expert, blk_valid, blk_first, blk_next, blk_ord.astype(jnp.int32), n_used, xs,
                 w_exp_gate_up, w_exp_down, d)
    ysh = _shared(x1p, w_sh_gate_up.astype(BF16), w_sh_down.astype(BF16), t, d)
    return _combine(pos_flat, x1, ysh, topw, ln_g, ln_b, y, t, d, alpha)


def kernel(x, w_in, conv_w, a_log, dt_bias, dn_norm_w, w_dn_branch, w_att_branch, w_o, ln1_g, ln1_b,
           w_router, router_bias, w_exp_gate_up, w_exp_down, w_sh_gate_up, w_sh_down, ln2_g, ln2_b):
    bsz, t, d = x.shape
    depth = w_in.shape[0]
    alpha = (2.0 * depth) ** 0.25
    outs = []
    for bi in range(bsz):
        xb = x[bi]
        for l in range(depth):
            y_mix = _token_mixer(xb, w_in[l], conv_w[l], a_log[l], dt_bias[l], dn_norm_w[l],
                                 w_dn_branch[l], w_att_branch[l], w_o[l], alpha)
            xb = _moe(y_mix, ln1_g[l], ln1_b[l], w_router[l], router_bias[l], w_exp_gate_up[l],
                      w_exp_down[l], w_sh_gate_up[l], w_sh_down[l], ln2_g[l], ln2_b[l], alpha)
        outs.append(xb)
    return jnp.stack(outs, 0)
```

```python
import functools

import jax
import jax.numpy as jnp
from jax import lax
from jax.experimental import pallas as pl
from jax.experimental.pallas import tpu as pltpu

F32 = jnp.float32
BF16 = jnp.bfloat16

DN_HEADS = 16
DN_HEAD_DIM = 128
DN_WIDTH = DN_HEADS * DN_HEAD_DIM
DN_CONV = 4
DN_CHUNK = 64
ATT_GROUPS = ((128, 1), (512, 4), (2048, 16))
ATT_HEADS_PER_GROUP = 8
ATT_HEAD_DIM = 128
ATT_GROUP_WIDTH = ATT_HEADS_PER_GROUP * ATT_HEAD_DIM
ATT_WIDTH = len(ATT_GROUPS) * ATT_GROUP_WIDTH
ATT_BLOCK = 128
ROPE_THETA = 10000.0
N_EXPERTS = 128
TOP_K = 8
N_GROUPS = 8
GROUP_SIZE = N_EXPERTS // N_GROUPS
TOPK_GROUPS = 4
EXPERT_FF = 384
ROUTED_SCALE = 2.5
LN_EPS = 1e-5
RMS_EPS = 1e-6
L2_EPS = 1e-6

LANES = 128
SUBLANES = 8
BF16_SUBLANES = 16
VMEM_LIMIT = 56 * 1024 * 1024

DN_PREP_ROWS = 1024
MOE_ROWS = 256
EXPERT_CHUNK = 1024
EXPERT_OUT_CHUNK = 512
NEG_BIG = -1e30


def _cparams(sem, vmem=VMEM_LIMIT):
    return pltpu.CompilerParams(dimension_semantics=sem, vmem_limit_bytes=vmem)


def _split3(a):
    hi = a.astype(BF16)
    r1 = a - hi.astype(F32)
    mid = r1.astype(BF16)
    lo = (r1 - mid.astype(F32)).astype(BF16)
    return hi, mid, lo


def _split2(a):
    hi = a.astype(BF16)
    return hi, (a - hi.astype(F32)).astype(BF16)


def _dot(a, b):
    return jnp.dot(a, b, preferred_element_type=F32)


def _dot_nt(a, b):
    return lax.dot_general(a, b, (((1,), (1,)), ((), ())), preferred_element_type=F32)


def _dot_tn(a, b):
    return lax.dot_general(a, b, (((0,), (0,)), ((), ())), preferred_element_type=F32)


def _sigmoid(x):
    return 1.0 / (1.0 + jnp.exp(-x))


def _pack_halves(lo, hi):
    lo_b = lax.bitcast_convert_type(lo.astype(BF16).astype(F32), jnp.uint32)
    hi_b = lax.bitcast_convert_type(hi.astype(BF16).astype(F32), jnp.uint32)
    return (hi_b & jnp.uint32(0xFFFF0000)) | (lo_b >> 16)


def _unpack_halves(u):
    lo = lax.bitcast_convert_type(u << 16, F32)
    hi = lax.bitcast_convert_type(u & jnp.uint32(0xFFFF0000), F32)
    return lo, hi


def _proj_kernel(x_ref, wt_ref, o_ref):
    o_ref[...] = _dot_nt(x_ref[...], wt_ref[...].astype(BF16)).astype(o_ref.dtype)


def _proj(x_b, w_t, row0, n, name):
    m, k = x_b.shape
    tm, tn = min(1024, m), min(512, n)
    return pl.pallas_call(
        _proj_kernel,
        grid=(m // tm, n // tn),
        in_specs=[pl.BlockSpec((tm, k), lambda i, j: (i, 0)),
                  pl.BlockSpec((pl.Element(tn), pl.Element(k)),
                               lambda i, j: (pl.multiple_of(row0 + j * tn, SUBLANES), 0))],
        out_specs=pl.BlockSpec((tm, tn), lambda i, j: (i, j)),
        out_shape=jax.ShapeDtypeStruct((m, n), BF16),
        compiler_params=_cparams(("parallel", "parallel")),
        name=name,
    )(x_b, w_t)


def _gates_kernel(x_ref, w_ref, prm_ref, betab_ref, gcb_ref, gct_ref, xb_ref):
    tm = x_ref.shape[0]
    x_hi, x_mid = _split2(x_ref[...])
    xb_ref[...] = x_hi
    w_hi, w_mid = _split2(w_ref[...])
    logits = _dot_nt(x_hi, w_hi) + _dot_nt(x_hi, w_mid) + _dot_nt(x_mid, w_hi)

    def softplus(v):
        return jnp.maximum(v, 0.0) + jnp.log(1.0 + jnp.exp(-jnp.abs(v)))

    prm = prm_ref[...]
    beta = _sigmoid(logits)
    g = prm[0:1, :] * softplus(logits + prm[1:2, :])

    ri = lax.broadcasted_iota(jnp.int32, (tm, tm), 0)
    ci = lax.broadcasted_iota(jnp.int32, (tm, tm), 1)
    same = (ri // DN_CHUNK) == (ci // DN_CHUNK)
    lower = jnp.where(same & (ci <= ri), 1.0, 0.0).astype(BF16)
    gc = jnp.zeros((tm, LANES), F32)
    for part in _split3(g):
        gc = gc + _dot(lower, part)
    gct_ref[...] = gc.T[DN_HEADS:2 * DN_HEADS, :]
    for h in range(DN_HEADS):
        sl = slice(h * LANES, (h + 1) * LANES)
        betab_ref[:, sl] = jnp.broadcast_to(beta[:, h:h + 1], (tm, LANES))
        gcb_ref[:, sl] = jnp.broadcast_to(gc[:, DN_HEADS + h:DN_HEADS + h + 1], (tm, LANES))


def _dn_gates(x2, w_t, col0, a_log, dt_bias):
    t, d = x2.shape
    tm = min(512, t)
    neg_a = -jnp.exp(a_log.astype(F32))
    prm = jnp.zeros((SUBLANES, LANES), F32)
    prm = prm.at[0, DN_HEADS:2 * DN_HEADS].set(neg_a).at[1, DN_HEADS:2 * DN_HEADS].set(dt_bias.astype(F32))
    return pl.pallas_call(
        _gates_kernel,
        grid=(t // tm,),
        in_specs=[pl.BlockSpec((tm, d), lambda i: (i, 0)),
                  pl.BlockSpec((LANES, d), lambda i: (col0 // LANES, 0)),
                  pl.BlockSpec((SUBLANES, LANES), lambda i: (0, 0))],
        out_specs=[pl.BlockSpec((tm, DN_WIDTH), lambda i: (i, 0)),
                   pl.BlockSpec((tm, DN_WIDTH), lambda i: (i, 0)),
                   pl.BlockSpec((DN_HEADS, tm), lambda i: (0, i)),
                   pl.BlockSpec((tm, d), lambda i: (i, 0))],
        out_shape=[jax.ShapeDtypeStruct((t, DN_WIDTH), F32),
                   jax.ShapeDtypeStruct((t, DN_WIDTH), F32),
                   jax.ShapeDtypeStruct((DN_HEADS, t), F32),
                   jax.ShapeDtypeStruct((t, d), BF16)],
        compiler_params=_cparams(("parallel",)),
        name="dn_gates",
    )(x2, w_t.astype(F32), prm)


def _dn_prep_kernel(q_ref, k_ref, v_ref, hq_ref, hk_ref, hv_ref, cq_ref, ck_ref, cv_ref,
                    betab_ref, gcb_ref, gct_ref,
                    u_ref, w_ref, qe_ref, kd_ref, intra_ref, egl_ref, cbuf):
    i = pl.program_id(1)
    rows = q_ref.shape[0]
    c = DN_CHUNK
    halo_rows = hq_ref.shape[0]

    def conv_silu(slot, x_ref, halo_ref, cw_ref):
        cbuf[slot, 0:halo_rows, :] = jnp.where(i > 0, halo_ref[...].astype(F32), 0.0)
        cbuf[slot, halo_rows:halo_rows + rows, :] = x_ref[...].astype(F32)
        cw = cw_ref[...]
        y = jnp.zeros((rows, LANES), F32)
        for j in range(DN_CONV):
            off = halo_rows - (DN_CONV - 1) + j
            y = y + cw[j:j + 1, :] * cbuf[slot, off:off + rows, :]
        return y * _sigmoid(y)

    def l2n(a):
        return a * lax.rsqrt(jnp.sum(a * a, axis=-1, keepdims=True) + L2_EPS)

    q = l2n(conv_silu(0, q_ref, hq_ref, cq_ref)) * (DN_HEAD_DIM ** -0.5)
    k = l2n(conv_silu(1, k_ref, hk_ref, ck_ref))
    v = conv_silu(2, v_ref, hv_ref, cv_ref)
    beta = betab_ref[...]
    gcol = gcb_ref[...]
    grow_all = gct_ref[...]

    ii = lax.broadcasted_iota(jnp.int32, (c, c), 0)
    jj = lax.broadcasted_iota(jnp.int32, (c, c), 1)
    eye = jnp.where(ii == jj, 1.0, 0.0)
    n_chunks = rows // c

    decays, a_mats = [], []
    kbs, egcs = [], []
    for n in range(n_chunks):
        rs = slice(n * c, (n + 1) * c)
        kc = k[rs]
        kb = kc * beta[rs]
        grow = grow_all[:, rs]
        diff = jnp.where(ii >= jj, gcol[rs, :c] - grow, 0.0)
        decay = jnp.where(ii >= jj, jnp.exp(diff), 0.0)
        kk = _dot_nt(kb.astype(BF16), kc.astype(BF16))
        a_mats.append(jnp.where(ii > jj, kk * decay, 0.0))
        decays.append(decay)
        kbs.append(kb)
        egcs.append(jnp.exp(gcol[rs]))

    invs = []
    for n in range(n_chunks):
        invs.append(eye - jnp.where((ii // 2 == jj // 2) & (ii > jj), a_mats[n], 0.0))
    s = 2
    while s < c:
        sel = ((ii // (2 * s)) == (jj // (2 * s))) & ((ii // s) > (jj // s))
        for n in range(n_chunks):
            d_b = invs[n].astype(BF16)
            x_b = jnp.where(sel, a_mats[n], 0.0).astype(BF16)
            t1 = _dot(d_b, x_b).astype(BF16)
            invs[n] = invs[n] - _dot(t1, d_b)
        s *= 2

    for n in range(n_chunks):
        rs = slice(n * c, (n + 1) * c)
        t_b = invs[n].astype(BF16)
        u_ref[rs, :] = _dot(t_b, (v[rs] * beta[rs]).astype(BF16)).astype(u_ref.dtype)
        w_ref[rs, :] = _dot(t_b, (kbs[n] * egcs[n]).astype(BF16)).astype(w_ref.dtype)
        qc = q[rs]
        kc = k[rs]
        qk = _dot_nt(qc.astype(BF16), kc.astype(BF16))
        intra = qk * decays[n]
        intra_ref[rs, :] = jnp.concatenate([intra, jnp.zeros_like(intra)], axis=1).astype(intra_ref.dtype)
        qe_ref[rs, :] = (qc * egcs[n]).astype(qe_ref.dtype)
        glast = gcol[(n + 1) * c - 1:(n + 1) * c, :]
        kd_t = (kc * jnp.exp(glast - gcol[rs])).T
        kd_ref[2 * n * c:(2 * n + 2) * c, :] = jnp.concatenate(
            [kd_t, jnp.zeros_like(kd_t)], axis=1).astype(kd_ref.dtype)
        egl_ref[n * SUBLANES:(n + 1) * SUBLANES, :] = jnp.broadcast_to(jnp.exp(glast),
                                                                       (SUBLANES, LANES))


def _dn_prep(proj, conv_w, betab, gcb, gct, t):
    rows = min(DN_PREP_ROWS, t)
    hb = BF16_SUBLANES
    nq = DN_WIDTH // LANES

    def blk(off):
        return pl.BlockSpec((rows, LANES), lambda h, i, off=off: (i, off + h))

    def halo(off):
        return pl.BlockSpec((hb, LANES),
                            lambda h, i, off=off: (jnp.maximum(i * (rows // hb) - 1, 0), off + h))

    def cw(off):
        return pl.BlockSpec((DN_CONV, LANES), lambda h, i, off=off: (0, off + h))

    per_head = pl.BlockSpec((rows, LANES), lambda h, i: (i, h))
    kd_t_spec = pl.BlockSpec((2 * rows, LANES), lambda h, i: (i, h))
    outs = [jax.ShapeDtypeStruct((t, DN_WIDTH), BF16)] * 3
    outs.append(jax.ShapeDtypeStruct((2 * t, DN_WIDTH), BF16))
    outs.append(jax.ShapeDtypeStruct((t, DN_WIDTH), BF16))
    outs.append(jax.ShapeDtypeStruct((t // DN_CHUNK * SUBLANES, DN_WIDTH), F32))
    return pl.pallas_call(
        _dn_prep_kernel,
        grid=(DN_HEADS, t // rows),
        in_specs=[blk(0), blk(nq), blk(2 * nq), halo(0), halo(nq), halo(2 * nq),
                  cw(0), cw(nq), cw(2 * nq), per_head, per_head,
                  pl.BlockSpec((None, 1, rows), lambda h, i: (h, 0, i))],
        out_specs=[per_head] * 3 + [kd_t_spec, per_head]
                  + [pl.BlockSpec((rows // DN_CHUNK * SUBLANES, LANES), lambda h, i: (i, h))],
        out_shape=outs,
        scratch_shapes=[pltpu.VMEM((3, hb + rows, LANES), F32)],
        compiler_params=_cparams(("parallel", "parallel")),
        name="dn_prep",
    )(proj, proj, proj, proj, proj, proj, conv_w, conv_w, conv_w, betab, gcb,
      gct.reshape(DN_HEADS, 1, t))


def _dn_scan_kernel(u_ref, w_ref, qe_ref, kd_ref, intra_ref, egl_ref, z_ref, nw_ref, o_ref, s_ref):
    i = pl.program_id(0)
    c = DN_CHUNK
    n_chunks = u_ref.shape[0] // c

    @pl.when(i == 0)
    def _():
        s_ref[...] = jnp.zeros_like(s_ref)

    nw = nw_ref[...]

    pw = 2 * LANES
    first = lax.broadcasted_iota(jnp.int32, (c, pw), 1) < LANES
    zeros_s = jnp.zeros((DN_HEAD_DIM, LANES), BF16)
    zeros_v = jnp.zeros((c, pw), BF16)
    first_k = lax.broadcasted_iota(jnp.int32, (DN_HEAD_DIM, pw), 1) < LANES
    zeros_k = jnp.zeros((DN_HEAD_DIM, pw), BF16)

    def chunk(n, carry):
        r0 = pl.multiple_of(n * c, c)
        k0 = pl.multiple_of(n * 2 * c, 2 * c)
        e0 = pl.multiple_of(n * SUBLANES, SUBLANES)
        for p in range(DN_HEADS // 2):
            ps = slice(p * pw, (p + 1) * pw)
            s_p = s_ref[p]
            s_b = s_p.astype(BF16)
            s_diag = jnp.concatenate(
                [jnp.concatenate([s_b[:, :LANES], zeros_s], axis=1),
                 jnp.concatenate([zeros_s, s_b[:, LANES:]], axis=1)], axis=0)
            wq = jnp.concatenate([w_ref[pl.ds(r0, c), ps], qe_ref[pl.ds(r0, c), ps]], axis=0)
            ws_qs = _dot(wq, s_diag)
            v_new = u_ref[pl.ds(r0, c), ps].astype(F32) - ws_qs[:c]
            v_b = v_new.astype(BF16)
            v_diag = jnp.concatenate([jnp.where(first, v_b, zeros_v), zeros_v,
                                      jnp.where(first, zeros_v, v_b), zeros_v], axis=0)
            kd_t = kd_ref[pl.ds(k0, 2 * c), ps]
            lhs = jnp.concatenate([intra_ref[pl.ds(r0, c), ps],
                                   jnp.where(first_k, kd_t, zeros_k),
                                   jnp.where(first_k, zeros_k, kd_t)], axis=0)
            iv = _dot(lhs, v_diag)
            o = ws_qs[c:] + iv[:c]
            eg = egl_ref[pl.ds(e0, SUBLANES), ps]
            s_dec = (s_p.reshape(DN_HEAD_DIM // SUBLANES, SUBLANES, pw) * eg[None]
                     ).reshape(DN_HEAD_DIM, pw)
            s_ref[p] = s_dec + iv[c:c + DN_HEAD_DIM] + iv[c + DN_HEAD_DIM:]
            z = z_ref[pl.ds(r0, c), ps].astype(F32)
            gate = z * _sigmoid(z)
            for hh in range(2):
                ls = slice(hh * LANES, (hh + 1) * LANES)
                o_h = o[:, ls]
                o_h = o_h * lax.rsqrt(jnp.mean(o_h * o_h, axis=-1, keepdims=True) + RMS_EPS)
                o_ref[pl.ds(r0, c), p * pw + hh * LANES:p * pw + (hh + 1) * LANES] = (
                    o_h * nw * gate[:, ls]).astype(o_ref.dtype)
        return carry

    lax.fori_loop(0, n_chunks, chunk, 0)


def _dn_scan(u, w, qe, kd, intra, egl, proj, norm_w, t):
    rows = min(512, t)
    full = pl.BlockSpec((rows, DN_WIDTH), lambda i: (i, 0))
    return pl.pallas_call(
        _dn_scan_kernel,
        grid=(t // rows,),
        in_specs=[full, full, full, pl.BlockSpec((2 * rows, DN_WIDTH), lambda i: (i, 0)), full,
                  pl.BlockSpec((rows // DN_CHUNK * SUBLANES, DN_WIDTH), lambda i: (i, 0)),
                  pl.BlockSpec((rows, DN_WIDTH), lambda i: (i, 3)),
                  pl.BlockSpec((1, LANES), lambda i: (0, 0))],
        out_specs=full,
        out_shape=jax.ShapeDtypeStruct((t, DN_WIDTH), BF16),
        scratch_shapes=[pltpu.VMEM((DN_HEADS // 2, DN_HEAD_DIM, 2 * DN_HEAD_DIM), F32)],
        compiler_params=_cparams(("arbitrary",)),
        name="dn_scan",
    )(u, w, qe, kd, intra, egl, proj, norm_w.reshape(1, LANES).astype(F32))


def _attn_kernel(*refs, dilation, span, n_earlier):
    q_ref, k_ref, v_ref, cos_ref, sin_ref = refs[:5]
    earlier = refs[5:5 + 2 * n_earlier]
    out_refs = refs[5 + 2 * n_earlier:-5]
    kbuf, vbuf, qbuf, obuf, lbuf = refs[-5:]
    i = pl.program_id(1)
    bt, width = q_ref.shape
    d = dilation
    blk = ATT_BLOCK
    prev = d * blk
    half = ATT_HEAD_DIM // 2
    scale = ATT_HEAD_DIM ** -0.5

    heads = width // LANES

    @pl.when(i == 0)
    def _():
        kbuf[:, 0:prev, :] = jnp.zeros((heads, prev, LANES), F32)
        vbuf[:, 0:prev, :] = jnp.zeros((heads, prev, LANES), F32)

    cos, sin = cos_ref[...], sin_ref[...]

    def rope(a):
        a = a.astype(F32)
        return a * cos + pltpu.roll(a, half, 1) * sin

    for h in range(heads):
        ls = slice(h * LANES, (h + 1) * LANES)
        qbuf[h] = rope(q_ref[:, ls]) * scale
        kbuf[h, prev:prev + bt, :] = rope(k_ref[:, ls])
        vbuf[h, prev:prev + bt, :] = v_ref[:, ls].astype(F32)

    qi = lax.broadcasted_iota(jnp.int32, (blk, 2 * blk), 0) + blk
    ki = lax.broadcasted_iota(jnp.int32, (blk, 2 * blk), 1)
    dist = qi - ki
    band = (dist >= 0) & (dist <= span)
    band_first = band & ((ki >= blk) | (i > 0))

    def rows(start, size):
        return pl.ds(start, size, stride=d) if d > 1 else pl.ds(start, size)

    for j in range(bt // prev):
        for r in range(d):
            r0 = j * prev + r
            for h in range(heads):
                qj = qbuf[h, rows(r0, blk), :].astype(BF16)
                kj = kbuf[h, rows(r0, 2 * blk), :].astype(BF16)
                vj = vbuf[h, rows(r0, 2 * blk), :].astype(BF16)
                s = _dot_nt(qj, kj)
                s = jnp.where(band_first if j == 0 else band, s, NEG_BIG)
                m = jnp.max(s, axis=-1, keepdims=True)
                p = jnp.exp(s - m)
                l = jnp.sum(p, axis=-1, keepdims=True)
                obuf[h, rows(r0, blk), :] = _dot(p.astype(BF16), vj) / l
                lbuf[h, rows(r0, blk), :] = jnp.broadcast_to(m + jnp.log(l), (blk, LANES))

    for h in range(heads):
        ls = slice(h * LANES, (h + 1) * LANES)
        if n_earlier == 0:
            o_ref, lse_ref = out_refs
            o_ref[:, ls] = obuf[h].astype(o_ref.dtype)
            lse_ref[:, ls] = lbuf[h]
        else:
            (o_ref,) = out_refs
            outs = [earlier[2 * g][:, ls].astype(F32) for g in range(n_earlier)] + [obuf[h]]
            lses = [earlier[2 * g + 1][:, ls] for g in range(n_earlier)] + [lbuf[h]]
            m = functools.reduce(jnp.maximum, lses)
            es = [jnp.exp(l - m) for l in lses]
            num = functools.reduce(lambda a, b: a + b, [e * o for e, o in zip(es, outs)])
            den = functools.reduce(lambda a, b: a + b, es)
            o_ref[:, ls] = (num / den).astype(o_ref.dtype)
    kbuf[:, 0:prev, :] = kbuf[:, bt:bt + prev, :]
    vbuf[:, 0:prev, :] = vbuf[:, bt:bt + prev, :]


def _attn_group(proj, cos_t, sin_t, gi, dilation, span, t, earlier=()):
    prev = dilation * ATT_BLOCK
    bt = max(min(512, t), prev)
    width = (4 if prev <= 512 else 2) * LANES
    gw = ATT_GROUP_WIDTH
    per_g = gw // width
    q0 = gi * gw // width
    k0 = q0 + ATT_WIDTH // width
    v0 = k0 + ATT_WIDTH // width

    def cur(c0):
        return pl.BlockSpec((bt, width), lambda hg, i, c0=c0: (i, c0 + hg))

    tab = pl.BlockSpec((bt, LANES), lambda hg, i: (i, 0))
    out = pl.BlockSpec((bt, width), lambda hg, i: (i, hg))
    flat_earlier = [a for pair in earlier for a in pair]
    if earlier:
        out_specs, out_shape = [out], [jax.ShapeDtypeStruct((t, gw), BF16)]
    else:
        out_specs = [out, out]
        out_shape = [jax.ShapeDtypeStruct((t, gw), BF16), jax.ShapeDtypeStruct((t, gw), F32)]
    return pl.pallas_call(
        functools.partial(_attn_kernel, dilation=dilation, span=span, n_earlier=len(earlier)),
        grid=(per_g, t // bt),
        in_specs=[cur(q0), cur(k0), cur(v0), tab, tab] + [out] * len(flat_earlier),
        out_specs=out_specs,
        out_shape=out_shape,
        scratch_shapes=[pltpu.VMEM((width // LANES, prev + bt, LANES), F32),
                        pltpu.VMEM((width // LANES, prev + bt, LANES), F32),
                        pltpu.VMEM((width // LANES, bt, LANES), F32),
                        pltpu.VMEM((width // LANES, bt, LANES), F32),
                        pltpu.VMEM((width // LANES, bt, LANES), F32)],
        compiler_params=_cparams(("parallel", "arbitrary")),
        name=f"attn_g{gi}",
    )(proj, proj, proj, cos_t, sin_t, *flat_earlier)


def _attention(proj, cos_t, sin_t, t):
    earlier = []
    for gi, (window, dilation) in enumerate(ATT_GROUPS[:-1]):
        earlier.append(_attn_group(proj, cos_t, sin_t, gi, dilation, window // dilation, t))
    window, dilation = ATT_GROUPS[-1]
    (o_att,) = _attn_group(proj, cos_t, sin_t, len(ATT_GROUPS) - 1, dilation, window // dilation, t,
                           earlier=tuple(earlier))
    return o_att


def _mix_kernel(odn_ref, oatt_ref, wdn_ref, watt_ref, gdn_ref, gatt_ref, out_ref):
    y_dn = _dot(odn_ref[...], wdn_ref[...])
    y_att = _dot(oatt_ref[...], watt_ref[...])
    merged = (_sigmoid(gdn_ref[...].astype(F32)) * y_dn
              + _sigmoid(gatt_ref[...].astype(F32)) * y_att)
    out_ref[...] = merged.astype(out_ref.dtype)


def _mix(o_dn, o_att, w_dn, w_att, proj, t, d):
    tm = min(1024, t)
    tn = ATT_GROUP_WIDTH
    g0 = 3 * ATT_WIDTH // tn
    row_dn = pl.BlockSpec((tm, DN_WIDTH), lambda j, i: (i, 0))
    row_g = pl.BlockSpec((tm, ATT_GROUP_WIDTH), lambda j, i: (i, 0))
    return pl.pallas_call(
        _mix_kernel,
        grid=(d // tn, t // tm),
        in_specs=[row_dn, row_g,
                  pl.BlockSpec((DN_WIDTH, tn), lambda j, i: (0, j)),
                  pl.BlockSpec((ATT_GROUP_WIDTH, tn), lambda j, i: (0, j)),
                  pl.BlockSpec((tm, tn), lambda j, i: (i, g0 + j)),
                  pl.BlockSpec((tm, tn), lambda j, i: (i, g0 + d // tn + j))],
        out_specs=pl.BlockSpec((tm, tn), lambda j, i: (i, j)),
        out_shape=jax.ShapeDtypeStruct((t, d), BF16),
        compiler_params=_cparams(("parallel", "parallel")),
        name="mix",
    )(o_dn, o_att, w_dn, w_att, proj, proj)


def _layer_norm(y, g, b):
    mu = jnp.mean(y, axis=-1, keepdims=True)
    yc = y - mu
    var = jnp.mean(yc * yc, axis=-1, keepdims=True)
    return yc * lax.rsqrt(var + LN_EPS) * g + b


def _wo_kernel(m_ref, w_ref, x_ref, y_ref, *, alpha):
    y_ref[...] = alpha * x_ref[...] + _dot(m_ref[...], w_ref[...])


def _wo_residual(merged, w_o, x2, t, d, alpha):
    tm, tn = min(1024, t), min(1024, d)
    tile = pl.BlockSpec((tm, tn), lambda i, j: (i, j))
    return pl.pallas_call(
        functools.partial(_wo_kernel, alpha=alpha),
        grid=(t // tm, d // tn),
        in_specs=[pl.BlockSpec((tm, d), lambda i, j: (i, 0)),
                  pl.BlockSpec((d, tn), lambda i, j: (0, j)),
                  tile],
        out_specs=tile,
        out_shape=jax.ShapeDtypeStruct((t, d), F32),
        compiler_params=_cparams(("parallel", "parallel")),
        name="wo",
    )(merged, w_o, x2)


def _router_kernel(y_ref, g_ref, b_ref, w_ref, bias_ref,
                   x1_ref, x1p_ref, topi_ref, topw_ref, rank_ref, cnt_ref, run_ref):
    i = pl.program_id(0)
    tm, d = y_ref.shape

    @pl.when(i == 0)
    def _():
        run_ref[...] = jnp.zeros_like(run_ref)

    x1 = _layer_norm(y_ref[...], g_ref[...], b_ref[...])
    x1_ref[...] = x1
    x1p_ref[...] = _pack_halves(x1[:, :d // 2], x1[:, d // 2:])

    x_hi, x_mid = _split2(x1)
    w_hi, w_mid = _split2(w_ref[...])
    logits = _dot(x_hi, w_hi) + _dot(x_hi, w_mid) + _dot(x_mid, w_hi)
    s = _sigmoid(logits)
    sc = s + bias_ref[...]
    lane_i = lax.broadcasted_iota(jnp.int32, (tm, N_EXPERTS), 1)
    grp = lane_i // GROUP_SIZE
    lane = lane_i.astype(F32)
    neg = -jnp.inf

    def first_argmax(v):
        m = jnp.max(v, axis=-1, keepdims=True)
        idx = jnp.min(jnp.where(v == m, lane, float(N_EXPERTS)), axis=-1, keepdims=True)
        return m, idx

    gscore = []
    for gi in range(N_GROUPS):
        vg = jnp.where(grp == gi, sc, neg)
        m1, i1 = first_argmax(vg)
        m2 = jnp.max(jnp.where(lane == i1, neg, vg), axis=-1, keepdims=True)
        gscore.append(m1 + m2)
    emask = jnp.zeros((tm, N_EXPERTS), jnp.bool_)
    for gi in range(N_GROUPS):
        ahead = jnp.zeros((tm, 1), jnp.int32)
        for gj in range(N_GROUPS):
            if gj == gi:
                continue
            beats = (gscore[gj] > gscore[gi]) | ((gscore[gj] == gscore[gi]) & (gj < gi))
            ahead = ahead + beats.astype(jnp.int32)
        emask = emask | ((grp == gi) & (ahead < TOPK_GROUPS))
    masked = jnp.where(emask, sc, neg)

    sel =jnp.zeros((tm, N_EXPERTS), jnp.bool_)
    idxs, vals = [], []
    for _ in range(TOP_K):
        _, ik = first_argmax(masked)
        hit = lane == ik
        sel = sel | hit
        masked = jnp.where(hit, neg, masked)
        idxs.append(ik)
        vals.append(jnp.sum(jnp.where(hit, s, 0.0), axis=-1, keepdims=True))
    wsum = vals[0]
    for v in vals[1:]:
        wsum = wsum + v

    sel_b = jnp.where(sel, 1.0, 0.0).astype(BF16)
    strict = jnp.where(lax.broadcasted_iota(jnp.int32, (tm, tm), 1)
                       < lax.broadcasted_iota(jnp.int32, (tm, tm), 0), 1.0, 0.0).astype(BF16)
    rank_excl = run_ref[0:1, :] + _dot(strict, sel_b)
    run_new = run_ref[0:1, :] + jnp.sum(sel_b.astype(F32), axis=0, keepdims=True)
    run_ref[...] = jnp.broadcast_to(run_new, run_ref.shape)
    cnt_ref[...] = jnp.broadcast_to(run_new, cnt_ref.shape).astype(jnp.int32)

    topi = jnp.zeros((tm, N_EXPERTS), F32)
    topw = jnp.zeros((tm, N_EXPERTS), F32)
    rnk = jnp.zeros((tm, N_EXPERTS), F32)
    for kk in range(TOP_K):
        hit = lane == idxs[kk]
        rk = jnp.sum(jnp.where(hit, rank_excl, 0.0), axis=-1, keepdims=True)
        topi = jnp.where(lane == kk, idxs[kk], topi)
        topw = jnp.where(lane == kk, vals[kk] / wsum * ROUTED_SCALE, topw)
        rnk = jnp.where(lane == kk, rk, rnk)
    topi_ref[...] = topi.astype(jnp.int32)
    topw_ref[...] = topw
    rank_ref[...] = rnk.astype(jnp.int32)


def _ln_router(y, g, b, w_router, bias, t, d):
    tm = min(256, t)
    row = pl.BlockSpec((tm, N_EXPERTS), lambda i: (i, 0))
    full = pl.BlockSpec((tm, d), lambda i: (i, 0))
    vec = pl.BlockSpec((1, d), lambda i: (0, 0))
    return pl.pallas_call(
        _router_kernel,
        grid=(t // tm,),
        in_specs=[full, vec, vec,
                  pl.BlockSpec((d, N_EXPERTS), lambda i: (0, 0)),
                  pl.BlockSpec((1, N_EXPERTS), lambda i: (0, 0))],
        out_specs=[full, pl.BlockSpec((tm, d // 2), lambda i: (i, 0)), row, row, row,
                   pl.BlockSpec((SUBLANES, N_EXPERTS), lambda i: (0, 0))],
        out_shape=[jax.ShapeDtypeStruct((t, d), F32),
                   jax.ShapeDtypeStruct((t, d // 2), jnp.uint32),
                   jax.ShapeDtypeStruct((t, N_EXPERTS), jnp.int32),
                   jax.ShapeDtypeStruct((t, N_EXPERTS), F32),
                   jax.ShapeDtypeStruct((t, N_EXPERTS), jnp.int32),
                   jax.ShapeDtypeStruct((SUBLANES, N_EXPERTS), jnp.int32)],
        scratch_shapes=[pltpu.VMEM((SUBLANES, N_EXPERTS), F32)],
        compiler_params=_cparams(("arbitrary",)),
        name="ln_router",
    )(y, g.reshape(1, d).astype(F32), b.reshape(1, d).astype(F32), w_router.astype(F32),
      bias.reshape(1, N_EXPERTS).astype(F32))


def _slot_kernel(topi_ref, rank_ref, start_ref, pos_ref):
    topi = topi_ref[...]
    lane = lax.broadcasted_iota(jnp.int32, topi.shape, 1)
    start = jnp.broadcast_to(start_ref[...], topi.shape)
    pos = rank_ref[...]
    for kk in range(TOP_K):
        seg = jnp.sum(jnp.where(lane == topi[:, kk:kk + 1], start, 0.0), axis=-1, keepdims=True)
        pos = jnp.where(lane == kk, pos + seg.astype(jnp.int32), pos)
    pos_ref[...] = pos


def _slots(topi, rank, pad_start, t):
    tm = min(1024, t)
    row = pl.BlockSpec((tm, N_EXPERTS), lambda i: (i, 0))
    return pl.pallas_call(
        _slot_kernel,
        grid=(t // tm,),
        in_specs=[row, row, pl.BlockSpec((1, N_EXPERTS), lambda i: (0, 0))],
        out_specs=row,
        out_shape=jax.ShapeDtypeStruct((t, N_EXPERTS), jnp.int32),
        compiler_params=_cparams(("parallel",)),
        name="slots",
    )(topi, rank, pad_start.astype(F32).reshape(1, N_EXPERTS))


def _row_copy(src_ref, src_row, dst_ref, dst_row, sem):
    return pltpu.make_async_copy(src_ref.at[pl.ds(src_row, 1)], dst_ref.at[pl.ds(dst_row, 1)], sem)


def _dispatch_kernel(pos_ref, x_ref, xs_ref, sem):
    i = pl.program_id(0)
    tb = x_ref.shape[0]
    base = i * (tb * TOP_K)

    def start(r, carry):
        for kk in range(TOP_K):
            _row_copy(x_ref, r, xs_ref, pos_ref[base + r * TOP_K + kk], sem).start()
        return carry

    lax.fori_loop(0, tb, start, 0)
    for kk in range(TOP_K):
        pltpu.make_async_copy(x_ref, xs_ref.at[pl.ds(0, tb)], sem).wait()


def _dispatch(pos_flat, x1, n_slots, t, d):
    tb = min(256, t)
    return pl.pallas_call(
        _dispatch_kernel,
        grid_spec=pltpu.PrefetchScalarGridSpec(
            num_scalar_prefetch=1,
            grid=(t // tb,),
            in_specs=[pl.BlockSpec((tb, d), lambda i, pos: (i, 0))],
            out_specs=pl.BlockSpec(memory_space=pl.ANY),
            scratch_shapes=[pltpu.SemaphoreType.DMA],
        ),
        out_shape=jax.ShapeDtypeStruct((n_slots, d), x1.dtype),
        compiler_params=_cparams(("arbitrary",)),
        name="dispatch",
    )(pos_flat, x1)


def _expert_kernel(be_ref, nv_ref, first_ref, nxt_ref, ord_ref, nu_ref, xs_ref, wgu_hbm, wdn_hbm,
                   y_ref, gu_stage, dn_stage, gu_cache, dn_cache, act_ref, sem):
    b = pl.program_id(0)
    n_used = nu_ref[0]

    def weight_copies(e):
        return (pltpu.make_async_copy(wgu_hbm.at[e], gu_stage, sem.at[0]),
                pltpu.make_async_copy(wdn_hbm.at[e], dn_stage, sem.at[1]))

    @pl.when(b == 0)
    def _():
        act_ref[...] = jnp.zeros_like(act_ref)

    @pl.when(b <= n_used)
    def _():
        rows, hd = xs_ref.shape
        kc = min(EXPERT_CHUNK, hd)
        cur = jnp.minimum(b, n_used - 1)
        e = be_ref[cur]

        @pl.when((first_ref[cur] == 1) & (b < n_used))
        def _():
            @pl.when(b == 0)
            def _():
                for cp in weight_copies(e):
                    cp.start()

            for cp in weight_copies(e):
                cp.wait()
            for c0 in range(0, 2 * hd, kc):
                for j in range(EXPERT_FF // LANES):
                    gu_cache[c0:c0 + kc, 2 * j * LANES:(2 * j + 1) * LANES] = (
                        gu_stage[c0:c0 + kc, j * LANES:(j + 1) * LANES].astype(BF16))
                    gu_cache[c0:c0 + kc, (2 * j + 1) * LANES:(2 * j + 2) * LANES] = (
                        gu_stage[c0:c0 + kc, EXPERT_FF + j * LANES:EXPERT_FF + (j + 1) * LANES]
                        .astype(BF16))
                dn_cache[ord_ref[cur] % 2, :, c0:c0 + kc] = dn_stage[:, c0:c0 + kc].astype(BF16)

            @pl.when(nxt_ref[cur] >= 0)
            def _():
                for cp in weight_copies(nxt_ref[cur]):
                    cp.start()

        prev_slot = ord_ref[jnp.maximum(b - 1, 0)] % 2
        act_prev = act_ref[...]
        nc = min(EXPERT_OUT_CHUNK, hd)
        for c0 in range(0, hd, nc):
            y_ref[:, c0:c0 + nc] = _pack_halves(
                _dot(act_prev, dn_cache[prev_slot, :, c0:c0 + nc]),
                _dot(act_prev, dn_cache[prev_slot, :, hd + c0:hd + c0 + nc]))

        rid = lax.broadcasted_iota(jnp.int32, (rows, 1), 0)
        valid = rid < nv_ref[cur]
        lo, hi = _unpack_halves(xs_ref[...])
        x_lo = jnp.where(valid, lo, 0.0).astype(BF16)
        x_hi = jnp.where(valid, hi, 0.0).astype(BF16)
        for j in range(EXPERT_FF // LANES):
            cs = slice(2 * j * LANES, (2 * j + 2) * LANES)
            hgu = _dot(x_lo, gu_cache[:hd, cs]) + _dot(x_hi, gu_cache[hd:, cs])
            gate, up = hgu[:, :LANES], hgu[:, LANES:]
            act_ref[:, j * LANES:(j + 1) * LANES] = (gate * _sigmoid(gate) * up).astype(BF16)


def _experts(blk_expert, blk_valid, blk_first, blk_next, blk_ord, n_used, xs, w_gu, w_dn, d):
    n_slots = xs.shape[0]
    nb = n_slots // MOE_ROWS

    def in_map(b, be, nv, fi, nx, od, nu):
        return (jnp.minimum(b, nu[0] - 1), 0)

    def out_map(b, be, nv, fi, nx, od, nu):
        return (jnp.minimum(jnp.maximum(b - 1, 0), nu[0] - 1), 0)

    return pl.pallas_call(
        _expert_kernel,
        grid_spec=pltpu.PrefetchScalarGridSpec(
            num_scalar_prefetch=6,
            grid=(nb + 1,),
            in_specs=[pl.BlockSpec((MOE_ROWS, d // 2), in_map),
                      pl.BlockSpec(memory_space=pl.ANY),
                      pl.BlockSpec(memory_space=pl.ANY)],
            out_specs=pl.BlockSpec((MOE_ROWS, d // 2), out_map),
            scratch_shapes=[pltpu.VMEM((d, 2 * EXPERT_FF), w_gu.dtype),
                            pltpu.VMEM((EXPERT_FF, d), w_dn.dtype),
                            pltpu.VMEM((d, 2 * EXPERT_FF), BF16),
                            pltpu.VMEM((2, EXPERT_FF, d), BF16),
                            pltpu.VMEM((MOE_ROWS, EXPERT_FF), BF16),
                            pltpu.SemaphoreType.DMA((2,))],
        ),
        out_shape=jax.ShapeDtypeStruct((n_slots, d // 2), jnp.uint32),
        compiler_params=_cparams(("arbitrary",)),
        name="experts",
    )(blk_expert, blk_valid, blk_first, blk_next, blk_ord, n_used, xs, w_gu, w_dn)


def _shared_kernel(x_ref, wgu_ref, wdn_ref, y_ref, act_ref):
    i = pl.program_id(0)

    @pl.when(i == 0)
    def _():
        act_ref[...] = jnp.zeros_like(act_ref)

    y_ref[...] = _dot(act_ref[...], wdn_ref[...]).astype(y_ref.dtype)
    hd = x_ref.shape[1]
    lo, hi = _unpack_halves(x_ref[...])
    hgu = _dot(lo.astype(BF16), wgu_ref[:hd, :]) + _dot(hi.astype(BF16), wgu_ref[hd:, :])
    gate, up = hgu[:, :EXPERT_FF], hgu[:, EXPERT_FF:]
    act_ref[...] = (gate * _sigmoid(gate) * up).astype(BF16)


def _shared(x1p, w_gu, w_dn, t, d):
    tm = min(512, t)
    n = t // tm
    return pl.pallas_call(
        _shared_kernel,
        grid=(n + 1,),
        in_specs=[pl.BlockSpec((tm, d // 2), lambda i: (jnp.minimum(i, n - 1), 0)),
                  pl.BlockSpec((d, 2 * EXPERT_FF), lambda i: (0, 0)),
                  pl.BlockSpec((EXPERT_FF, d), lambda i: (0, 0))],
        out_specs=pl.BlockSpec((tm, d), lambda i: (jnp.maximum(i - 1, 0), 0)),
        out_shape=jax.ShapeDtypeStruct((t, d), BF16),
        scratch_shapes=[pltpu.VMEM((tm, EXPERT_FF), BF16)],
        compiler_params=_cparams(("arbitrary",)),
        name="shared",
    )(x1p, w_gu, w_dn)


def _combine_kernel(pos_ref, x_ref, ysh_ref, topw_ref, g_ref, b_ref, y_hbm, out_ref, buf_ref, sem,
                    *, alpha):
    i = pl.program_id(0)
    n = pl.num_programs(0)
    tb = x_ref.shape[0]

    def gather(tile, slot):
        base = tile * (tb * TOP_K)

        def body(r, carry):
            for kk in range(TOP_K):
                _row_copy(y_hbm, pos_ref[base + r * TOP_K + kk], buf_ref.at[slot, kk], r,
                          sem.at[slot]).start()
            return carry

        lax.fori_loop(0, tb, body, 0)

    @pl.when(i == 0)
    def _():
        gather(0, 0)

    @pl.when(i + 1 < n)
    def _():
        gather(i + 1, (i + 1) % 2)

    slot = i % 2
    for kk in range(TOP_K):
        pltpu.make_async_copy(y_hbm.at[pl.ds(0, tb)], buf_ref.at[slot, kk], sem.at[slot]).wait()
    topw = topw_ref[...]
    hd = x_ref.shape[1] // 2
    acc_lo = alpha * x_ref[:, :hd] + ysh_ref[:, :hd].astype(F32)
    acc_hi = alpha * x_ref[:, hd:] + ysh_ref[:, hd:].astype(F32)
    for kk in range(TOP_K):
        lo, hi = _unpack_halves(buf_ref[slot, kk])
        acc_lo = acc_lo + topw[:, kk:kk + 1] * lo
        acc_hi = acc_hi + topw[:, kk:kk + 1] * hi
    inv_d = 1.0 / (2 * hd)
    mu = (jnp.sum(acc_lo, axis=-1, keepdims=True) + jnp.sum(acc_hi, axis=-1, keepdims=True)) * inv_d
    c_lo, c_hi = acc_lo - mu, acc_hi - mu
    var = (jnp.sum(c_lo * c_lo, axis=-1, keepdims=True)
           + jnp.sum(c_hi * c_hi, axis=-1, keepdims=True)) * inv_d
    rstd = lax.rsqrt(var + LN_EPS)
    out_ref[:, :hd] = c_lo * rstd * g_ref[:, :hd] + b_ref[:, :hd]
    out_ref[:, hd:] = c_hi * rstd * g_ref[:, hd:] + b_ref[:, hd:]


def _combine(pos_flat, x1, ysh, topw, g, b, y, t, d, alpha):
    tb = min(128, t)
    row = pl.BlockSpec((tb, d), lambda i, pos: (i, 0))
    vec = pl.BlockSpec((1, d), lambda i, pos: (0, 0))
    return pl.pallas_call(
        functools.partial(_combine_kernel, alpha=alpha),
        grid_spec=pltpu.PrefetchScalarGridSpec(
            num_scalar_prefetch=1,
            grid=(t // tb,),
            in_specs=[row, row, pl.BlockSpec((tb, N_EXPERTS), lambda i, pos: (i, 0)), vec, vec,
                      pl.BlockSpec(memory_space=pl.ANY)],
            out_specs=row,
            scratch_shapes=[pltpu.VMEM((2, TOP_K, tb, d // 2), jnp.uint32),
                            pltpu.SemaphoreType.DMA((2,))],
        ),
        out_shape=jax.ShapeDtypeStruct((t, d), F32),
        compiler_params=_cparams(("arbitrary",)),
        name="combine",
    )(pos_flat, x1, ysh, topw, g.reshape(1, d).astype(F32), b.reshape(1, d).astype(F32), y)


def _rope_tables(t):
    half = ATT_HEAD_DIM // 2
    inv_freq = ROPE_THETA ** (-jnp.arange(half, dtype=F32) / half)
    ang = jnp.arange(t, dtype=F32)[:, None] * inv_freq[None, :]
    cos, sin = jnp.cos(ang), jnp.sin(ang)
    return jnp.concatenate([cos, cos], -1), jnp.concatenate([-sin, sin], -1)


def _token_mixer(x2, w_in, conv_w, a_log, dt_bias, dn_norm_w, w_dn_branch, w_att_branch, w_o,
                 alpha):
    t, d = x2.shape
    n_main = 4 * DN_WIDTH
    n_ba = 2 * DN_HEADS
    w_t = w_in.T
    betab, gcb, gct, x_b = _dn_gates(x2, w_t, n_main, a_log, dt_bias)
    proj_dn = _proj(x_b, w_t, 0, n_main, "proj_dn")
    proj_att = _proj(x_b, w_t, n_main + n_ba, w_t.shape[0] - n_main - n_ba, "proj_att")
    u, w, qe, kd, intra, egl = _dn_prep(proj_dn, conv_w.astype(F32), betab, gcb, gct, t)
    o_dn = _dn_scan(u, w, qe, kd, intra, egl, proj_dn, dn_norm_w, t)

    cos_t, sin_t = _rope_tables(t)
    o_att = _attention(proj_att, cos_t, sin_t, t)
    merged = _mix(o_dn, o_att, w_dn_branch.astype(BF16), w_att_branch.astype(BF16), proj_att, t, d)
    return _wo_residual(merged, w_o.astype(BF16), x2, t, d, alpha)


def _moe(y_mix, ln1_g, ln1_b, w_router, router_bias, w_exp_gate_up, w_exp_down, w_sh_gate_up,
         w_sh_down, ln_g, ln_b, alpha):
    t, d = y_mix.shape
    r = MOE_ROWS
    x1, x1p, topi, topw, rank, cnt = _ln_router(y_mix, ln1_g, ln1_b, w_router, router_bias, t, d)
    counts = cnt[0]
    padded = (counts + r - 1) // r * r
    pad_end = jnp.cumsum(padded)
    pad_start = pad_end - padded
    nb = (t * TOP_K) // r + N_EXPERTS
    blk_row0 = jnp.arange(nb, dtype=jnp.int32) * r
    in_blk = (blk_row0[:, None] >= pad_start[None, :]) & (blk_row0[:, None] < pad_end[None, :])
    expert_ids = jnp.arange(N_EXPERTS, dtype=jnp.int32)
    blk_expert = jnp.where(blk_row0 < pad_end[-1],
                           jnp.sum(jnp.where(in_blk, expert_ids[None, :], 0), axis=1),
                           N_EXPERTS - 1).astype(jnp.int32)
    blk_valid = jnp.sum(jnp.where(in_blk, jnp.clip(counts[None, :] - (blk_row0[:, None] - pad_start[None, :]), 0, r), 0),
                        axis=1).astype(jnp.int32)
    n_used = (pad_end[-1:] // r).astype(jnp.int32)
    blk_ids = jnp.arange(nb, dtype=jnp.int32)
    prev_expert = jnp.concatenate([jnp.full((1,), -1, jnp.int32), blk_expert[:-1]])
    blk_first = (blk_expert != prev_expert).astype(jnp.int32)
    blk_ord = jnp.cumsum(blk_first) - 1
    after = jnp.sum(jnp.where(in_blk, pad_end[None, :] // r, 0), axis=1)
    after_expert = jnp.sum(jnp.where(after[:, None] == blk_ids[None, :], blk_expert[None, :], 0), axis=1)
    blk_next = jnp.where(after < n_used[0], after_expert, -1).astype(jnp.int32)
    pos_flat = _slots(topi, rank, pad_start, t)[:, :TOP_K].reshape(-1)

    xs = _dispatch(pos_flat, x1p, nb * r, t, d // 2)
    y = _experts(blk_expert, blk_valid, blk_first, blk_next, blk_ord.astype(jnp.int32), n_used, xs,
                 w_exp_gate_up, w_exp_down, d)
    ysh = _shared(x1p, w_sh_gate_up.astype(BF16), w_sh_down.astype(BF16), t, d)
    return _combine(pos_flat, x1, ysh, topw, ln_g, ln_b, y, t, d, alpha)


def kernel(x, w_in, conv_w, a_log, dt_bias, dn_norm_w, w_dn_branch, w_att_branch, w_o, ln1_g, ln1_b,
           w_router, router_bias, w_exp_gate_up, w_exp_down, w_sh_gate_up, w_sh_down, ln2_g, ln2_b):
    bsz, t, d = x.shape
    depth = w_in.shape[0]
    alpha = (2.0 * depth) ** 0.25
    outs = []
    for bi in range(bsz):
        xb = x[bi]
        for l in range(depth):
            y_mix = _token_mixer(xb, w_in[l], conv_w[l], a_log[l], dt_bias[l], dn_norm_w[l],
                                 w_dn_branch[l], w_att_branch[l], w_o[l], alpha)
            xb = _moe(y_mix, ln1_g[l], ln1_b[l], w_router[l], router_bias[l], w_exp_gate_up[l],
                      w_exp_down[l], w_sh_gate_up[l], w_sh_down[l], ln2_g[l], ln2_b[l], alpha)
        outs.append(xb)
    return jnp.stack(outs, 0)
```

```python
import functools

import jax
import jax.numpy as jnp
from jax import lax
from jax.experimental import pallas as pl
from jax.experimental.pallas import tpu as pltpu

F32 = jnp.float32
BF16 = jnp.bfloat16

DN_HEADS = 16
DN_HEAD_DIM = 128
DN_WIDTH = DN_HEADS * DN_HEAD_DIM
DN_CONV = 4
DN_CHUNK = 64
ATT_GROUPS = ((128, 1), (512, 4), (2048, 16))
ATT_HEADS_PER_GROUP = 8
ATT_HEAD_DIM = 128
ATT_GROUP_WIDTH = ATT_HEADS_PER_GROUP * ATT_HEAD_DIM
ATT_WIDTH = len(ATT_GROUPS) * ATT_GROUP_WIDTH
ATT_BLOCK = 128
ROPE_THETA = 10000.0
N_EXPERTS = 128
TOP_K = 8
N_GROUPS = 8
GROUP_SIZE = N_EXPERTS // N_GROUPS
TOPK_GROUPS = 4
EXPERT_FF = 384
ROUTED_SCALE = 2.5
LN_EPS = 1e-5
RMS_EPS = 1e-6
L2_EPS = 1e-6

LANES = 128
SUBLANES = 8
BF16_SUBLANES = 16
VMEM_LIMIT = 56 * 1024 * 1024

DN_PREP_ROWS = 1024
MOE_ROWS = 256
EXPERT_CHUNK = 1024
EXPERT_OUT_CHUNK = 512
NEG_BIG = -1e30


def _cparams(sem, vmem=VMEM_LIMIT):
    return pltpu.CompilerParams(dimension_semantics=sem, vmem_limit_bytes=vmem)


def _split3(a):
    hi = a.astype(BF16)
    r1 = a - hi.astype(F32)
    mid = r1.astype(BF16)
    lo = (r1 - mid.astype(F32)).astype(BF16)
    return hi, mid, lo


def _split2(a):
    hi = a.astype(BF16)
    return hi, (a - hi.astype(F32)).astype(BF16)


def _dot(a, b):
    return jnp.dot(a, b, preferred_element_type=F32)


def _dot_nt(a, b):
    return lax.dot_general(a, b, (((1,), (1,)), ((), ())), preferred_element_type=F32)


def _dot_tn(a, b):
    return lax.dot_general(a, b, (((0,), (0,)), ((), ())), preferred_element_type=F32)


def _sigmoid(x):
    return 1.0 / (1.0 + jnp.exp(-x))


def _pack_halves(lo, hi):
    lo_b = lax.bitcast_convert_type(lo.astype(BF16).astype(F32), jnp.uint32)
    hi_b = lax.bitcast_convert_type(hi.astype(BF16).astype(F32), jnp.uint32)
    return (hi_b & jnp.uint32(0xFFFF0000)) | (lo_b >> 16)


def _unpack_halves(u):
    lo = lax.bitcast_convert_type(u << 16, F32)
    hi = lax.bitcast_convert_type(u & jnp.uint32(0xFFFF0000), F32)
    return lo, hi


def _proj_kernel(x_ref, wt_ref, o_ref):
    o_ref[...] = _dot_nt(x_ref[...], wt_ref[...].astype(BF16)).astype(o_ref.dtype)


def _proj(x_b, w_t, row0, n, name):
    m, k = x_b.shape
    tm, tn = min(1024, m), min(512, n)
    return pl.pallas_call(
        _proj_kernel,
        grid=(m // tm, n // tn),
        in_specs=[pl.BlockSpec((tm, k), lambda i, j: (i, 0)),
                  pl.BlockSpec((pl.Element(tn), pl.Element(k)),
                               lambda i, j: (pl.multiple_of(row0 + j * tn, SUBLANES), 0))],
        out_specs=pl.BlockSpec((tm, tn), lambda i, j: (i, j)),
        out_shape=jax.ShapeDtypeStruct((m, n), BF16),
        compiler_params=_cparams(("parallel", "parallel")),
        name=name,
    )(x_b, w_t)


def _gates_kernel(x_ref, w_ref, prm_ref, betab_ref, gcb_ref, gct_ref, xb_ref):
    tm = x_ref.shape[0]
    x_hi, x_mid = _split2(x_ref[...])
    xb_ref[...] = x_hi
    w_hi, w_mid = _split2(w_ref[...])
    logits = _dot_nt(x_hi, w_hi) + _dot_nt(x_hi, w_mid) + _dot_nt(x_mid, w_hi)

    def softplus(v):
        return jnp.maximum(v, 0.0) + jnp.log(1.0 + jnp.exp(-jnp.abs(v)))

    prm = prm_ref[...]
    beta = _sigmoid(logits)
    g = prm[0:1, :] * softplus(logits + prm[1:2, :])

    ri = lax.broadcasted_iota(jnp.int32, (tm, tm), 0)
    ci = lax.broadcasted_iota(jnp.int32, (tm, tm), 1)
    same = (ri // DN_CHUNK) == (ci // DN_CHUNK)
    lower = jnp.where(same & (ci <= ri), 1.0, 0.0).astype(BF16)
    gc = jnp.zeros((tm, LANES), F32)
    for part in _split3(g):
        gc = gc + _dot(lower, part)
    gct_ref[...] = gc.T[DN_HEADS:2 * DN_HEADS, :]
    for h in range(DN_HEADS):
        sl = slice(h * LANES, (h + 1) * LANES)
        betab_ref[:, sl] = jnp.broadcast_to(beta[:, h:h + 1], (tm, LANES))
        gcb_ref[:, sl] = jnp.broadcast_to(gc[:, DN_HEADS + h:DN_HEADS + h + 1], (tm, LANES))


def _dn_gates(x2, w_t, col0, a_log, dt_bias):
    t, d = x2.shape
    tm = min(512, t)
    neg_a = -jnp.exp(a_log.astype(F32))
    prm = jnp.zeros((SUBLANES, LANES), F32)
    prm = prm.at[0, DN_HEADS:2 * DN_HEADS].set(neg_a).at[1, DN_HEADS:2 * DN_HEADS].set(dt_bias.astype(F32))
    return pl.pallas_call(
        _gates_kernel,
        grid=(t // tm,),
        in_specs=[pl.BlockSpec((tm, d), lambda i: (i, 0)),
                  pl.BlockSpec((LANES, d), lambda i: (col0 // LANES, 0)),
                  pl.BlockSpec((SUBLANES, LANES), lambda i: (0, 0))],
        out_specs=[pl.BlockSpec((tm, DN_WIDTH), lambda i: (i, 0)),
                   pl.BlockSpec((tm, DN_WIDTH), lambda i: (i, 0)),
                   pl.BlockSpec((DN_HEADS, tm), lambda i: (0, i)),
                   pl.BlockSpec((tm, d), lambda i: (i, 0))],
        out_shape=[jax.ShapeDtypeStruct((t, DN_WIDTH), F32),
                   jax.ShapeDtypeStruct((t, DN_WIDTH), F32),
                   jax.ShapeDtypeStruct((DN_HEADS, t), F32),
                   jax.ShapeDtypeStruct((t, d), BF16)],
        compiler_params=_cparams(("parallel",)),
        name="dn_gates",
    )(x2, w_t.astype(F32), prm)


def _dn_prep_kernel(q_ref, k_ref, v_ref, hq_ref, hk_ref, hv_ref, cq_ref, ck_ref, cv_ref,
                    betab_ref, gcb_ref, gct_ref,
                    u_ref, w_ref, qe_ref, kd_ref, intra_ref, egl_ref, cbuf):
    i = pl.program_id(1)
    rows = q_ref.shape[0]
    c = DN_CHUNK
    halo_rows = hq_ref.shape[0]

    def conv_silu(slot, x_ref, halo_ref, cw_ref):
        cbuf[slot, 0:halo_rows, :] = jnp.where(i > 0, halo_ref[...].astype(F32), 0.0)
        cbuf[slot, halo_rows:halo_rows + rows, :] = x_ref[...].astype(F32)
        cw = cw_ref[...]
        y = jnp.zeros((rows, LANES), F32)
        for j in range(DN_CONV):
            off = halo_rows - (DN_CONV - 1) + j
            y = y + cw[j:j + 1, :] * cbuf[slot, off:off + rows, :]
        return y * _sigmoid(y)

    def l2n(a):
        return a * lax.rsqrt(jnp.sum(a * a, axis=-1, keepdims=True) + L2_EPS)

    q = l2n(conv_silu(0, q_ref, hq_ref, cq_ref)) * (DN_HEAD_DIM ** -0.5)
    k = l2n(conv_silu(1, k_ref, hk_ref, ck_ref))
    v = conv_silu(2, v_ref, hv_ref, cv_ref)
    beta = betab_ref[...]
    gcol = gcb_ref[...]
    grow_all = gct_ref[...]

    ii = lax.broadcasted_iota(jnp.int32, (c, c), 0)
    jj = lax.broadcasted_iota(jnp.int32, (c, c), 1)
    eye = jnp.where(ii == jj, 1.0, 0.0)
    n_chunks = rows // c

    decays, a_mats = [], []
    kbs, egcs = [], []
    for n in range(n_chunks):
        rs = slice(n * c, (n + 1) * c)
        kc = k[rs]
        kb = kc * beta[rs]
        grow = grow_all[:, rs]
        diff = jnp.where(ii >= jj, gcol[rs, :c] - grow, 0.0)
        decay = jnp.where(ii >= jj, jnp.exp(diff), 0.0)
        kk = _dot_nt(kb.astype(BF16), kc.astype(BF16))
        a_mats.append(jnp.where(ii > jj, kk * decay, 0.0))
        decays.append(decay)
        kbs.append(kb)
        egcs.append(jnp.exp(gcol[rs]))

    invs = []
    for n in range(n_chunks):
        invs.append(eye - jnp.where((ii // 2 == jj // 2) & (ii > jj), a_mats[n], 0.0))
    s = 2
    while s < c:
        sel = ((ii // (2 * s)) == (jj // (2 * s))) & ((ii // s) > (jj // s))
        for n in range(n_chunks):
            d_b = invs[n].astype(BF16)
            x_b = jnp.where(sel, a_mats[n], 0.0).astype(BF16)
            t1 = _dot(d_b, x_b).astype(BF16)
            invs[n] = invs[n] - _dot(t1, d_b)
        s *= 2

    for n in range(n_chunks):
        rs = slice(n * c, (n + 1) * c)
        t_b = invs[n].astype(BF16)
        u_ref[rs, :] = _dot(t_b, (v[rs] * beta[rs]).astype(BF16)).astype(u_ref.dtype)
        w_ref[rs, :] = _dot(t_b, (kbs[n] * egcs[n]).astype(BF16)).astype(w_ref.dtype)
        qc = q[rs]
        kc = k[rs]
        qk = _dot_nt(qc.astype(BF16), kc.astype(BF16))
        intra = qk * decays[n]
        intra_ref[rs, :] = jnp.concatenate([intra, jnp.zeros_like(intra)], axis=1).astype(intra_ref.dtype)
        qe_ref[rs, :] = (qc * egcs[n]).astype(qe_ref.dtype)
        glast = gcol[(n + 1) * c - 1:(n + 1) * c, :]
        kd_t = (kc * jnp.exp(glast - gcol[rs])).T
        kd_ref[2 * n * c:(2 * n + 2) * c, :] = jnp.concatenate(
            [kd_t, jnp.zeros_like(kd_t)], axis=1).astype(kd_ref.dtype)
        egl_ref[n * SUBLANES:(n + 1) * SUBLANES, :] = jnp.broadcast_to(jnp.exp(glast),
                                                                       (SUBLANES, LANES))


def _dn_prep(proj, conv_w, betab, gcb, gct, t):
    rows = min(DN_PREP_ROWS, t)
    hb = BF16_SUBLANES
    nq = DN_WIDTH // LANES

    def blk(off):
        return pl.BlockSpec((rows, LANES), lambda h, i, off=off: (i, off + h))

    def halo(off):
        return pl.BlockSpec((hb, LANES),
                            lambda h, i, off=off: (jnp.maximum(i * (rows // hb) - 1, 0), off + h))

    def cw(off):
        return pl.BlockSpec((DN_CONV, LANES), lambda h, i, off=off: (0, off + h))

    per_head = pl.BlockSpec((rows, LANES), lambda h, i: (i, h))
    kd_t_spec = pl.BlockSpec((2 * rows, LANES), lambda h, i: (i, h))
    outs = [jax.ShapeDtypeStruct((t, DN_WIDTH), BF16)] * 3
    outs.append(jax.ShapeDtypeStruct((2 * t, DN_WIDTH), BF16))
    outs.append(jax.ShapeDtypeStruct((t, DN_WIDTH), BF16))
    outs.append(jax.ShapeDtypeStruct((t // DN_CHUNK * SUBLANES, DN_WIDTH), F32))
    return pl.pallas_call(
        _dn_prep_kernel,
        grid=(DN_HEADS, t // rows),
        in_specs=[blk(0), blk(nq), blk(2 * nq), halo(0), halo(nq), halo(2 * nq),
                  cw(0), cw(nq), cw(2 * nq), per_head, per_head,
                  pl.BlockSpec((None, 1, rows), lambda h, i: (h, 0, i))],
        out_specs=[per_head] * 3 + [kd_t_spec, per_head]
                  + [pl.BlockSpec((rows // DN_CHUNK * SUBLANES, LANES), lambda h, i: (i, h))],
        out_shape=outs,
        scratch_shapes=[pltpu.VMEM((3, hb + rows, LANES), F32)],
        compiler_params=_cparams(("parallel", "parallel")),
        name="dn_prep",
    )(proj, proj, proj, proj, proj, proj, conv_w, conv_w, conv_w, betab, gcb,
      gct.reshape(DN_HEADS, 1, t))


def _dn_scan_kernel(u_ref, w_ref, qe_ref, kd_ref, intra_ref, egl_ref, z_ref, nw_ref, o_ref, s_ref):
    i = pl.program_id(0)
    c = DN_CHUNK
    n_chunks = u_ref.shape[0] // c

    @pl.when(i == 0)
    def _():
        s_ref[...] = jnp.zeros_like(s_ref)

    nw = nw_ref[...]

    pw = 2 * LANES
    first = lax.broadcasted_iota(jnp.int32, (c, pw), 1) < LANES
    zeros_s = jnp.zeros((DN_HEAD_DIM, LANES), BF16)
    zeros_v = jnp.zeros((c, pw), BF16)
    first_k = lax.broadcasted_iota(jnp.int32, (DN_HEAD_DIM, pw), 1) < LANES
    zeros_k = jnp.zeros((DN_HEAD_DIM, pw), BF16)

    def chunk(n, carry):
        r0 = pl.multiple_of(n * c, c)
        k0 = pl.multiple_of(n * 2 * c, 2 * c)
        e0 = pl.multiple_of(n * SUBLANES, SUBLANES)
        for p in range(DN_HEADS // 2):
            ps = slice(p * pw, (p + 1) * pw)
            s_p = s_ref[p]
            s_b = s_p.astype(BF16)
            s_diag = jnp.concatenate(
                [jnp.concatenate([s_b[:, :LANES], zeros_s], axis=1),
                 jnp.concatenate([zeros_s, s_b[:, LANES:]], axis=1)], axis=0)
            wq = jnp.concatenate([w_ref[pl.ds(r0, c), ps], qe_ref[pl.ds(r0, c), ps]], axis=0)
            ws_qs = _dot(wq, s_diag)
            v_new = u_ref[pl.ds(r0, c), ps].astype(F32) - ws_qs[:c]
            v_b = v_new.astype(BF16)
            v_diag = jnp.concatenate([jnp.where(first, v_b, zeros_v), zeros_v,
                                      jnp.where(first, zeros_v, v_b), zeros_v], axis=0)
            kd_t = kd_ref[pl.ds(k0, 2 * c), ps]
            lhs = jnp.concatenate([intra_ref[pl.ds(r0, c), ps],
                                   jnp.where(first_k, kd_t, zeros_k),
                                   jnp.where(first_k, zeros_k, kd_t)], axis=0)
            iv = _dot(lhs, v_diag)
            o = ws_qs[c:] + iv[:c]
            eg = egl_ref[pl.ds(e0, SUBLANES), ps]
            s_dec = (s_p.reshape(DN_HEAD_DIM // SUBLANES, SUBLANES, pw) * eg[None]
                     ).reshape(DN_HEAD_DIM, pw)
            s_ref[p] = s_dec + iv[c:c + DN_HEAD_DIM] + iv[c + DN_HEAD_DIM:]
            z = z_ref[pl.ds(r0, c), ps].astype(F32)
            gate = z * _sigmoid(z)
            for hh in range(2):
                ls = slice(hh * LANES, (hh + 1) * LANES)
                o_h = o[:, ls]
                o_h = o_h * lax.rsqrt(jnp.mean(o_h * o_h, axis=-1, keepdims=True) + RMS_EPS)
                o_ref[pl.ds(r0, c), p * pw + hh * LANES:p * pw + (hh + 1) * LANES] = (
                    o_h * nw * gate[:, ls]).astype(o_ref.dtype)
        return carry

    lax.fori_loop(0, n_chunks, chunk, 0)


def _dn_scan(u, w, qe, kd, intra, egl, proj, norm_w, t):
    rows = min(512, t)
    full = pl.BlockSpec((rows, DN_WIDTH), lambda i: (i, 0))
    return pl.pallas_call(
        _dn_scan_kernel,
        grid=(t // rows,),
        in_specs=[full, full, full, pl.BlockSpec((2 * rows, DN_WIDTH), lambda i: (i, 0)), full,
                  pl.BlockSpec((rows // DN_CHUNK * SUBLANES, DN_WIDTH), lambda i: (i, 0)),
                  pl.BlockSpec((rows, DN_WIDTH), lambda i: (i, 3)),
                  pl.BlockSpec((1, LANES), lambda i: (0, 0))],
        out_specs=full,
        out_shape=jax.ShapeDtypeStruct((t, DN_WIDTH), BF16),
        scratch_shapes=[pltpu.VMEM((DN_HEADS // 2, DN_HEAD_DIM, 2 * DN_HEAD_DIM), F32)],
        compiler_params=_cparams(("arbitrary",)),
        name="dn_scan",
    )(u, w, qe, kd, intra, egl, proj, norm_w.reshape(1, LANES).astype(F32))


def _attn_kernel(*refs, dilation, span, n_earlier):
    q_ref, k_ref, v_ref, cos_ref, sin_ref = refs[:5]
    earlier = refs[5:5 + 2 * n_earlier]
    out_refs = refs[5 + 2 * n_earlier:-7]
    kbuf, vbuf, krm, vrm, qbuf, obuf, lbuf = refs[-7:]
    i = pl.program_id(1)
    bt, width = q_ref.shape
    d = dilation
    blk = ATT_BLOCK
    prev = d * blk
    per_res = bt // d
    half = ATT_HEAD_DIM // 2
    scale = ATT_HEAD_DIM ** -0.5

    heads = width // LANES

    @pl.when(i == 0)
    def _():
        krm[:, :, 0:blk, :] = jnp.zeros((heads, d, blk, LANES), F32)
        vrm[:, :, 0:blk, :] = jnp.zeros((heads, d, blk, LANES), F32)

    cos, sin = cos_ref[...], sin_ref[...]

    def rope(a):
        a = a.astype(F32)
        return a * cos + pltpu.roll(a, half, 1) * sin

    def rows(start, size):
        return pl.ds(start, size, stride=d) if d > 1 else pl.ds(start, size)

    for h in range(heads):
        ls = slice(h * LANES, (h + 1) * LANES)
        qbuf[h] = rope(q_ref[:, ls]) * scale
        kbuf[h] = rope(k_ref[:, ls])
        vbuf[h] = v_ref[:, ls].astype(F32)
        for r in range(d):
            krm[h, r, blk:blk + per_res, :] = kbuf[h, rows(r, per_res), :]
            vrm[h, r, blk:blk + per_res, :] = vbuf[h, rows(r, per_res), :]

    qi = lax.broadcasted_iota(jnp.int32, (blk, 2 * blk), 0) + blk
    ki = lax.broadcasted_iota(jnp.int32, (blk, 2 * blk), 1)
    dist = qi - ki
    band = (dist >= 0) & (dist <= span)
    band_first = band & ((ki >= blk) | (i > 0))

    for j in range(bt // prev):
        for r in range(d):
            r0 = j * prev + r
            for h in range(heads):
                qj = qbuf[h, rows(r0, blk), :].astype(BF16)
                kj = krm[h, r, j * blk:(j + 2) * blk, :].astype(BF16)
                vj = vrm[h, r, j * blk:(j + 2) * blk, :].astype(BF16)
                s = _dot_nt(qj, kj)
                s = jnp.where(band_first if j == 0 else band, s, NEG_BIG)
                m = jnp.max(s, axis=-1, keepdims=True)
                p = jnp.exp(s - m)
                l = jnp.sum(p, axis=-1, keepdims=True)
                obuf[h, rows(r0, blk), :] = _dot(p.astype(BF16), vj) / l
                lbuf[h, rows(r0, blk), :] = jnp.broadcast_to(m + jnp.log(l), (blk, LANES))

    for h in range(heads):
        ls = slice(h * LANES, (h + 1) * LANES)
        if n_earlier == 0:
            o_ref, lse_ref = out_refs
            o_ref[:, ls] = obuf[h].astype(o_ref.dtype)
            lse_ref[:, ls] = lbuf[h]
        else:
            (o_ref,) = out_refs
            outs = [earlier[2 * g][:, ls].astype(F32) for g in range(n_earlier)] + [obuf[h]]
            lses = [earlier[2 * g + 1][:, ls] for g in range(n_earlier)] + [lbuf[h]]
            m = functools.reduce(jnp.maximum, lses)
            es = [jnp.exp(l - m) for l in lses]
            num = functools.reduce(lambda a, b: a + b, [e * o for e, o in zip(es, outs)])
            den = functools.reduce(lambda a, b: a + b, es)
            o_ref[:, ls] = (num / den).astype(o_ref.dtype)
    krm[:, :, 0:blk, :] = krm[:, :, per_res:per_res + blk, :]
    vrm[:, :, 0:blk, :] = vrm[:, :, per_res:per_res + blk, :]


def _attn_group(proj, cos_t, sin_t, gi, dilation, span, t, earlier=()):
    prev = dilation * ATT_BLOCK
    bt = max(min(512, t), prev)
    width = (4 if prev <= 512 else 2) * LANES
    gw = ATT_GROUP_WIDTH
    per_g = gw // width
    q0 = gi * gw // width
    k0 = q0 + ATT_WIDTH // width
    v0 = k0 + ATT_WIDTH // width

    def cur(c0):
        return pl.BlockSpec((bt, width), lambda hg, i, c0=c0: (i, c0 + hg))

    tab = pl.BlockSpec((bt, LANES), lambda hg, i: (i, 0))
    out = pl.BlockSpec((bt, width), lambda hg, i: (i, hg))
    flat_earlier = [a for pair in earlier for a in pair]
    if earlier:
        out_specs, out_shape = [out], [jax.ShapeDtypeStruct((t, gw), BF16)]
    else:
        out_specs = [out, out]
        out_shape = [jax.ShapeDtypeStruct((t, gw), BF16), jax.ShapeDtypeStruct((t, gw), F32)]
    return pl.pallas_call(
        functools.partial(_attn_kernel, dilation=dilation, span=span, n_earlier=len(earlier)),
        grid=(per_g, t // bt),
        in_specs=[cur(q0), cur(k0), cur(v0), tab, tab] + [out] * len(flat_earlier),
        out_specs=out_specs,
        out_shape=out_shape,
        scratch_shapes=[pltpu.VMEM((width // LANES, bt, LANES), F32),
                        pltpu.VMEM((width // LANES, bt, LANES), F32),
                        pltpu.VMEM((width // LANES, dilation, ATT_BLOCK + bt // dilation, LANES), F32),
                        pltpu.VMEM((width // LANES, dilation, ATT_BLOCK + bt // dilation, LANES), F32),
                        pltpu.VMEM((width // LANES, bt, LANES), F32),
                        pltpu.VMEM((width // LANES, bt, LANES), F32),
                        pltpu.VMEM((width // LANES, bt, LANES), F32)],
        compiler_params=_cparams(("parallel", "arbitrary")),
        name=f"attn_g{gi}",
    )(proj, proj, proj, cos_t, sin_t, *flat_earlier)


def _attention(proj, cos_t, sin_t, t):
    earlier = []
    for gi, (window, dilation) in enumerate(ATT_GROUPS[:-1]):
        earlier.append(_attn_group(proj, cos_t, sin_t, gi, dilation, window // dilation, t))
    window, dilation = ATT_GROUPS[-1]
    (o_att,) = _attn_group(proj, cos_t, sin_t, len(ATT_GROUPS) - 1, dilation, window // dilation, t,
                           earlier=tuple(earlier))
    return o_att


def _mix_kernel(odn_ref, oatt_ref, wdn_ref, watt_ref, gdn_ref, gatt_ref, out_ref):
    y_dn = _dot(odn_ref[...], wdn_ref[...])
    y_att = _dot(oatt_ref[...], watt_ref[...])
    merged = (_sigmoid(gdn_ref[...].astype(F32)) * y_dn
              + _sigmoid(gatt_ref[...].astype(F32)) * y_att)
    out_ref[...] = merged.astype(out_ref.dtype)


def _mix(o_dn, o_att, w_dn, w_att, proj, t, d):
    tm = min(1024, t)
    tn = ATT_GROUP_WIDTH
    g0 = 3 * ATT_WIDTH // tn
    row_dn = pl.BlockSpec((tm, DN_WIDTH), lambda j, i: (i, 0))
    row_g = pl.BlockSpec((tm, ATT_GROUP_WIDTH), lambda j, i: (i, 0))
    return pl.pallas_call(
        _mix_kernel,
        grid=(d // tn, t // tm),
        in_specs=[row_dn, row_g,
                  pl.BlockSpec((DN_WIDTH, tn), lambda j, i: (0, j)),
                  pl.BlockSpec((ATT_GROUP_WIDTH, tn), lambda j, i: (0, j)),
                  pl.BlockSpec((tm, tn), lambda j, i: (i, g0 + j)),
                  pl.BlockSpec((tm, tn), lambda j, i: (i, g0 + d // tn + j))],
        out_specs=pl.BlockSpec((tm, tn), lambda j, i: (i, j)),
        out_shape=jax.ShapeDtypeStruct((t, d), BF16),
        compiler_params=_cparams(("parallel", "parallel")),
        name="mix",
    )(o_dn, o_att, w_dn, w_att, proj, proj)


def _layer_norm(y, g, b):
    mu = jnp.mean(y, axis=-1, keepdims=True)
    yc = y - mu
    var = jnp.mean(yc * yc, axis=-1, keepdims=True)
    return yc * lax.rsqrt(var + LN_EPS) * g + b


def _wo_kernel(m_ref, w_ref, x_ref, y_ref, *, alpha):
    y_ref[...] = alpha * x_ref[...] + _dot(m_ref[...], w_ref[...])


def _wo_residual(merged, w_o, x2, t, d, alpha):
    tm, tn = min(1024, t), min(1024, d)
    tile = pl.BlockSpec((tm, tn), lambda i, j: (i, j))
    return pl.pallas_call(
        functools.partial(_wo_kernel, alpha=alpha),
        grid=(t // tm, d // tn),
        in_specs=[pl.BlockSpec((tm, d), lambda i, j: (i, 0)),
                  pl.BlockSpec((d, tn), lambda i, j: (0, j)),
                  tile],
        out_specs=tile,
        out_shape=jax.ShapeDtypeStruct((t, d), F32),
        compiler_params=_cparams(("parallel", "parallel")),
        name="wo",
    )(merged, w_o, x2)


def _router_kernel(y_ref, g_ref, b_ref, w_ref, bias_ref,
                   x1_ref, x1p_ref, topi_ref, topw_ref, rank_ref, cnt_ref, run_ref):
    i = pl.program_id(0)
    tm, d = y_ref.shape

    @pl.when(i == 0)
    def _():
        run_ref[...] = jnp.zeros_like(run_ref)

    x1 = _layer_norm(y_ref[...], g_ref[...], b_ref[...])
    x1_ref[...] = x1
    x1p_ref[...] = _pack_halves(x1[:, :d // 2], x1[:, d // 2:])

    x_hi, x_mid = _split2(x1)
    w_hi, w_mid = _split2(w_ref[...])
    logits = _dot(x_hi, w_hi) + _dot(x_hi, w_mid) + _dot(x_mid, w_hi)
    s = _sigmoid(logits)
    sc = s + bias_ref[...]
    lane_i = lax.broadcasted_iota(jnp.int32, (tm, N_EXPERTS), 1)
    grp = lane_i // GROUP_SIZE
    lane = lane_i.astype(F32)
    neg = -jnp.inf

    def first_argmax(v):
        m = jnp.max(v, axis=-1, keepdims=True)
        idx = jnp.min(jnp.where(v == m, lane, float(N_EXPERTS)), axis=-1, keepdims=True)
        return m, idx

    gscore = []
    for gi in range(N_GROUPS):
        vg = jnp.where(grp == gi, sc, neg)
        m1, i1 = first_argmax(vg)
        m2 = jnp.max(jnp.where(lane == i1, neg, vg), axis=-1, keepdims=True)
        gscore.append(m1 + m2)
    emask = jnp.zeros((tm, N_EXPERTS), jnp.bool_)
    for gi in range(N_GROUPS):
        ahead = jnp.zeros((tm, 1), jnp.int32)
        for gj in range(N_GROUPS):
            if gj == gi:
                continue
            beats = (gscore[gj] > gscore[gi]) | ((gscore[gj] == gscore[gi]) & (gj < gi))
            ahead = ahead + beats.astype(jnp.int32)
        emask = emask | ((grp == gi) & (ahead < TOPK_GROUPS))
    masked = jnp.where(emask, sc, neg)

    sel =jnp.zeros((tm, N_EXPERTS), jnp.bool_)
    idxs, vals = [], []
    for _ in range(TOP_K):
        _, ik = first_argmax(masked)
        hit = lane == ik
        sel = sel | hit
        masked = jnp.where(hit, neg, masked)
        idxs.append(ik)
        vals.append(jnp.sum(jnp.where(hit, s, 0.0), axis=-1, keepdims=True))
    wsum = vals[0]
    for v in vals[1:]:
        wsum = wsum + v

    sel_b = jnp.where(sel, 1.0, 0.0).astype(BF16)
    strict = jnp.where(lax.broadcasted_iota(jnp.int32, (tm, tm), 1)
                       < lax.broadcasted_iota(jnp.int32, (tm, tm), 0), 1.0, 0.0).astype(BF16)
    rank_excl = run_ref[0:1, :] + _dot(strict, sel_b)
    run_new = run_ref[0:1, :] + jnp.sum(sel_b.astype(F32), axis=0, keepdims=True)
    run_ref[...] = jnp.broadcast_to(run_new, run_ref.shape)
    cnt_ref[...] = jnp.broadcast_to(run_new, cnt_ref.shape).astype(jnp.int32)

    topi = jnp.zeros((tm, N_EXPERTS), F32)
    topw = jnp.zeros((tm, N_EXPERTS), F32)
    rnk = jnp.zeros((tm, N_EXPERTS), F32)
    for kk in range(TOP_K):
        hit = lane == idxs[kk]
        rk = jnp.sum(jnp.where(hit, rank_excl, 0.0), axis=-1, keepdims=True)
        topi = jnp.where(lane == kk, idxs[kk], topi)
        topw = jnp.where(lane == kk, vals[kk] / wsum * ROUTED_SCALE, topw)
        rnk = jnp.where(lane == kk, rk, rnk)
    topi_ref[...] = topi.astype(jnp.int32)
    topw_ref[...] = topw
    rank_ref[...] = rnk.astype(jnp.int32)


def _ln_router(y, g, b, w_router, bias, t, d):
    tm = min(256, t)
    row = pl.BlockSpec((tm, N_EXPERTS), lambda i: (i, 0))
    full = pl.BlockSpec((tm, d), lambda i: (i, 0))
    vec = pl.BlockSpec((1, d), lambda i: (0, 0))
    return pl.pallas_call(
        _router_kernel,
        grid=(t // tm,),
        in_specs=[full, vec, vec,
                  pl.BlockSpec((d, N_EXPERTS), lambda i: (0, 0)),
                  pl.BlockSpec((1, N_EXPERTS), lambda i: (0, 0))],
        out_specs=[full, pl.BlockSpec((tm, d // 2), lambda i: (i, 0)), row, row, row,
                   pl.BlockSpec((SUBLANES, N_EXPERTS), lambda i: (0, 0))],
        out_shape=[jax.ShapeDtypeStruct((t, d), F32),
                   jax.ShapeDtypeStruct((t, d // 2), jnp.uint32),
                   jax.ShapeDtypeStruct((t, N_EXPERTS), jnp.int32),
                   jax.ShapeDtypeStruct((t, N_EXPERTS), F32),
                   jax.ShapeDtypeStruct((t, N_EXPERTS), jnp.int32),
                   jax.ShapeDtypeStruct((SUBLANES, N_EXPERTS), jnp.int32)],
        scratch_shapes=[pltpu.VMEM((SUBLANES, N_EXPERTS), F32)],
        compiler_params=_cparams(("arbitrary",)),
        name="ln_router",
    )(y, g.reshape(1, d).astype(F32), b.reshape(1, d).astype(F32), w_router.astype(F32),
      bias.reshape(1, N_EXPERTS).astype(F32))


def _slot_kernel(topi_ref, rank_ref, start_ref, pos_ref):
    topi = topi_ref[...]
    lane = lax.broadcasted_iota(jnp.int32, topi.shape, 1)
    start = jnp.broadcast_to(start_ref[...], topi.shape)
    pos = rank_ref[...]
    for kk in range(TOP_K):
        seg = jnp.sum(jnp.where(lane == topi[:, kk:kk + 1], start, 0.0), axis=-1, keepdims=True)
        pos = jnp.where(lane == kk, pos + seg.astype(jnp.int32), pos)
    pos_ref[...] = pos


def _slots(topi, rank, pad_start, t):
    tm = min(1024, t)
    row = pl.BlockSpec((tm, N_EXPERTS), lambda i: (i, 0))
    return pl.pallas_call(
        _slot_kernel,
        grid=(t // tm,),
        in_specs=[row, row, pl.BlockSpec((1, N_EXPERTS), lambda i: (0, 0))],
        out_specs=row,
        out_shape=jax.ShapeDtypeStruct((t, N_EXPERTS), jnp.int32),
        compiler_params=_cparams(("parallel",)),
        name="slots",
    )(topi, rank, pad_start.astype(F32).reshape(1, N_EXPERTS))


def _row_copy(src_ref, src_row, dst_ref, dst_row, sem):
    return pltpu.make_async_copy(src_ref.at[pl.ds(src_row, 1)], dst_ref.at[pl.ds(dst_row, 1)], sem)


def _dispatch_kernel(pos_ref, x_ref, xs_ref, sem):
    i = pl.program_id(0)
    tb = x_ref.shape[0]
    base = i * (tb * TOP_K)

    def start(r, carry):
        for kk in range(TOP_K):
            _row_copy(x_ref, r, xs_ref, pos_ref[base + r * TOP_K + kk], sem).start()
        return carry

    lax.fori_loop(0, tb, start, 0)
    for kk in range(TOP_K):
        pltpu.make_async_copy(x_ref, xs_ref.at[pl.ds(0, tb)], sem).wait()


def _dispatch(pos_flat, x1, n_slots, t, d):
    tb = min(512, t)
    return pl.pallas_call(
        _dispatch_kernel,
        grid_spec=pltpu.PrefetchScalarGridSpec(
            num_scalar_prefetch=1,
            grid=(t // tb,),
            in_specs=[pl.BlockSpec((tb, d), lambda i, pos: (i, 0))],
            out_specs=pl.BlockSpec(memory_space=pl.ANY),
            scratch_shapes=[pltpu.SemaphoreType.DMA],
        ),
        out_shape=jax.ShapeDtypeStruct((n_slots, d), x1.dtype),
        compiler_params=_cparams(("arbitrary",)),
        name="dispatch",
    )(pos_flat, x1)


def _expert_kernel(be_ref, nv_ref, first_ref, nxt_ref, ord_ref, nu_ref, xs_ref, wgu_hbm, wdn_hbm,
                   y_ref, gu_stage, dn_stage, gu_cache, dn_cache, act_ref, sem):
    b = pl.program_id(0)
    n_used = nu_ref[0]

    def weight_copies(e):
        return (pltpu.make_async_copy(wgu_hbm.at[e], gu_stage, sem.at[0]),
                pltpu.make_async_copy(wdn_hbm.at[e], dn_stage, sem.at[1]))

    @pl.when(b == 0)
    def _():
        act_ref[...] = jnp.zeros_like(act_ref)

    @pl.when(b <= n_used)
    def _():
        rows, hd = xs_ref.shape
        kc = min(EXPERT_CHUNK, hd)
        cur = jnp.minimum(b, n_used - 1)
        e = be_ref[cur]

        @pl.when((first_ref[cur] == 1) & (b < n_used))
        def _():
            @pl.when(b == 0)
            def _():
                for cp in weight_copies(e):
                    cp.start()

            for cp in weight_copies(e):
                cp.wait()
            for c0 in range(0, 2 * hd, kc):
                for j in range(EXPERT_FF // LANES):
                    gu_cache[c0:c0 + kc, 2 * j * LANES:(2 * j + 1) * LANES] = (
                        gu_stage[c0:c0 + kc, j * LANES:(j + 1) * LANES].astype(BF16))
                    gu_cache[c0:c0 + kc, (2 * j + 1) * LANES:(2 * j + 2) * LANES] = (
                        gu_stage[c0:c0 + kc, EXPERT_FF + j * LANES:EXPERT_FF + (j + 1) * LANES]
                        .astype(BF16))
                dn_cache[ord_ref[cur] % 2, :, c0:c0 + kc] = dn_stage[:, c0:c0 + kc].astype(BF16)

            @pl.when(nxt_ref[cur] >= 0)
            def _():
                for cp in weight_copies(nxt_ref[cur]):
                    cp.start()

        prev_slot = ord_ref[jnp.maximum(b - 1, 0)] % 2
        act_prev = act_ref[...]
        nc = min(EXPERT_OUT_CHUNK, hd)
        for c0 in range(0, hd, nc):
            y_ref[:, c0:c0 + nc] = _pack_halves(
                _dot(act_prev, dn_cache[prev_slot, :, c0:c0 + nc]),
                _dot(act_prev, dn_cache[prev_slot, :, hd + c0:hd + c0 + nc]))

        rid = lax.broadcasted_iota(jnp.int32, (rows, 1), 0)
        valid = rid < nv_ref[cur]
        lo, hi = _unpack_halves(xs_ref[...])
        x_lo = jnp.where(valid, lo, 0.0).astype(BF16)
        x_hi = jnp.where(valid, hi, 0.0).astype(BF16)
        for j in range(EXPERT_FF // LANES):
            cs = slice(2 * j * LANES, (2 * j + 2) * LANES)
            hgu = _dot(x_lo, gu_cache[:hd, cs]) + _dot(x_hi, gu_cache[hd:, cs])
            gate, up = hgu[:, :LANES], hgu[:, LANES:]
            act_ref[:, j * LANES:(j + 1) * LANES] = (gate * _sigmoid(gate) * up).astype(BF16)


def _experts(blk_expert, blk_valid, blk_first, blk_next, blk_ord, n_used, xs, w_gu, w_dn, d):
    n_slots = xs.shape[0]
    nb = n_slots // MOE_ROWS

    def in_map(b, be, nv, fi, nx, od, nu):
        return (jnp.minimum(b, nu[0] - 1), 0)

    def out_map(b, be, nv, fi, nx, od, nu):
        return (jnp.minimum(jnp.maximum(b - 1, 0), nu[0] - 1), 0)

    return pl.pallas_call(
        _expert_kernel,
        grid_spec=pltpu.PrefetchScalarGridSpec(
            num_scalar_prefetch=6,
            grid=(nb + 1,),
            in_specs=[pl.BlockSpec((MOE_ROWS, d // 2), in_map),
                      pl.BlockSpec(memory_space=pl.ANY),
                      pl.BlockSpec(memory_space=pl.ANY)],
            out_specs=pl.BlockSpec((MOE_ROWS, d // 2), out_map),
            scratch_shapes=[pltpu.VMEM((d, 2 * EXPERT_FF), w_gu.dtype),
                            pltpu.VMEM((EXPERT_FF, d), w_dn.dtype),
                            pltpu.VMEM((d, 2 * EXPERT_FF), BF16),
                            pltpu.VMEM((2, EXPERT_FF, d), BF16),
                            pltpu.VMEM((MOE_ROWS, EXPERT_FF), BF16),
                            pltpu.SemaphoreType.DMA((2,))],
        ),
        out_shape=jax.ShapeDtypeStruct((n_slots, d // 2), jnp.uint32),
        compiler_params=_cparams(("arbitrary",)),
        name="experts",
    )(blk_expert, blk_valid, blk_first, blk_next, blk_ord, n_used, xs, w_gu, w_dn)


def _shared_kernel(x_ref, wgu_ref, wdn_ref, y_ref, act_ref):
    i = pl.program_id(0)

    @pl.when(i == 0)
    def _():
        act_ref[...] = jnp.zeros_like(act_ref)

    y_ref[...] = _dot(act_ref[...], wdn_ref[...]).astype(y_ref.dtype)
    hd = x_ref.shape[1]
    lo, hi = _unpack_halves(x_ref[...])
    hgu = _dot(lo.astype(BF16), wgu_ref[:hd, :]) + _dot(hi.astype(BF16), wgu_ref[hd:, :])
    gate, up = hgu[:, :EXPERT_FF], hgu[:, EXPERT_FF:]
    act_ref[...] = (gate * _sigmoid(gate) * up).astype(BF16)


def _shared(x1p, w_gu, w_dn, t, d):
    tm = min(512, t)
    n = t // tm
    return pl.pallas_call(
        _shared_kernel,
        grid=(n + 1,),
        in_specs=[pl.BlockSpec((tm, d // 2), lambda i: (jnp.minimum(i, n - 1), 0)),
                  pl.BlockSpec((d, 2 * EXPERT_FF), lambda i: (0, 0)),
                  pl.BlockSpec((EXPERT_FF, d), lambda i: (0, 0))],
        out_specs=pl.BlockSpec((tm, d), lambda i: (jnp.maximum(i - 1, 0), 0)),
        out_shape=jax.ShapeDtypeStruct((t, d), BF16),
        scratch_shapes=[pltpu.VMEM((tm, EXPERT_FF), BF16)],
        compiler_params=_cparams(("arbitrary",)),
        name="shared",
    )(x1p, w_gu, w_dn)


def _combine_kernel(pos_ref, x_ref, ysh_ref, topw_ref, g_ref, b_ref, y_hbm, out_ref, buf_ref, sem,
                    *, alpha):
    i = pl.program_id(0)
    n = pl.num_programs(0)
    tb = x_ref.shape[0]

    def gather(tile, slot):
        base = tile * (tb * TOP_K)

        def body(r, carry):
            for kk in range(TOP_K):
                _row_copy(y_hbm, pos_ref[base + r * TOP_K + kk], buf_ref.at[slot, kk], r,
                          sem.at[slot]).start()
            return carry

        lax.fori_loop(0, tb, body, 0)

    @pl.when(i == 0)
    def _():
        gather(0, 0)

    @pl.when(i + 1 < n)
    def _():
        gather(i + 1, (i + 1) % 2)

    slot = i % 2
    for kk in range(TOP_K):
        pltpu.make_async_copy(y_hbm.at[pl.ds(0, tb)], buf_ref.at[slot, kk], sem.at[slot]).wait()
    topw = topw_ref[...]
    hd = x_ref.shape[1] // 2
    acc_lo = alpha * x_ref[:, :hd] + ysh_ref[:, :hd].astype(F32)
    acc_hi = alpha * x_ref[:, hd:] + ysh_ref[:, hd:].astype(F32)
    for kk in range(TOP_K):
        lo, hi = _unpack_halves(buf_ref[slot, kk])
        acc_lo = acc_lo + topw[:, kk:kk + 1] * lo
        acc_hi = acc_hi + topw[:, kk:kk + 1] * hi
    inv_d = 1.0 / (2 * hd)
    mu = (jnp.sum(acc_lo, axis=-1, keepdims=True) + jnp.sum(acc_hi, axis=-1, keepdims=True)) * inv_d
    c_lo, c_hi = acc_lo - mu, acc_hi - mu
    var = (jnp.sum(c_lo * c_lo, axis=-1, keepdims=True)
           + jnp.sum(c_hi * c_hi, axis=-1, keepdims=True)) * inv_d
    rstd = lax.rsqrt(var + LN_EPS)
    out_ref[:, :hd] = c_lo * rstd * g_ref[:, :hd] + b_ref[:, :hd]
    out_ref[:, hd:] = c_hi * rstd * g_ref[:, hd:] + b_ref[:, hd:]


def _combine(pos_flat, x1, ysh, topw, g, b, y, t, d, alpha):
    tb = min(128, t)
    row = pl.BlockSpec((tb, d), lambda i, pos: (i, 0))
    vec = pl.BlockSpec((1, d), lambda i, pos: (0, 0))
    return pl.pallas_call(
        functools.partial(_combine_kernel, alpha=alpha),
        grid_spec=pltpu.PrefetchScalarGridSpec(
            num_scalar_prefetch=1,
            grid=(t // tb,),
            in_specs=[row, row, pl.BlockSpec((tb, N_EXPERTS), lambda i, pos: (i, 0)), vec, vec,
                      pl.BlockSpec(memory_space=pl.ANY)],
            out_specs=row,
            scratch_shapes=[pltpu.VMEM((2, TOP_K, tb, d // 2), jnp.uint32),
                            pltpu.SemaphoreType.DMA((2,))],
        ),
        out_shape=jax.ShapeDtypeStruct((t, d), F32),
        compiler_params=_cparams(("arbitrary",)),
        name="combine",
    )(pos_flat, x1, ysh, topw, g.reshape(1, d).astype(F32), b.reshape(1, d).astype(F32), y)


def _rope_tables(t):
    half = ATT_HEAD_DIM // 2
    inv_freq = ROPE_THETA ** (-jnp.arange(half, dtype=F32) / half)
    ang = jnp.arange(t, dtype=F32)[:, None] * inv_freq[None, :]
    cos, sin = jnp.cos(ang), jnp.sin(ang)
    return jnp.concatenate([cos, cos], -1), jnp.concatenate([-sin, sin], -1)


def _token_mixer(x2, w_in, conv_w, a_log, dt_bias, dn_norm_w, w_dn_branch, w_att_branch, w_o,
                 alpha):
    t, d = x2.shape
    n_main = 4 * DN_WIDTH
    n_ba = 2 * DN_HEADS
    w_t = w_in.T
    betab, gcb, gct, x_b = _dn_gates(x2, w_t, n_main, a_log, dt_bias)
    proj_dn = _proj(x_b, w_t, 0, n_main, "proj_dn")
    proj_att = _proj(x_b, w_t, n_main + n_ba, w_t.shape[0] - n_main - n_ba, "proj_att")
    u, w, qe, kd, intra, egl = _dn_prep(proj_dn, conv_w.astype(F32), betab, gcb, gct, t)
    o_dn = _dn_scan(u, w, qe, kd, intra, egl, proj_dn, dn_norm_w, t)

    cos_t, sin_t = _rope_tables(t)
    o_att = _attention(proj_att, cos_t, sin_t, t)
    merged = _mix(o_dn, o_att, w_dn_branch.astype(BF16), w_att_branch.astype(BF16), proj_att, t, d)
    return _wo_residual(merged, w_o.astype(BF16), x2, t, d, alpha)


def _moe(y_mix, ln1_g, ln1_b, w_router, router_bias, w_exp_gate_up, w_exp_down, w_sh_gate_up,
         w_sh_down, ln_g, ln_b, alpha):
    t, d = y_mix.shape
    r = MOE_ROWS
    x1, x1p, topi, topw, rank, cnt = _ln_router(y_mix, ln1_g, ln1_b, w_router, router_bias, t, d)
    counts = cnt[0]
    padded = (counts + r - 1) // r * r
    pad_end = jnp.cumsum(padded)
    pad_start = pad_end - padded
    nb = (t * TOP_K) // r + N_EXPERTS
    blk_row0 = jnp.arange(nb, dtype=jnp.int32) * r
    in_blk = (blk_row0[:, None] >= pad_start[None, :]) & (blk_row0[:, None] < pad_end[None, :])
    expert_ids = jnp.arange(N_EXPERTS, dtype=jnp.int32)
    blk_expert = jnp.where(blk_row0 < pad_end[-1],
                           jnp.sum(jnp.where(in_blk, expert_ids[None, :], 0), axis=1),
                           N_EXPERTS - 1).astype(jnp.int32)
    blk_valid = jnp.sum(jnp.where(in_blk, jnp.clip(counts[None, :] - (blk_row0[:, None] - pad_start[None, :]), 0, r), 0),
                        axis=1).astype(jnp.int32)
    n_used = (pad_end[-1:] // r).astype(jnp.int32)
    blk_ids = jnp.arange(nb, dtype=jnp.int32)
    prev_expert = jnp.concatenate([jnp.full((1,), -1, jnp.int32), blk_expert[:-1]])
    blk_first = (blk_expert != prev_expert).astype(jnp.int32)
    blk_ord = jnp.cumsum(blk_first) - 1
    after = jnp.sum(jnp.where(in_blk, pad_end[None, :] // r, 0), axis=1)
    after_expert = jnp.sum(jnp.where(after[:, None] == blk_ids[None, :], blk_expert[None, :], 0), axis=1)
    blk_next = jnp.where(after < n_used[0], after_expert, -1).astype(jnp.int32)
    pos_flat = _slots(topi, rank, pad_start, t)[:, :TOP_K].reshape(-1)

    xs = _dispatch(pos_flat, x1p, nb * r, t, d // 2)
    y = _experts(blk_expert, blk_valid, blk_first, blk_next, blk_ord.astype(jnp.int32), n_used, xs,
                 w_exp_gate_up, w_exp_down, d)
    ysh = _shared(x1p, w_sh_gate_up.astype(BF16), w_sh_down.astype(BF16), t, d)
    return _combine(pos_flat, x1, ysh, topw, ln_g, ln_b, y, t, d, alpha)


def kernel(x, w_in, conv_w, a_log, dt_bias, dn_norm_w, w_dn_branch, w_att_branch, w_o, ln1_g, ln1_b,
           w_router, router_bias, w_exp_gate_up, w_exp_down, w_sh_gate_up, w_sh_down, ln2_g, ln2_b):
    bsz, t, d = x.shape
    depth = w_in.shape[0]
    alpha = (2.0 * depth) ** 0.25
    outs = []
    for bi in range(bsz):
        xb = x[bi]
        for l in range(depth):
            y_mix = _token_mixer(xb, w_in[l], conv_w[l], a_log[l], dt_bias[l], dn_norm_w[l],
                                 w_dn_branch[l], w_att_branch[l], w_o[l], alpha)
            xb = _moe(y_mix, ln1_g[l], ln1_b[l], w_router[l], router_bias[l], w_exp_gate_up[l],
                      w_exp_down[l], w_sh_gate_up[l], w_sh_down[l], ln2_g[l], ln2_b[l], alpha)
        outs.append(xb)
    return jnp.stack(outs, 0)
```

```python
import functools

import jax
import jax.numpy as jnp
from jax import lax
from jax.experimental import pallas as pl
from jax.experimental.pallas import tpu as pltpu

F32 = jnp.float32
BF16 = jnp.bfloat16

DN_HEADS = 16
DN_HEAD_DIM = 128
DN_WIDTH = DN_HEADS * DN_HEAD_DIM
DN_CONV = 4
DN_CHUNK = 64
ATT_GROUPS = ((128, 1), (512, 4), (2048, 16))
ATT_HEADS_PER_GROUP = 8
ATT_HEAD_DIM = 128
ATT_GROUP_WIDTH = ATT_HEADS_PER_GROUP * ATT_HEAD_DIM
ATT_WIDTH = len(ATT_GROUPS) * ATT_GROUP_WIDTH
ATT_BLOCK = 128
ROPE_THETA = 10000.0
N_EXPERTS = 128
TOP_K = 8
N_GROUPS = 8
GROUP_SIZE = N_EXPERTS // N_GROUPS
TOPK_GROUPS = 4
EXPERT_FF = 384
ROUTED_SCALE = 2.5
LN_EPS = 1e-5
RMS_EPS = 1e-6
L2_EPS = 1e-6

LANES = 128
SUBLANES = 8
BF16_SUBLANES = 16
VMEM_LIMIT = 56 * 1024 * 1024

DN_PREP_ROWS = 1024
MOE_ROWS = 256
EXPERT_CHUNK = 1024
EXPERT_OUT_CHUNK = 512
NEG_BIG = -1e30


def _cparams(sem, vmem=VMEM_LIMIT):
    return pltpu.CompilerParams(dimension_semantics=sem, vmem_limit_bytes=vmem)


def _split3(a):
    hi = a.astype(BF16)
    r1 = a - hi.astype(F32)
    mid = r1.astype(BF16)
    lo = (r1 - mid.astype(F32)).astype(BF16)
    return hi, mid, lo


def _split2(a):
    hi = a.astype(BF16)
    return hi, (a - hi.astype(F32)).astype(BF16)


def _dot(a, b):
    return jnp.dot(a, b, preferred_element_type=F32)


def _dot_nt(a, b):
    return lax.dot_general(a, b, (((1,), (1,)), ((), ())), preferred_element_type=F32)


def _dot_tn(a, b):
    return lax.dot_general(a, b, (((0,), (0,)), ((), ())), preferred_element_type=F32)


def _sigmoid(x):
    return 1.0 / (1.0 + jnp.exp(-x))


def _pack_halves(lo, hi):
    lo_b = lax.bitcast_convert_type(lo.astype(BF16).astype(F32), jnp.uint32)
    hi_b = lax.bitcast_convert_type(hi.astype(BF16).astype(F32), jnp.uint32)
    return (hi_b & jnp.uint32(0xFFFF0000)) | (lo_b >> 16)


def _unpack_halves(u):
    lo = lax.bitcast_convert_type(u << 16, F32)
    hi = lax.bitcast_convert_type(u & jnp.uint32(0xFFFF0000), F32)
    return lo, hi


def _proj_kernel(x_ref, wt_ref, o_ref):
    o_ref[...] = _dot_nt(x_ref[...], wt_ref[...].astype(BF16)).astype(o_ref.dtype)


def _proj(x_b, w_t, row0, n, name):
    m, k = x_b.shape
    tm, tn = min(1024, m), min(512, n)
    return pl.pallas_call(
        _proj_kernel,
        grid=(m // tm, n // tn),
        in_specs=[pl.BlockSpec((tm, k), lambda i, j: (i, 0)),
                  pl.BlockSpec((pl.Element(tn), pl.Element(k)),
                               lambda i, j: (pl.multiple_of(row0 + j * tn, SUBLANES), 0))],
        out_specs=pl.BlockSpec((tm, tn), lambda i, j: (i, j)),
        out_shape=jax.ShapeDtypeStruct((m, n), BF16),
        compiler_params=_cparams(("parallel", "parallel")),
        name=name,
    )(x_b, w_t)


def _gates_kernel(x_ref, w_ref, prm_ref, betab_ref, gcb_ref, gct_ref, xb_ref):
    tm = x_ref.shape[0]
    x_hi, x_mid = _split2(x_ref[...])
    xb_ref[...] = x_hi
    w_hi, w_mid = _split2(w_ref[...])
    logits = _dot_nt(x_hi, w_hi) + _dot_nt(x_hi, w_mid) + _dot_nt(x_mid, w_hi)

    def softplus(v):
        return jnp.maximum(v, 0.0) + jnp.log(1.0 + jnp.exp(-jnp.abs(v)))

    prm = prm_ref[...]
    beta = _sigmoid(logits)
    g = prm[0:1, :] * softplus(logits + prm[1:2, :])

    ri = lax.broadcasted_iota(jnp.int32, (tm, tm), 0)
    ci = lax.broadcasted_iota(jnp.int32, (tm, tm), 1)
    same = (ri // DN_CHUNK) == (ci // DN_CHUNK)
    lower = jnp.where(same & (ci <= ri), 1.0, 0.0).astype(BF16)
    gc = jnp.zeros((tm, LANES), F32)
    for part in _split3(g):
        gc = gc + _dot(lower, part)
    gct_ref[...] = gc.T[DN_HEADS:2 * DN_HEADS, :]
    for h in range(DN_HEADS):
        sl = slice(h * LANES, (h + 1) * LANES)
        betab_ref[:, sl] = jnp.broadcast_to(beta[:, h:h + 1], (tm, LANES))
        gcb_ref[:, sl] = jnp.broadcast_to(gc[:, DN_HEADS + h:DN_HEADS + h + 1], (tm, LANES))


def _dn_gates(x2, w_t, col0, a_log, dt_bias):
    t, d = x2.shape
    tm = min(512, t)
    neg_a = -jnp.exp(a_log.astype(F32))
    prm = jnp.zeros((SUBLANES, LANES), F32)
    prm = prm.at[0, DN_HEADS:2 * DN_HEADS].set(neg_a).at[1, DN_HEADS:2 * DN_HEADS].set(dt_bias.astype(F32))
    return pl.pallas_call(
        _gates_kernel,
        grid=(t // tm,),
        in_specs=[pl.BlockSpec((tm, d), lambda i: (i, 0)),
                  pl.BlockSpec((LANES, d), lambda i: (col0 // LANES, 0)),
                  pl.BlockSpec((SUBLANES, LANES), lambda i: (0, 0))],
        out_specs=[pl.BlockSpec((tm, DN_WIDTH), lambda i: (i, 0)),
                   pl.BlockSpec((tm, DN_WIDTH), lambda i: (i, 0)),
                   pl.BlockSpec((DN_HEADS, tm), lambda i: (0, i)),
                   pl.BlockSpec((tm, d), lambda i: (i, 0))],
        out_shape=[jax.ShapeDtypeStruct((t, DN_WIDTH), F32),
                   jax.ShapeDtypeStruct((t, DN_WIDTH), F32),
                   jax.ShapeDtypeStruct((DN_HEADS, t), F32),
                   jax.ShapeDtypeStruct((t, d), BF16)],
        compiler_params=_cparams(("parallel",)),
        name="dn_gates",
    )(x2, w_t.astype(F32), prm)


def _dn_prep_kernel(q_ref, k_ref, v_ref, hq_ref, hk_ref, hv_ref, cq_ref, ck_ref, cv_ref,
                    betab_ref, gcb_ref, gct_ref,
                    u_ref, w_ref, qe_ref, kd_ref, intra_ref, egl_ref, cbuf):
    i = pl.program_id(1)
    rows = q_ref.shape[0]
    c = DN_CHUNK
    halo_rows = hq_ref.shape[0]

    def conv_silu(slot, x_ref, halo_ref, cw_ref):
        cbuf[slot, 0:halo_rows, :] = jnp.where(i > 0, halo_ref[...].astype(F32), 0.0)
        cbuf[slot, halo_rows:halo_rows + rows, :] = x_ref[...].astype(F32)
        cw = cw_ref[...]
        y = jnp.zeros((rows, LANES), F32)
        for j in range(DN_CONV):
            off = halo_rows - (DN_CONV - 1) + j
            y = y + cw[j:j + 1, :] * cbuf[slot, off:off + rows, :]
        return y * _sigmoid(y)

    def l2n(a):
        return a * lax.rsqrt(jnp.sum(a * a, axis=-1, keepdims=True) + L2_EPS)

    q = l2n(conv_silu(0, q_ref, hq_ref, cq_ref)) * (DN_HEAD_DIM ** -0.5)
    k = l2n(conv_silu(1, k_ref, hk_ref, ck_ref))
    v = conv_silu(2, v_ref, hv_ref, cv_ref)
    beta = betab_ref[...]
    gcol = gcb_ref[...]
    grow_all = gct_ref[...]

    ii = lax.broadcasted_iota(jnp.int32, (c, c), 0)
    jj = lax.broadcasted_iota(jnp.int32, (c, c), 1)
    eye = jnp.where(ii == jj, 1.0, 0.0)
    n_chunks = rows // c

    decays, a_mats = [], []
    kbs, egcs = [], []
    for n in range(n_chunks):
        rs = slice(n * c, (n + 1) * c)
        kc = k[rs]
        kb = kc * beta[rs]
        grow = grow_all[:, rs]
        diff = jnp.where(ii >= jj, gcol[rs, :c] - grow, 0.0)
        decay = jnp.where(ii >= jj, jnp.exp(diff), 0.0)
        kk = _dot_nt(kb.astype(BF16), kc.astype(BF16))
        a_mats.append(jnp.where(ii > jj, kk * decay, 0.0))
        decays.append(decay)
        kbs.append(kb)
        egcs.append(jnp.exp(gcol[rs]))

    invs = []
    for n in range(n_chunks):
        invs.append(eye - jnp.where((ii // 2 == jj // 2) & (ii > jj), a_mats[n], 0.0))
    s = 2
    while s < c:
        sel = ((ii // (2 * s)) == (jj // (2 * s))) & ((ii // s) > (jj // s))
        for n in range(n_chunks):
            d_b = invs[n].astype(BF16)
            x_b = jnp.where(sel, a_mats[n], 0.0).astype(BF16)
            t1 = _dot(d_b, x_b).astype(BF16)
            invs[n] = invs[n] - _dot(t1, d_b)
        s *= 2

    for n in range(n_chunks):
        rs = slice(n * c, (n + 1) * c)
        t_b = invs[n].astype(BF16)
        u_ref[rs, :] = _dot(t_b, (v[rs] * beta[rs]).astype(BF16)).astype(u_ref.dtype)
        w_ref[rs, :] = _dot(t_b, (kbs[n] * egcs[n]).astype(BF16)).astype(w_ref.dtype)
        qc = q[rs]
        kc = k[rs]
        qk = _dot_nt(qc.astype(BF16), kc.astype(BF16))
        intra = qk * decays[n]
        intra_ref[rs, :] = jnp.concatenate([intra, jnp.zeros_like(intra)], axis=1).astype(intra_ref.dtype)
        qe_ref[rs, :] = (qc * egcs[n]).astype(qe_ref.dtype)
        glast = gcol[(n + 1) * c - 1:(n + 1) * c, :]
        kd_t = (kc * jnp.exp(glast - gcol[rs])).T
        kd_ref[2 * n * c:(2 * n + 2) * c, :] = jnp.concatenate(
            [kd_t, jnp.zeros_like(kd_t)], axis=1).astype(kd_ref.dtype)
        egl_ref[n * SUBLANES:(n + 1) * SUBLANES, :] = jnp.broadcast_to(jnp.exp(glast),
                                                                       (SUBLANES, LANES))


def _dn_prep(proj, conv_w, betab, gcb, gct, t):
    rows = min(DN_PREP_ROWS, t)
    hb = BF16_SUBLANES
    nq = DN_WIDTH // LANES

    def blk(off):
        return pl.BlockSpec((rows, LANES), lambda h, i, off=off: (i, off + h))

    def halo(off):
        return pl.BlockSpec((hb, LANES),
                            lambda h, i, off=off: (jnp.maximum(i * (rows // hb) - 1, 0), off + h))

    def cw(off):
        return pl.BlockSpec((DN_CONV, LANES), lambda h, i, off=off: (0, off + h))

    per_head = pl.BlockSpec((rows, LANES), lambda h, i: (i, h))
    kd_t_spec = pl.BlockSpec((2 * rows, LANES), lambda h, i: (i, h))
    outs = [jax.ShapeDtypeStruct((t, DN_WIDTH), BF16)] * 3
    outs.append(jax.ShapeDtypeStruct((2 * t, DN_WIDTH), BF16))
    outs.append(jax.ShapeDtypeStruct((t, DN_WIDTH), BF16))
    outs.append(jax.ShapeDtypeStruct((t // DN_CHUNK * SUBLANES, DN_WIDTH), F32))
    return pl.pallas_call(
        _dn_prep_kernel,
        grid=(DN_HEADS, t // rows),
        in_specs=[blk(0), blk(nq), blk(2 * nq), halo(0), halo(nq), halo(2 * nq),
                  cw(0), cw(nq), cw(2 * nq), per_head, per_head,
                  pl.BlockSpec((None, 1, rows), lambda h, i: (h, 0, i))],
        out_specs=[per_head] * 3 + [kd_t_spec, per_head]
                  + [pl.BlockSpec((rows // DN_CHUNK * SUBLANES, LANES), lambda h, i: (i, h))],
        out_shape=outs,
        scratch_shapes=[pltpu.VMEM((3, hb + rows, LANES), F32)],
        compiler_params=_cparams(("parallel", "parallel")),
        name="dn_prep",
    )(proj, proj, proj, proj, proj, proj, conv_w, conv_w, conv_w, betab, gcb,
      gct.reshape(DN_HEADS, 1, t))


def _dn_scan_kernel(u_ref, w_ref, qe_ref, kd_ref, intra_ref, egl_ref, z_ref, nw_ref, o_ref, s_ref):
    i = pl.program_id(0)
    c = DN_CHUNK
    n_chunks = u_ref.shape[0] // c

    @pl.when(i == 0)
    def _():
        s_ref[...] = jnp.zeros_like(s_ref)

    nw = nw_ref[...]

    pw = 2 * LANES
    first = lax.broadcasted_iota(jnp.int32, (c, pw), 1) < LANES
    zeros_s = jnp.zeros((DN_HEAD_DIM, LANES), BF16)
    zeros_v = jnp.zeros((c, pw), BF16)
    first_k = lax.broadcasted_iota(jnp.int32, (DN_HEAD_DIM, pw), 1) < LANES
    zeros_k = jnp.zeros((DN_HEAD_DIM, pw), BF16)

    def chunk(n, carry):
        r0 = pl.multiple_of(n * c, c)
        k0 = pl.multiple_of(n * 2 * c, 2 * c)
        e0 = pl.multiple_of(n * SUBLANES, SUBLANES)
        for p in range(DN_HEADS // 2):
            ps = slice(p * pw, (p + 1) * pw)
            s_p = s_ref[p]
            s_b = s_p.astype(BF16)
            s_diag = jnp.concatenate(
                [jnp.concatenate([s_b[:, :LANES], zeros_s], axis=1),
                 jnp.concatenate([zeros_s, s_b[:, LANES:]], axis=1)], axis=0)
            wq = jnp.concatenate([w_ref[pl.ds(r0, c), ps], qe_ref[pl.ds(r0, c), ps]], axis=0)
            ws_qs = _dot(wq, s_diag)
            v_new = u_ref[pl.ds(r0, c), ps].astype(F32) - ws_qs[:c]
            v_b = v_new.astype(BF16)
            v_diag = jnp.concatenate([jnp.where(first, v_b, zeros_v), zeros_v,
                                      jnp.where(first, zeros_v, v_b), zeros_v], axis=0)
            kd_t = kd_ref[pl.ds(k0, 2 * c), ps]
            lhs = jnp.concatenate([intra_ref[pl.ds(r0, c), ps],
                                   jnp.where(first_k, kd_t, zeros_k),
                                   jnp.where(first_k, zeros_k, kd_t)], axis=0)
            iv = _dot(lhs, v_diag)
            o = ws_qs[c:] + iv[:c]
            eg = egl_ref[pl.ds(e0, SUBLANES), ps]
            s_dec = (s_p.reshape(DN_HEAD_DIM // SUBLANES, SUBLANES, pw) * eg[None]
                     ).reshape(DN_HEAD_DIM, pw)
            s_ref[p] = s_dec + iv[c:c + DN_HEAD_DIM] + iv[c + DN_HEAD_DIM:]
            z = z_ref[pl.ds(r0, c), ps].astype(F32)
            gate = z * _sigmoid(z)
            for hh in range(2):
                ls = slice(hh * LANES, (hh + 1) * LANES)
                o_h = o[:, ls]
                o_h = o_h * lax.rsqrt(jnp.mean(o_h * o_h, axis=-1, keepdims=True) + RMS_EPS)
                o_ref[pl.ds(r0, c), p * pw + hh * LANES:p * pw + (hh + 1) * LANES] = (
                    o_h * nw * gate[:, ls]).astype(o_ref.dtype)
        return carry

    lax.fori_loop(0, n_chunks, chunk, 0)


def _dn_scan(u, w, qe, kd, intra, egl, proj, norm_w, t):
    rows = min(512, t)
    full = pl.BlockSpec((rows, DN_WIDTH), lambda i: (i, 0))
    return pl.pallas_call(
        _dn_scan_kernel,
        grid=(t // rows,),
        in_specs=[full, full, full, pl.BlockSpec((2 * rows, DN_WIDTH), lambda i: (i, 0)), full,
                  pl.BlockSpec((rows // DN_CHUNK * SUBLANES, DN_WIDTH), lambda i: (i, 0)),
                  pl.BlockSpec((rows, DN_WIDTH), lambda i: (i, 3)),
                  pl.BlockSpec((1, LANES), lambda i: (0, 0))],
        out_specs=full,
        out_shape=jax.ShapeDtypeStruct((t, DN_WIDTH), BF16),
        scratch_shapes=[pltpu.VMEM((DN_HEADS // 2, DN_HEAD_DIM, 2 * DN_HEAD_DIM), F32)],
        compiler_params=_cparams(("arbitrary",)),
        name="dn_scan",
    )(u, w, qe, kd, intra, egl, proj, norm_w.reshape(1, LANES).astype(F32))


def _attn_kernel(*refs, dilation, span, n_earlier):
    q_ref, k_ref, v_ref, cos_ref, sin_ref = refs[:5]
    earlier = refs[5:5 + 2 * n_earlier]
    out_refs = refs[5 + 2 * n_earlier:-7]
    kbuf, vbuf, krm, vrm, qbuf, obuf, lbuf = refs[-7:]
    i = pl.program_id(1)
    bt, width = q_ref.shape
    d = dilation
    blk = ATT_BLOCK
    prev = d * blk
    per_res = bt // d
    half = ATT_HEAD_DIM // 2
    scale = ATT_HEAD_DIM ** -0.5

    heads = width // LANES

    @pl.when(i == 0)
    def _():
        krm[:, :, 0:blk, :] = jnp.zeros((heads, d, blk, LANES), F32)
        vrm[:, :, 0:blk, :] = jnp.zeros((heads, d, blk, LANES), F32)

    cos, sin = cos_ref[...], sin_ref[...]

    def rope(a):
        a = a.astype(F32)
        return a * cos + pltpu.roll(a, half, 1) * sin

    def rows(start, size):
        return pl.ds(start, size, stride=d) if d > 1 else pl.ds(start, size)

    for h in range(heads):
        ls = slice(h * LANES, (h + 1) * LANES)
        qbuf[h] = rope(q_ref[:, ls]) * scale
        kbuf[h] = rope(k_ref[:, ls])
        vbuf[h] = v_ref[:, ls].astype(F32)
        for r in range(d):
            krm[h, r, blk:blk + per_res, :] = kbuf[h, rows(r, per_res), :]
            vrm[h, r, blk:blk + per_res, :] = vbuf[h, rows(r, per_res), :]

    qi = lax.broadcasted_iota(jnp.int32, (blk, 2 * blk), 0) + blk
    ki = lax.broadcasted_iota(jnp.int32, (blk, 2 * blk), 1)
    dist = qi - ki
    band = (dist >= 0) & (dist <= span)
    band_first = band & ((ki >= blk) | (i > 0))

    for j in range(bt // prev):
        for r in range(d):
            r0 = j * prev + r
            for h in range(heads):
                qj = qbuf[h, rows(r0, blk), :].astype(BF16)
                kj = krm[h, r, j * blk:(j + 2) * blk, :].astype(BF16)
                vj = vrm[h, r, j * blk:(j + 2) * blk, :].astype(BF16)
                s = _dot_nt(qj, kj)
                s = jnp.where(band_first if j == 0 else band, s, NEG_BIG)
                m = jnp.max(s, axis=-1, keepdims=True)
                p = jnp.exp(s - m)
                l = jnp.sum(p, axis=-1, keepdims=True)
                obuf[h, rows(r0, blk), :] = _dot(p.astype(BF16), vj) / l
                lbuf[h, rows(r0, blk), :] = jnp.broadcast_to(m + jnp.log(l), (blk, LANES))

    for h in range(heads):
        ls = slice(h * LANES, (h + 1) * LANES)
        if n_earlier == 0:
            o_ref, lse_ref = out_refs
            o_ref[:, ls] = obuf[h].astype(o_ref.dtype)
            lse_ref[:, ls] = lbuf[h]
        else:
            (o_ref,) = out_refs
            outs = [earlier[2 * g][:, ls].astype(F32) for g in range(n_earlier)] + [obuf[h]]
            lses = [earlier[2 * g + 1][:, ls] for g in range(n_earlier)] + [lbuf[h]]
            m = functools.reduce(jnp.maximum, lses)
            es = [jnp.exp(l - m) for l in lses]
            num = functools.reduce(lambda a, b: a + b, [e * o for e, o in zip(es, outs)])
            den = functools.reduce(lambda a, b: a + b, es)
            o_ref[:, ls] = (num / den).astype(o_ref.dtype)
    krm[:, :, 0:blk, :] = krm[:, :, per_res:per_res + blk, :]
    vrm[:, :, 0:blk, :] = vrm[:, :, per_res:per_res + blk, :]


def _attn_group(proj, cos_t, sin_t, gi, dilation, span, t, earlier=()):
    prev = dilation * ATT_BLOCK
    bt = max(min(512, t), prev)
    width = (4 if prev <= 512 else 2) * LANES
    gw = ATT_GROUP_WIDTH
    per_g = gw // width
    q0 = gi * gw // width
    k0 = q0 + ATT_WIDTH // width
    v0 = k0 + ATT_WIDTH // width

    def cur(c0):
        return pl.BlockSpec((bt, width), lambda hg, i, c0=c0: (i, c0 + hg))

    tab = pl.BlockSpec((bt, LANES), lambda hg, i: (i, 0))
    out = pl.BlockSpec((bt, width), lambda hg, i: (i, hg))
    flat_earlier = [a for pair in earlier for a in pair]
    if earlier:
        out_specs, out_shape = [out], [jax.ShapeDtypeStruct((t, gw), BF16)]
    else:
        out_specs = [out, out]
        out_shape = [jax.ShapeDtypeStruct((t, gw), BF16), jax.ShapeDtypeStruct((t, gw), F32)]
    return pl.pallas_call(
        functools.partial(_attn_kernel, dilation=dilation, span=span, n_earlier=len(earlier)),
        grid=(per_g, t // bt),
        in_specs=[cur(q0), cur(k0), cur(v0), tab, tab] + [out] * len(flat_earlier),
        out_specs=out_specs,
        out_shape=out_shape,
        scratch_shapes=[pltpu.VMEM((width // LANES, bt, LANES), F32),
                        pltpu.VMEM((width // LANES, bt, LANES), F32),
                        pltpu.VMEM((width // LANES, dilation, ATT_BLOCK + bt // dilation, LANES), F32),
                        pltpu.VMEM((width // LANES, dilation, ATT_BLOCK + bt // dilation, LANES), F32),
                        pltpu.VMEM((width // LANES, bt, LANES), F32),
                        pltpu.VMEM((width // LANES, bt, LANES), F32),
                        pltpu.VMEM((width // LANES, bt, LANES), F32)],
        compiler_params=_cparams(("parallel", "arbitrary")),
        name=f"attn_g{gi}",
    )(proj, proj, proj, cos_t, sin_t, *flat_earlier)


def _attention(proj, cos_t, sin_t, t):
    earlier = []
    for gi, (window, dilation) in enumerate(ATT_GROUPS[:-1]):
        earlier.append(_attn_group(proj, cos_t, sin_t, gi, dilation, window // dilation, t))
    window, dilation = ATT_GROUPS[-1]
    (o_att,) = _attn_group(proj, cos_t, sin_t, len(ATT_GROUPS) - 1, dilation, window // dilation, t,
                           earlier=tuple(earlier))
    return o_att


def _mix_kernel(odn_ref, oatt_ref, wdn_ref, watt_ref, gdn_ref, gatt_ref, out_ref):
    y_dn = _dot(odn_ref[...], wdn_ref[...])
    y_att = _dot(oatt_ref[...], watt_ref[...])
    merged = (_sigmoid(gdn_ref[...].astype(F32)) * y_dn
              + _sigmoid(gatt_ref[...].astype(F32)) * y_att)
    out_ref[...] = merged.astype(out_ref.dtype)


def _mix(o_dn, o_att, w_dn, w_att, proj, t, d):
    tm = min(1024, t)
    tn = ATT_GROUP_WIDTH
    g0 = 3 * ATT_WIDTH // tn
    row_dn = pl.BlockSpec((tm, DN_WIDTH), lambda j, i: (i, 0))
    row_g = pl.BlockSpec((tm, ATT_GROUP_WIDTH), lambda j, i: (i, 0))
    return pl.pallas_call(
        _mix_kernel,
        grid=(d // tn, t // tm),
        in_specs=[row_dn, row_g,
                  pl.BlockSpec((DN_WIDTH, tn), lambda j, i: (0, j)),
                  pl.BlockSpec((ATT_GROUP_WIDTH, tn), lambda j, i: (0, j)),
                  pl.BlockSpec((tm, tn), lambda j, i: (i, g0 + j)),
                  pl.BlockSpec((tm, tn), lambda j, i: (i, g0 + d // tn + j))],
        out_specs=pl.BlockSpec((tm, tn), lambda j, i: (i, j)),
        out_shape=jax.ShapeDtypeStruct((t, d), BF16),
        compiler_params=_cparams(("parallel", "parallel")),
        name="mix",
    )(o_dn, o_att, w_dn, w_att, proj, proj)


def _layer_norm(y, g, b):
    mu = jnp.mean(y, axis=-1, keepdims=True)
    yc = y - mu
    var = jnp.mean(yc * yc, axis=-1, keepdims=True)
    return yc * lax.rsqrt(var + LN_EPS) * g + b


def _wo_kernel(m_ref, w_ref, x_ref, y_ref, *, alpha):
    y_ref[...] = alpha * x_ref[...] + _dot(m_ref[...], w_ref[...])


def _wo_residual(merged, w_o, x2, t, d, alpha):
    tm, tn = min(1024, t), min(1024, d)
    tile = pl.BlockSpec((tm, tn), lambda i, j: (i, j))
    return pl.pallas_call(
        functools.partial(_wo_kernel, alpha=alpha),
        grid=(t // tm, d // tn),
        in_specs=[pl.BlockSpec((tm, d), lambda i, j: (i, 0)),
                  pl.BlockSpec((d, tn), lambda i, j: (0, j)),
                  tile],
        out_specs=tile,
        out_shape=jax.ShapeDtypeStruct((t, d), F32),
        compiler_params=_cparams(("parallel", "parallel")),
        name="wo",
    )(merged, w_o, x2)


def _router_kernel(y_ref, g_ref, b_ref, w_ref, bias_ref,
                   x1_ref, x1p_ref, topi_ref, topw_ref, rank_ref, cnt_ref, run_ref):
    i = pl.program_id(0)
    tm, d = y_ref.shape

    @pl.when(i == 0)
    def _():
        run_ref[...] = jnp.zeros_like(run_ref)

    x1 = _layer_norm(y_ref[...], g_ref[...], b_ref[...])
    x1_ref[...] = x1
    x1p_ref[...] = _pack_halves(x1[:, :d // 2], x1[:, d // 2:])

    x_hi, x_mid = _split2(x1)
    w_hi, w_mid = _split2(w_ref[...])
    logits = _dot(x_hi, w_hi) + _dot(x_hi, w_mid) + _dot(x_mid, w_hi)
    s = _sigmoid(logits)
    sc = s + bias_ref[...]
    lane_i = lax.broadcasted_iota(jnp.int32, (tm, N_EXPERTS), 1)
    grp = lane_i // GROUP_SIZE
    lane = lane_i.astype(F32)
    neg = -jnp.inf

    def first_argmax(v):
        m = jnp.max(v, axis=-1, keepdims=True)
        idx = jnp.min(jnp.where(v == m, lane, float(N_EXPERTS)), axis=-1, keepdims=True)
        return m, idx

    gscore = []
    for gi in range(N_GROUPS):
        vg = jnp.where(grp == gi, sc, neg)
        m1, i1 = first_argmax(vg)
        m2 = jnp.max(jnp.where(lane == i1, neg, vg), axis=-1, keepdims=True)
        gscore.append(m1 + m2)
    emask = jnp.zeros((tm, N_EXPERTS), jnp.bool_)
    for gi in range(N_GROUPS):
        ahead = jnp.zeros((tm, 1), jnp.int32)
        for gj in range(N_GROUPS):
            if gj == gi:
                continue
            beats = (gscore[gj] > gscore[gi]) | ((gscore[gj] == gscore[gi]) & (gj < gi))
            ahead = ahead + beats.astype(jnp.int32)
        emask = emask | ((grp == gi) & (ahead < TOPK_GROUPS))
    masked = jnp.where(emask, sc, neg)

    sel =jnp.zeros((tm, N_EXPERTS), jnp.bool_)
    idxs, vals = [], []
    for _ in range(TOP_K):
        _, ik = first_argmax(masked)
        hit = lane == ik
        sel = sel | hit
        masked = jnp.where(hit, neg, masked)
        idxs.append(ik)
        vals.append(jnp.sum(jnp.where(hit, s, 0.0), axis=-1, keepdims=True))
    wsum = vals[0]
    for v in vals[1:]:
        wsum = wsum + v

    sel_b = jnp.where(sel, 1.0, 0.0).astype(BF16)
    strict = jnp.where(lax.broadcasted_iota(jnp.int32, (tm, tm), 1)
                       < lax.broadcasted_iota(jnp.int32, (tm, tm), 0), 1.0, 0.0).astype(BF16)
    rank_excl = run_ref[0:1, :] + _dot(strict, sel_b)
    run_new = run_ref[0:1, :] + jnp.sum(sel_b.astype(F32), axis=0, keepdims=True)
    run_ref[...] = jnp.broadcast_to(run_new, run_ref.shape)
    cnt_ref[...] = jnp.broadcast_to(run_new, cnt_ref.shape).astype(jnp.int32)

    topi = jnp.zeros((tm, N_EXPERTS), F32)
    topw = jnp.zeros((tm, N_EXPERTS), F32)
    rnk = jnp.zeros((tm, N_EXPERTS), F32)
    for kk in range(TOP_K):
        hit = lane == idxs[kk]
        rk = jnp.sum(jnp.where(hit, rank_excl, 0.0), axis=-1, keepdims=True)
        topi = jnp.where(lane == kk, idxs[kk], topi)
        topw = jnp.where(lane == kk, vals[kk] / wsum * ROUTED_SCALE, topw)
        rnk = jnp.where(lane == kk, rk, rnk)
    topi_ref[...] = topi.astype(jnp.int32)
    topw_ref[...] = topw
    rank_ref[...] = rnk.astype(jnp.int32)


def _ln_router(y, g, b, w_router, bias, t, d):
    tm = min(256, t)
    row = pl.BlockSpec((tm, N_EXPERTS), lambda i: (i, 0))
    full = pl.BlockSpec((tm, d), lambda i: (i, 0))
    vec = pl.BlockSpec((1, d), lambda i: (0, 0))
    return pl.pallas_call(
        _router_kernel,
        grid=(t // tm,),
        in_specs=[full, vec, vec,
                  pl.BlockSpec((d, N_EXPERTS), lambda i: (0, 0)),
                  pl.BlockSpec((1, N_EXPERTS), lambda i: (0, 0))],
        out_specs=[full, pl.BlockSpec((tm, d // 2), lambda i: (i, 0)), row, row, row,
                   pl.BlockSpec((SUBLANES, N_EXPERTS), lambda i: (0, 0))],
        out_shape=[jax.ShapeDtypeStruct((t, d), F32),
                   jax.ShapeDtypeStruct((t, d // 2), jnp.uint32),
                   jax.ShapeDtypeStruct((t, N_EXPERTS), jnp.int32),
                   jax.ShapeDtypeStruct((t, N_EXPERTS), F32),
                   jax.ShapeDtypeStruct((t, N_EXPERTS), jnp.int32),
                   jax.ShapeDtypeStruct((SUBLANES, N_EXPERTS), jnp.int32)],
        scratch_shapes=[pltpu.VMEM((SUBLANES, N_EXPERTS), F32)],
        compiler_params=_cparams(("arbitrary",)),
        name="ln_router",
    )(y, g.reshape(1, d).astype(F32), b.reshape(1, d).astype(F32), w_router.astype(F32),
      bias.reshape(1, N_EXPERTS).astype(F32))


def _slot_kernel(topi_ref, rank_ref, start_ref, pos_ref):
    topi = topi_ref[...]
    lane = lax.broadcasted_iota(jnp.int32, topi.shape, 1)
    start = jnp.broadcast_to(start_ref[...], topi.shape)
    pos = rank_ref[...]
    for kk in range(TOP_K):
        seg = jnp.sum(jnp.where(lane == topi[:, kk:kk + 1], start, 0.0), axis=-1, keepdims=True)
        pos = jnp.where(lane == kk, pos + seg.astype(jnp.int32), pos)
    pos_ref[...] = pos


def _slots(topi, rank, pad_start, t):
    tm = min(1024, t)
    row = pl.BlockSpec((tm, N_EXPERTS), lambda i: (i, 0))
    return pl.pallas_call(
        _slot_kernel,
        grid=(t // tm,),
        in_specs=[row, row, pl.BlockSpec((1, N_EXPERTS), lambda i: (0, 0))],
        out_specs=row,
        out_shape=jax.ShapeDtypeStruct((t, N_EXPERTS), jnp.int32),
        compiler_params=_cparams(("parallel",)),
        name="slots",
    )(topi, rank, pad_start.astype(F32).reshape(1, N_EXPERTS))


def _row_copy(src_ref, src_row, dst_ref, dst_row, sem):
    return pltpu.make_async_copy(src_ref.at[pl.ds(src_row, 1)], dst_ref.at[pl.ds(dst_row, 1)], sem)


def _dispatch_kernel(pos_ref, pad_lo_ref, pad_len_ref, tail_ref, x_ref, xs_ref, zbuf, sem, zsem):
    i = pl.program_id(0)
    tb = x_ref.shape[0]
    base = i * (tb * TOP_K)

    def zero_fill(do_start):
        def issue(cp):
            cp.start() if do_start else cp.wait()

        def per_expert(e, carry):
            lo = pad_lo_ref[e]
            n = pad_len_ref[e]
            head = jnp.minimum((SUBLANES - (lo & (SUBLANES - 1))) & (SUBLANES - 1), n)

            def row(r, c2):
                issue(pltpu.make_async_copy(zbuf.at[pl.ds(0, 1)], xs_ref.at[pl.ds(lo + r, 1)], zsem))
                return c2

            lax.fori_loop(0, head, row, 0)
            off = lo + head
            rest = n - head
            for p in (8, 16, 32, 64, 128):
                @pl.when((rest & p) != 0)
                def _(p=p, off=off):
                    issue(pltpu.make_async_copy(
                        zbuf.at[pl.ds(0, p)],
                        xs_ref.at[pl.ds(pl.multiple_of(off, SUBLANES), p)], zsem))

                off = off + (rest & p)
            return carry

        lax.fori_loop(0, N_EXPERTS, per_expert, 0)

        def tail_block(kb, carry):
            row0 = pl.multiple_of(tail_ref[0] + kb * MOE_ROWS, MOE_ROWS)
            issue(pltpu.make_async_copy(zbuf, xs_ref.at[pl.ds(row0, MOE_ROWS)], zsem))
            return carry

        lax.fori_loop(0, tail_ref[1], tail_block, 0)

    @pl.when(i == 0)
    def _():
        zbuf[...] = jnp.zeros_like(zbuf)
        zero_fill(True)

    def start(r, carry):
        for kk in range(TOP_K):
            _row_copy(x_ref, r, xs_ref, pos_ref[base + r * TOP_K + kk], sem).start()
        return carry

    lax.fori_loop(0, tb, start, 0)
    for kk in range(TOP_K):
        pltpu.make_async_copy(x_ref, xs_ref.at[pl.ds(0, tb)], sem).wait()

    @pl.when(i == 0)
    def _():
        zero_fill(False)


def _dispatch(pos_flat, pad_lo, pad_len, tail, x1, n_slots, t, d):
    tb = min(512, t)
    return pl.pallas_call(
        _dispatch_kernel,
        grid_spec=pltpu.PrefetchScalarGridSpec(
            num_scalar_prefetch=4,
            grid=(t // tb,),
            in_specs=[pl.BlockSpec((tb, d), lambda i, *_: (i, 0))],
            out_specs=pl.BlockSpec(memory_space=pl.ANY),
            scratch_shapes=[pltpu.VMEM((MOE_ROWS, d), x1.dtype),
                            pltpu.SemaphoreType.DMA, pltpu.SemaphoreType.DMA],
        ),
        out_shape=jax.ShapeDtypeStruct((n_slots, d), x1.dtype),
        compiler_params=_cparams(("arbitrary",)),
        name="dispatch",
    )(pos_flat, pad_lo, pad_len, tail, x1)


def _expert_kernel(be_ref, nv_ref, first_ref, nxt_ref, ord_ref, nu_ref, xs_ref, wgu_hbm, wdn_hbm,
                   y_ref, gu_stage, dn_stage, gu_cache, dn_cache, act_ref, sem):
    b = pl.program_id(0)
    n_used = nu_ref[0]

    def weight_copies(e):
        return (pltpu.make_async_copy(wgu_hbm.at[e], gu_stage, sem.at[0]),
                pltpu.make_async_copy(wdn_hbm.at[e], dn_stage, sem.at[1]))

    @pl.when(b == 0)
    def _():
        act_ref[...] = jnp.zeros_like(act_ref)

    @pl.when(b > n_used)
    def _():
        y_ref[...] = jnp.zeros_like(y_ref)

    @pl.when(b <= n_used)
    def _():
        rows, hd = xs_ref.shape
        kc = min(EXPERT_CHUNK, hd)
        cur = jnp.minimum(b, n_used - 1)
        e = be_ref[cur]

        @pl.when((first_ref[cur] == 1) & (b < n_used))
        def _():
            @pl.when(b == 0)
            def _():
                for cp in weight_copies(e):
                    cp.start()

            for cp in weight_copies(e):
                cp.wait()
            for c0 in range(0, 2 * hd, kc):
                for j in range(EXPERT_FF // LANES):
                    gu_cache[c0:c0 + kc, 2 * j * LANES:(2 * j + 1) * LANES] = (
                        gu_stage[c0:c0 + kc, j * LANES:(j + 1) * LANES].astype(BF16))
                    gu_cache[c0:c0 + kc, (2 * j + 1) * LANES:(2 * j + 2) * LANES] = (
                        gu_stage[c0:c0 + kc, EXPERT_FF + j * LANES:EXPERT_FF + (j + 1) * LANES]
                        .astype(BF16))
                dn_cache[ord_ref[cur] % 2, :, c0:c0 + kc] = dn_stage[:, c0:c0 + kc].astype(BF16)

            @pl.when(nxt_ref[cur] >= 0)
            def _():
                for cp in weight_copies(nxt_ref[cur]):
                    cp.start()

        prev_slot = ord_ref[jnp.maximum(b - 1, 0)] % 2
        act_prev = act_ref[...]
        nc = min(EXPERT_OUT_CHUNK, hd)
        for c0 in range(0, hd, nc):
            y_ref[:, c0:c0 + nc] = _pack_halves(
                _dot(act_prev, dn_cache[prev_slot, :, c0:c0 + nc]),
                _dot(act_prev, dn_cache[prev_slot, :, hd + c0:hd + c0 + nc]))

        rid = lax.broadcasted_iota(jnp.int32, (rows, 1), 0)
        valid = rid < nv_ref[cur]
        lo, hi = _unpack_halves(xs_ref[...])
        x_lo = jnp.where(valid, lo, 0.0).astype(BF16)
        x_hi = jnp.where(valid, hi, 0.0).astype(BF16)
        for j in range(EXPERT_FF // LANES):
            cs = slice(2 * j * LANES, (2 * j + 2) * LANES)
            hgu = _dot(x_lo, gu_cache[:hd, cs]) + _dot(x_hi, gu_cache[hd:, cs])
            gate, up = hgu[:, :LANES], hgu[:, LANES:]
            act_ref[:, j * LANES:(j + 1) * LANES] = (gate * _sigmoid(gate) * up).astype(BF16)


def _experts(blk_expert, blk_valid, blk_first, blk_next, blk_ord, n_used, xs, w_gu, w_dn, d):
    n_slots = xs.shape[0]
    nb = n_slots // MOE_ROWS

    def in_map(b, be, nv, fi, nx, od, nu):
        return (jnp.minimum(b, nu[0] - 1), 0)

    def out_map(b, be, nv, fi, nx, od, nu):
        return (jnp.maximum(b - 1, 0), 0)

    return pl.pallas_call(
        _expert_kernel,
        grid_spec=pltpu.PrefetchScalarGridSpec(
            num_scalar_prefetch=6,
            grid=(nb + 1,),
            in_specs=[pl.BlockSpec((MOE_ROWS, d // 2), in_map),
                      pl.BlockSpec(memory_space=pl.ANY),
                      pl.BlockSpec(memory_space=pl.ANY)],
            out_specs=pl.BlockSpec((MOE_ROWS, d // 2), out_map),
            scratch_shapes=[pltpu.VMEM((d, 2 * EXPERT_FF), w_gu.dtype),
                            pltpu.VMEM((EXPERT_FF, d), w_dn.dtype),
                            pltpu.VMEM((d, 2 * EXPERT_FF), BF16),
                            pltpu.VMEM((2, EXPERT_FF, d), BF16),
                            pltpu.VMEM((MOE_ROWS, EXPERT_FF), BF16),
                            pltpu.SemaphoreType.DMA((2,))],
        ),
        out_shape=jax.ShapeDtypeStruct((n_slots, d // 2), jnp.uint32),
        compiler_params=_cparams(("arbitrary",)),
        name="experts",
    )(blk_expert, blk_valid, blk_first, blk_next, blk_ord, n_used, xs, w_gu, w_dn)


def _shared_kernel(x_ref, wgu_ref, wdn_ref, y_ref, act_ref):
    i = pl.program_id(0)

    @pl.when(i == 0)
    def _():
        act_ref[...] = jnp.zeros_like(act_ref)

    y_ref[...] = _dot(act_ref[...], wdn_ref[...]).astype(y_ref.dtype)
    hd = x_ref.shape[1]
    lo, hi = _unpack_halves(x_ref[...])
    hgu = _dot(lo.astype(BF16), wgu_ref[:hd, :]) + _dot(hi.astype(BF16), wgu_ref[hd:, :])
    gate, up = hgu[:, :EXPERT_FF], hgu[:, EXPERT_FF:]
    act_ref[...] = (gate * _sigmoid(gate) * up).astype(BF16)


def _shared(x1p, w_gu, w_dn, t, d):
    tm = min(512, t)
    n = t // tm
    return pl.pallas_call(
        _shared_kernel,
        grid=(n + 1,),
        in_specs=[pl.BlockSpec((tm, d // 2), lambda i: (jnp.minimum(i, n - 1), 0)),
                  pl.BlockSpec((d, 2 * EXPERT_FF), lambda i: (0, 0)),
                  pl.BlockSpec((EXPERT_FF, d), lambda i: (0, 0))],
        out_specs=pl.BlockSpec((tm, d), lambda i: (jnp.maximum(i - 1, 0), 0)),
        out_shape=jax.ShapeDtypeStruct((t, d), BF16),
        scratch_shapes=[pltpu.VMEM((tm, EXPERT_FF), BF16)],
        compiler_params=_cparams(("arbitrary",)),
        name="shared",
    )(x1p, w_gu, w_dn)


def _combine_kernel(pos_ref, x_ref, ysh_ref, topw_ref, g_ref, b_ref, y_hbm, out_ref, buf_ref, sem,
                    *, alpha):
    i = pl.program_id(0)
    n = pl.num_programs(0)
    tb = x_ref.shape[0]

    def gather(tile, slot):
        base = tile * (tb * TOP_K)

        def body(r, carry):
            for kk in range(TOP_K):
                _row_copy(y_hbm, pos_ref[base + r * TOP_K + kk], buf_ref.at[slot, kk], r,
                          sem.at[slot]).start()
            return carry

        lax.fori_loop(0, tb, body, 0)

    @pl.when(i == 0)
    def _():
        gather(0, 0)

    @pl.when(i + 1 < n)
    def _():
        gather(i + 1, (i + 1) % 2)

    slot = i % 2
    for kk in range(TOP_K):
        pltpu.make_async_copy(y_hbm.at[pl.ds(0, tb)], buf_ref.at[slot, kk], sem.at[slot]).wait()
    topw = topw_ref[...]
    hd = x_ref.shape[1] // 2
    acc_lo = alpha * x_ref[:, :hd] + ysh_ref[:, :hd].astype(F32)
    acc_hi = alpha * x_ref[:, hd:] + ysh_ref[:, hd:].astype(F32)
    for kk in range(TOP_K):
        lo, hi = _unpack_halves(buf_ref[slot, kk])
        acc_lo = acc_lo + topw[:, kk:kk + 1] * lo
        acc_hi = acc_hi + topw[:, kk:kk + 1] * hi
    inv_d = 1.0 / (2 * hd)
    mu = (jnp.sum(acc_lo, axis=-1, keepdims=True) + jnp.sum(acc_hi, axis=-1, keepdims=True)) * inv_d
    c_lo, c_hi = acc_lo - mu, acc_hi - mu
    var = (jnp.sum(c_lo * c_lo, axis=-1, keepdims=True)
           + jnp.sum(c_hi * c_hi, axis=-1, keepdims=True)) * inv_d
    rstd = lax.rsqrt(var + LN_EPS)
    out_ref[:, :hd] = c_lo * rstd * g_ref[:, :hd] + b_ref[:, :hd]
    out_ref[:, hd:] = c_hi * rstd * g_ref[:, hd:] + b_ref[:, hd:]


def _combine(pos_flat, x1, ysh, topw, g, b, y, t, d, alpha):
    tb = min(128, t)
    row = pl.BlockSpec((tb, d), lambda i, pos: (i, 0))
    vec = pl.BlockSpec((1, d), lambda i, pos: (0, 0))
    return pl.pallas_call(
        functools.partial(_combine_kernel, alpha=alpha),
        grid_spec=pltpu.PrefetchScalarGridSpec(
            num_scalar_prefetch=1,
            grid=(t // tb,),
            in_specs=[row, row, pl.BlockSpec((tb, N_EXPERTS), lambda i, pos: (i, 0)), vec, vec,
                      pl.BlockSpec(memory_space=pl.ANY)],
            out_specs=row,
            scratch_shapes=[pltpu.VMEM((2, TOP_K, tb, d // 2), jnp.uint32),
                            pltpu.SemaphoreType.DMA((2,))],
        ),
        out_shape=jax.ShapeDtypeStruct((t, d), F32),
        compiler_params=_cparams(("arbitrary",)),
        name="combine",
    )(pos_flat, x1, ysh, topw, g.reshape(1, d).astype(F32), b.reshape(1, d).astype(F32), y)


def _rope_tables(t):
    half = ATT_HEAD_DIM // 2
    inv_freq = ROPE_THETA ** (-jnp.arange(half, dtype=F32) / half)
    ang = jnp.arange(t, dtype=F32)[:, None] * inv_freq[None, :]
    cos, sin = jnp.cos(ang), jnp.sin(ang)
    return jnp.concatenate([cos, cos], -1), jnp.concatenate([-sin, sin], -1)


def _token_mixer(x2, w_in, conv_w, a_log, dt_bias, dn_norm_w, w_dn_branch, w_att_branch, w_o,
                 alpha):
    t, d = x2.shape
    n_main = 4 * DN_WIDTH
    n_ba = 2 * DN_HEADS
    w_t = w_in.T
    betab, gcb, gct, x_b = _dn_gates(x2, w_t, n_main, a_log, dt_bias)
    proj_dn = _proj(x_b, w_t, 0, n_main, "proj_dn")
    proj_att = _proj(x_b, w_t, n_main + n_ba, w_t.shape[0] - n_main - n_ba, "proj_att")
    u, w, qe, kd, intra, egl = _dn_prep(proj_dn, conv_w.astype(F32), betab, gcb, gct, t)
    o_dn = _dn_scan(u, w, qe, kd, intra, egl, proj_dn, dn_norm_w, t)

    cos_t, sin_t = _rope_tables(t)
    o_att = _attention(proj_att, cos_t, sin_t, t)
    merged = _mix(o_dn, o_att, w_dn_branch.astype(BF16), w_att_branch.astype(BF16), proj_att, t, d)
    return _wo_residual(merged, w_o.astype(BF16), x2, t, d, alpha)


def _moe(y_mix, ln1_g, ln1_b, w_router, router_bias, w_exp_gate_up, w_exp_down, w_sh_gate_up,
         w_sh_down, ln_g, ln_b, alpha):
    t, d = y_mix.shape
    r = MOE_ROWS
    x1, x1p, topi, topw, rank, cnt = _ln_router(y_mix, ln1_g, ln1_b, w_router, router_bias, t, d)
    counts = cnt[0]
    padded = (counts + r - 1) // r * r
    pad_end = jnp.cumsum(padded)
    pad_start = pad_end - padded
    nb = (t * TOP_K) // r + N_EXPERTS
    blk_row0 = jnp.arange(nb, dtype=jnp.int32) * r
    in_blk = (blk_row0[:, None] >= pad_start[None, :]) & (blk_row0[:, None] < pad_end[None, :])
    expert_ids = jnp.arange(N_EXPERTS, dtype=jnp.int32)
    blk_expert = jnp.where(blk_row0 < pad_end[-1],
                           jnp.sum(jnp.where(in_blk, expert_ids[None, :], 0), axis=1),
                           N_EXPERTS - 1).astype(jnp.int32)
    blk_valid = jnp.sum(jnp.where(in_blk, jnp.clip(counts[None, :] - (blk_row0[:, None] - pad_start[None, :]), 0, r), 0),
                        axis=1).astype(jnp.int32)
    n_used = (pad_end[-1:] // r).astype(jnp.int32)
    blk_ids = jnp.arange(nb, dtype=jnp.int32)
    prev_expert = jnp.concatenate([jnp.full((1,), -1, jnp.int32), blk_expert[:-1]])
    blk_first = (blk_expert != prev_expert).astype(jnp.int32)
    blk_ord = jnp.cumsum(blk_first) - 1
    after = jnp.sum(jnp.where(in_blk, pad_end[None, :] // r, 0), axis=1)
    after_expert = jnp.sum(jnp.where(after[:, None] == blk_ids[None, :], blk_expert[None, :], 0), axis=1)
    blk_next = jnp.where(after < n_used[0], after_expert, -1).astype(jnp.int32)
    pos_flat = _slots(topi, rank, pad_start, t)[:, :TOP_K].reshape(-1)

    tail = jnp.stack([pad_end[-1], (nb * r - pad_end[-1]) // r]).astype(jnp.int32)
    xs = _dispatch(pos_flat, (pad_start + counts).astype(jnp.int32),
                   (padded - counts).astype(jnp.int32), tail, x1p, nb * r, t, d // 2)
    y = _experts(blk_expert, blk_valid, blk_first, blk_next, blk_ord.astype(jnp.int32), n_used, xs,
                 w_exp_gate_up, w_exp_down, d)
    ysh = _shared(x1p, w_sh_gate_up.astype(BF16), w_sh_down.astype(BF16), t, d)
    return _combine(pos_flat, x1, ysh, topw, ln_g, ln_b, y, t, d, alpha)


def kernel(x, w_in, conv_w, a_log, dt_bias, dn_norm_w, w_dn_branch, w_att_branch, w_o, ln1_g, ln1_b,
           w_router, router_bias, w_exp_gate_up, w_exp_down, w_sh_gate_up, w_sh_down, ln2_g, ln2_b):
    bsz, t, d = x.shape
    depth = w_in.shape[0]
    alpha = (2.0 * depth) ** 0.25
    outs = []
    for bi in range(bsz):
        xb = x[bi]
        for l in range(depth):
            y_mix = _token_mixer(xb, w_in[l], conv_w[l], a_log[l], dt_bias[l], dn_norm_w[l],
                                 w_dn_branch[l], w_att_branch[l], w_o[l], alpha)
            xb = _moe(y_mix, ln1_g[l], ln1_b[l], w_router[l], router_bias[l], w_exp_gate_up[l],
                      w_exp_down[l], w_sh_gate_up[l], w_sh_down[l], ln2_g[l], ln2_b[l], alpha)
        outs.append(xb)
    return jnp.stack(outs, 0)
```

```python
import functools

import jax
import jax.numpy as jnp
from jax import lax
from jax.experimental import pallas as pl
from jax.experimental.pallas import tpu as pltpu

F32 = jnp.float32
BF16 = jnp.bfloat16

DN_HEADS = 16
DN_HEAD_DIM = 128
DN_WIDTH = DN_HEADS * DN_HEAD_DIM
DN_CONV = 4
DN_CHUNK = 64
ATT_GROUPS = ((128, 1), (512, 4), (2048, 16))
ATT_HEADS_PER_GROUP = 8
ATT_HEAD_DIM = 128
ATT_GROUP_WIDTH = ATT_HEADS_PER_GROUP * ATT_HEAD_DIM
ATT_WIDTH = len(ATT_GROUPS) * ATT_GROUP_WIDTH
ATT_BLOCK = 128
ROPE_THETA = 10000.0
N_EXPERTS = 128
TOP_K = 8
N_GROUPS = 8
GROUP_SIZE = N_EXPERTS // N_GROUPS
TOPK_GROUPS = 4
EXPERT_FF = 384
ROUTED_SCALE = 2.5
LN_EPS = 1e-5
RMS_EPS = 1e-6
L2_EPS = 1e-6

LANES = 128
SUBLANES = 8
BF16_SUBLANES = 16
VMEM_LIMIT = 56 * 1024 * 1024

DN_PREP_ROWS = 1024
MOE_ROWS = 256
EXPERT_CHUNK = 1024
EXPERT_OUT_CHUNK = 512
NEG_BIG = -1e30


def _cparams(sem, vmem=VMEM_LIMIT):
    return pltpu.CompilerParams(dimension_semantics=sem, vmem_limit_bytes=vmem)


def _split3(a):
    hi = a.astype(BF16)
    r1 = a - hi.astype(F32)
    mid = r1.astype(BF16)
    lo = (r1 - mid.astype(F32)).astype(BF16)
    return hi, mid, lo


def _split2(a):
    hi = a.astype(BF16)
    return hi, (a - hi.astype(F32)).astype(BF16)


def _dot(a, b):
    return jnp.dot(a, b, preferred_element_type=F32)


def _dot_nt(a, b):
    return lax.dot_general(a, b, (((1,), (1,)), ((), ())), preferred_element_type=F32)


def _dot_tn(a, b):
    return lax.dot_general(a, b, (((0,), (0,)), ((), ())), preferred_element_type=F32)


def _sigmoid(x):
    return 1.0 / (1.0 + jnp.exp(-x))


def _pack_halves(lo, hi):
    lo_b = lax.bitcast_convert_type(lo.astype(BF16).astype(F32), jnp.uint32)
    hi_b = lax.bitcast_convert_type(hi.astype(BF16).astype(F32), jnp.uint32)
    return (hi_b & jnp.uint32(0xFFFF0000)) | (lo_b >> 16)


def _unpack_halves(u):
    lo = lax.bitcast_convert_type(u << 16, F32)
    hi = lax.bitcast_convert_type(u & jnp.uint32(0xFFFF0000), F32)
    return lo, hi


def _proj_kernel(x_ref, wt_ref, o_ref):
    o_ref[...] = _dot_nt(x_ref[...], wt_ref[...].astype(BF16)).astype(o_ref.dtype)


def _proj(x_b, w_t, row0, n, name):
    m, k = x_b.shape
    tm, tn = min(1024, m), min(512, n)
    return pl.pallas_call(
        _proj_kernel,
        grid=(m // tm, n // tn),
        in_specs=[pl.BlockSpec((tm, k), lambda i, j: (i, 0)),
                  pl.BlockSpec((pl.Element(tn), pl.Element(k)),
                               lambda i, j: (pl.multiple_of(row0 + j * tn, SUBLANES), 0))],
        out_specs=pl.BlockSpec((tm, tn), lambda i, j: (i, j)),
        out_shape=jax.ShapeDtypeStruct((m, n), BF16),
        compiler_params=_cparams(("parallel", "parallel")),
        name=name,
    )(x_b, w_t)


def _gates_kernel(x_ref, w_ref, prm_ref, betab_ref, gcb_ref, gct_ref, xb_ref):
    tm = x_ref.shape[0]
    x_hi, x_mid = _split2(x_ref[...])
    xb_ref[...] = x_hi
    w_hi, w_mid = _split2(w_ref[...])
    logits = _dot_nt(x_hi, w_hi) + _dot_nt(x_hi, w_mid) + _dot_nt(x_mid, w_hi)

    def softplus(v):
        return jnp.maximum(v, 0.0) + jnp.log(1.0 + jnp.exp(-jnp.abs(v)))

    prm = prm_ref[...]
    beta = _sigmoid(logits)
    g = prm[0:1, :] * softplus(logits + prm[1:2, :])

    ri = lax.broadcasted_iota(jnp.int32, (tm, tm), 0)
    ci = lax.broadcasted_iota(jnp.int32, (tm, tm), 1)
    same = (ri // DN_CHUNK) == (ci // DN_CHUNK)
    lower = jnp.where(same & (ci <= ri), 1.0, 0.0).astype(BF16)
    gc = jnp.zeros((tm, LANES), F32)
    for part in _split3(g):
        gc = gc + _dot(lower, part)
    gct_ref[...] = gc.T[DN_HEADS:2 * DN_HEADS, :]
    for h in range(DN_HEADS):
        sl = slice(h * LANES, (h + 1) * LANES)
        betab_ref[:, sl] = jnp.broadcast_to(beta[:, h:h + 1], (tm, LANES))
        gcb_ref[:, sl] = jnp.broadcast_to(gc[:, DN_HEADS + h:DN_HEADS + h + 1], (tm, LANES))


def _dn_gates(x2, w_t, col0, a_log, dt_bias):
    t, d = x2.shape
    tm = min(512, t)
    neg_a = -jnp.exp(a_log.astype(F32))
    prm = jnp.zeros((SUBLANES, LANES), F32)
    prm = prm.at[0, DN_HEADS:2 * DN_HEADS].set(neg_a).at[1, DN_HEADS:2 * DN_HEADS].set(dt_bias.astype(F32))
    return pl.pallas_call(
        _gates_kernel,
        grid=(t // tm,),
        in_specs=[pl.BlockSpec((tm, d), lambda i: (i, 0)),
                  pl.BlockSpec((LANES, d), lambda i: (col0 // LANES, 0)),
                  pl.BlockSpec((SUBLANES, LANES), lambda i: (0, 0))],
        out_specs=[pl.BlockSpec((tm, DN_WIDTH), lambda i: (i, 0)),
                   pl.BlockSpec((tm, DN_WIDTH), lambda i: (i, 0)),
                   pl.BlockSpec((DN_HEADS, tm), lambda i: (0, i)),
                   pl.BlockSpec((tm, d), lambda i: (i, 0))],
        out_shape=[jax.ShapeDtypeStruct((t, DN_WIDTH), F32),
                   jax.ShapeDtypeStruct((t, DN_WIDTH), F32),
                   jax.ShapeDtypeStruct((DN_HEADS, t), F32),
                   jax.ShapeDtypeStruct((t, d), BF16)],
        compiler_params=_cparams(("parallel",)),
        name="dn_gates",
    )(x2, w_t.astype(F32), prm)


def _dn_prep_kernel(q_ref, k_ref, v_ref, hq_ref, hk_ref, hv_ref, cq_ref, ck_ref, cv_ref,
                    betab_ref, gcb_ref, gct_ref,
                    u_ref, w_ref, qe_ref, kd_ref, intra_ref, egl_ref, cbuf):
    i = pl.program_id(1)
    rows = q_ref.shape[0]
    c = DN_CHUNK
    halo_rows = hq_ref.shape[0]

    def conv_silu(slot, x_ref, halo_ref, cw_ref):
        cbuf[slot, 0:halo_rows, :] = jnp.where(i > 0, halo_ref[...].astype(F32), 0.0)
        cbuf[slot, halo_rows:halo_rows + rows, :] = x_ref[...].astype(F32)
        cw = cw_ref[...]
        y = jnp.zeros((rows, LANES), F32)
        for j in range(DN_CONV):
            off = halo_rows - (DN_CONV - 1) + j
            y = y + cw[j:j + 1, :] * cbuf[slot, off:off + rows, :]
        return y * _sigmoid(y)

    def l2n(a):
        return a * lax.rsqrt(jnp.sum(a * a, axis=-1, keepdims=True) + L2_EPS)

    q = l2n(conv_silu(0, q_ref, hq_ref, cq_ref)) * (DN_HEAD_DIM ** -0.5)
    k = l2n(conv_silu(1, k_ref, hk_ref, ck_ref))
    v = conv_silu(2, v_ref, hv_ref, cv_ref)
    beta = betab_ref[...]
    gcol = gcb_ref[...]
    grow_all = gct_ref[...]

    ii = lax.broadcasted_iota(jnp.int32, (c, c), 0)
    jj = lax.broadcasted_iota(jnp.int32, (c, c), 1)
    eye = jnp.where(ii == jj, 1.0, 0.0)
    n_chunks = rows // c

    decays, a_mats = [], []
    kbs, egcs = [], []
    for n in range(n_chunks):
        rs = slice(n * c, (n + 1) * c)
        kc = k[rs]
        kb = kc * beta[rs]
        grow = grow_all[:, rs]
        diff = jnp.where(ii >= jj, gcol[rs, :c] - grow, 0.0)
        decay = jnp.where(ii >= jj, jnp.exp(diff), 0.0)
        kk = _dot_nt(kb.astype(BF16), kc.astype(BF16))
        a_mats.append(jnp.where(ii > jj, kk * decay, 0.0))
        decays.append(decay)
        kbs.append(kb)
        egcs.append(jnp.exp(gcol[rs]))

    invs = []
    for n in range(n_chunks):
        invs.append(eye - jnp.where((ii // 2 == jj // 2) & (ii > jj), a_mats[n], 0.0))
    s = 2
    while s < c:
        sel = ((ii // (2 * s)) == (jj // (2 * s))) & ((ii // s) > (jj // s))
        for n in range(n_chunks):
            d_b = invs[n].astype(BF16)
            x_b = jnp.where(sel, a_mats[n], 0.0).astype(BF16)
            t1 = _dot(d_b, x_b).astype(BF16)
            invs[n] = invs[n] - _dot(t1, d_b)
        s *= 2

    for n in range(n_chunks):
        rs = slice(n * c, (n + 1) * c)
        t_b = invs[n].astype(BF16)
        u_ref[rs, :] = _dot(t_b, (v[rs] * beta[rs]).astype(BF16)).astype(u_ref.dtype)
        w_ref[rs, :] = _dot(t_b, (kbs[n] * egcs[n]).astype(BF16)).astype(w_ref.dtype)
        qc = q[rs]
        kc = k[rs]
        qk = _dot_nt(qc.astype(BF16), kc.astype(BF16))
        intra = qk * decays[n]
        intra_ref[rs, :] = jnp.concatenate([intra, jnp.zeros_like(intra)], axis=1).astype(intra_ref.dtype)
        qe_ref[rs, :] = (qc * egcs[n]).astype(qe_ref.dtype)
        glast = gcol[(n + 1) * c - 1:(n + 1) * c, :]
        kd_t = (kc * jnp.exp(glast - gcol[rs])).T
        kd_ref[2 * n * c:(2 * n + 2) * c, :] = jnp.concatenate(
            [kd_t, jnp.zeros_like(kd_t)], axis=1).astype(kd_ref.dtype)
        egl_ref[n * SUBLANES:(n + 1) * SUBLANES, :] = jnp.broadcast_to(jnp.exp(glast),
                                                                       (SUBLANES, LANES))


def _dn_prep(proj, conv_w, betab, gcb, gct, t):
    rows = min(DN_PREP_ROWS, t)
    hb = BF16_SUBLANES
    nq = DN_WIDTH // LANES

    def blk(off):
        return pl.BlockSpec((rows, LANES), lambda h, i, off=off: (i, off + h))

    def halo(off):
        return pl.BlockSpec((hb, LANES),
                            lambda h, i, off=off: (jnp.maximum(i * (rows // hb) - 1, 0), off + h))

    def cw(off):
        return pl.BlockSpec((DN_CONV, LANES), lambda h, i, off=off: (0, off + h))

    per_head = pl.BlockSpec((rows, LANES), lambda h, i: (i, h))
    kd_t_spec = pl.BlockSpec((2 * rows, LANES), lambda h, i: (i, h))
    outs = [jax.ShapeDtypeStruct((t, DN_WIDTH), BF16)] * 3
    outs.append(jax.ShapeDtypeStruct((2 * t, DN_WIDTH), BF16))
    outs.append(jax.ShapeDtypeStruct((t, DN_WIDTH), BF16))
    outs.append(jax.ShapeDtypeStruct((t // DN_CHUNK * SUBLANES, DN_WIDTH), F32))
    return pl.pallas_call(
        _dn_prep_kernel,
        grid=(DN_HEADS, t // rows),
        in_specs=[blk(0), blk(nq), blk(2 * nq), halo(0), halo(nq), halo(2 * nq),
                  cw(0), cw(nq), cw(2 * nq), per_head, per_head,
                  pl.BlockSpec((None, 1, rows), lambda h, i: (h, 0, i))],
        out_specs=[per_head] * 3 + [kd_t_spec, per_head]
                  + [pl.BlockSpec((rows // DN_CHUNK * SUBLANES, LANES), lambda h, i: (i, h))],
        out_shape=outs,
        scratch_shapes=[pltpu.VMEM((3, hb + rows, LANES), F32)],
        compiler_params=_cparams(("parallel", "parallel")),
        name="dn_prep",
    )(proj, proj, proj, proj, proj, proj, conv_w, conv_w, conv_w, betab, gcb,
      gct.reshape(DN_HEADS, 1, t))


def _dn_scan_kernel(u_ref, w_ref, qe_ref, kd_ref, intra_ref, egl_ref, z_ref, nw_ref, o_ref, s_ref):
    i = pl.program_id(0)
    c = DN_CHUNK
    n_chunks = u_ref.shape[0] // c

    @pl.when(i == 0)
    def _():
        s_ref[...] = jnp.zeros_like(s_ref)

    nw = nw_ref[...]

    pw = 2 * LANES
    first = lax.broadcasted_iota(jnp.int32, (c, pw), 1) < LANES
    zeros_s = jnp.zeros((DN_HEAD_DIM, LANES), BF16)
    zeros_v = jnp.zeros((c, pw), BF16)
    first_k = lax.broadcasted_iota(jnp.int32, (DN_HEAD_DIM, pw), 1) < LANES
    zeros_k = jnp.zeros((DN_HEAD_DIM, pw), BF16)

    def chunk(n, carry):
        r0 = pl.multiple_of(n * c, c)
        k0 = pl.multiple_of(n * 2 * c, 2 * c)
        e0 = pl.multiple_of(n * SUBLANES, SUBLANES)
        for p in range(DN_HEADS // 2):
            ps = slice(p * pw, (p + 1) * pw)
            s_p = s_ref[p]
            s_b = s_p.astype(BF16)
            s_diag = jnp.concatenate(
                [jnp.concatenate([s_b[:, :LANES], zeros_s], axis=1),
                 jnp.concatenate([zeros_s, s_b[:, LANES:]], axis=1)], axis=0)
            wq = jnp.concatenate([w_ref[pl.ds(r0, c), ps], qe_ref[pl.ds(r0, c), ps]], axis=0)
            ws_qs = _dot(wq, s_diag)
            v_new = u_ref[pl.ds(r0, c), ps].astype(F32) - ws_qs[:c]
            v_b = v_new.astype(BF16)
            v_diag = jnp.concatenate([jnp.where(first, v_b, zeros_v), zeros_v,
                                      jnp.where(first, zeros_v, v_b), zeros_v], axis=0)
            kd_t = kd_ref[pl.ds(k0, 2 * c), ps]
            lhs = jnp.concatenate([intra_ref[pl.ds(r0, c), ps],
                                   jnp.where(first_k, kd_t, zeros_k),
                                   jnp.where(first_k, zeros_k, kd_t)], axis=0)
            iv = _dot(lhs, v_diag)
            o = ws_qs[c:] + iv[:c]
            eg = egl_ref[pl.ds(e0, SUBLANES), ps]
            s_dec = (s_p.reshape(DN_HEAD_DIM // SUBLANES, SUBLANES, pw) * eg[None]
                     ).reshape(DN_HEAD_DIM, pw)
            s_ref[p] = s_dec + iv[c:c + DN_HEAD_DIM] + iv[c + DN_HEAD_DIM:]
            z = z_ref[pl.ds(r0, c), ps].astype(F32)
            gate = z * _sigmoid(z)
            for hh in range(2):
                ls = slice(hh * LANES, (hh + 1) * LANES)
                o_h = o[:, ls]
                o_h = o_h * lax.rsqrt(jnp.mean(o_h * o_h, axis=-1, keepdims=True) + RMS_EPS)
                o_ref[pl.ds(r0, c), p * pw + hh * LANES:p * pw + (hh + 1) * LANES] = (
                    o_h * nw * gate[:, ls]).astype(o_ref.dtype)
        return carry

    lax.fori_loop(0, n_chunks, chunk, 0)


def _dn_scan(u, w, qe, kd, intra, egl, proj, norm_w, t):
    rows = min(512, t)
    full = pl.BlockSpec((rows, DN_WIDTH), lambda i: (i, 0))
    return pl.pallas_call(
        _dn_scan_kernel,
        grid=(t // rows,),
        in_specs=[full, full, full, pl.BlockSpec((2 * rows, DN_WIDTH), lambda i: (i, 0)), full,
                  pl.BlockSpec((rows // DN_CHUNK * SUBLANES, DN_WIDTH), lambda i: (i, 0)),
                  pl.BlockSpec((rows, DN_WIDTH), lambda i: (i, 3)),
                  pl.BlockSpec((1, LANES), lambda i: (0, 0))],
        out_specs=full,
        out_shape=jax.ShapeDtypeStruct((t, DN_WIDTH), BF16),
        scratch_shapes=[pltpu.VMEM((DN_HEADS // 2, DN_HEAD_DIM, 2 * DN_HEAD_DIM), F32)],
        compiler_params=_cparams(("arbitrary",)),
        name="dn_scan",
    )(u, w, qe, kd, intra, egl, proj, norm_w.reshape(1, LANES).astype(F32))


def _attn_kernel(*refs, dilation, span, n_earlier):
    q_ref, k_ref, v_ref, cos_ref, sin_ref = refs[:5]
    earlier = refs[5:5 + 2 * n_earlier]
    out_refs = refs[5 + 2 * n_earlier:-7]
    kbuf, vbuf, krm, vrm, qbuf, obuf, lbuf = refs[-7:]
    i = pl.program_id(1)
    bt, width = q_ref.shape
    d = dilation
    blk = ATT_BLOCK
    prev = d * blk
    per_res = bt // d
    half = ATT_HEAD_DIM // 2
    scale = ATT_HEAD_DIM ** -0.5

    heads = width // LANES

    @pl.when(i == 0)
    def _():
        krm[:, :, 0:blk, :] = jnp.zeros((heads, d, blk, LANES), F32)
        vrm[:, :, 0:blk, :] = jnp.zeros((heads, d, blk, LANES), F32)

    cos, sin = cos_ref[...], sin_ref[...]

    def rope(a):
        a = a.astype(F32)
        return a * cos + pltpu.roll(a, half, 1) * sin

    def rows(start, size):
        return pl.ds(start, size, stride=d) if d > 1 else pl.ds(start, size)

    for h in range(heads):
        ls = slice(h * LANES, (h + 1) * LANES)
        qbuf[h] = rope(q_ref[:, ls]) * scale
        kbuf[h] = rope(k_ref[:, ls])
        vbuf[h] = v_ref[:, ls].astype(F32)
        for r in range(d):
            krm[h, r, blk:blk + per_res, :] = kbuf[h, rows(r, per_res), :]
            vrm[h, r, blk:blk + per_res, :] = vbuf[h, rows(r, per_res), :]

    qi = lax.broadcasted_iota(jnp.int32, (blk, 2 * blk), 0) + blk
    ki = lax.broadcasted_iota(jnp.int32, (blk, 2 * blk), 1)
    dist = qi - ki
    band = (dist >= 0) & (dist <= span)
    band_first = band & ((ki >= blk) | (i > 0))

    for j in range(bt // prev):
        for r in range(d):
            r0 = j * prev + r
            for h in range(heads):
                qj = qbuf[h, rows(r0, blk), :].astype(BF16)
                kj = krm[h, r, j * blk:(j + 2) * blk, :].astype(BF16)
                vj = vrm[h, r, j * blk:(j + 2) * blk, :].astype(BF16)
                s = _dot_nt(qj, kj)
                s = jnp.where(band_first if j == 0 else band, s, NEG_BIG)
                m = jnp.max(s, axis=-1, keepdims=True)
                p = jnp.exp(s - m)
                l = jnp.sum(p, axis=-1, keepdims=True)
                obuf[h, rows(r0, blk), :] = _dot(p.astype(BF16), vj) / l
                lbuf[h, rows(r0, blk), :] = jnp.broadcast_to(m + jnp.log(l), (blk, LANES))

    for h in range(heads):
        ls = slice(h * LANES, (h + 1) * LANES)
        if n_earlier == 0:
            o_ref, lse_ref = out_refs
            o_ref[:, ls] = obuf[h].astype(o_ref.dtype)
            lse_ref[:, ls] = lbuf[h]
        else:
            (o_ref,) = out_refs
            outs = [earlier[2 * g][:, ls].astype(F32) for g in range(n_earlier)] + [obuf[h]]
            lses = [earlier[2 * g + 1][:, ls] for g in range(n_earlier)] + [lbuf[h]]
            m = functools.reduce(jnp.maximum, lses)
            es = [jnp.exp(l - m) for l in lses]
            num = functools.reduce(lambda a, b: a + b, [e * o for e, o in zip(es, outs)])
            den = functools.reduce(lambda a, b: a + b, es)
            o_ref[:, ls] = (num / den).astype(o_ref.dtype)
    krm[:, :, 0:blk, :] = krm[:, :, per_res:per_res + blk, :]
    vrm[:, :, 0:blk, :] = vrm[:, :, per_res:per_res + blk, :]


def _attn_group(proj, cos_t, sin_t, gi, dilation, span, t, earlier=()):
    prev = dilation * ATT_BLOCK
    bt = max(min(512, t), prev)
    width = (4 if prev <= 512 else 2) * LANES
    gw = ATT_GROUP_WIDTH
    per_g = gw // width
    q0 = gi * gw // width
    k0 = q0 + ATT_WIDTH // width
    v0 = k0 + ATT_WIDTH // width

    def cur(c0):
        return pl.BlockSpec((bt, width), lambda hg, i, c0=c0: (i, c0 + hg))

    tab = pl.BlockSpec((bt, LANES), lambda hg, i: (i, 0))
    out = pl.BlockSpec((bt, width), lambda hg, i: (i, hg))
    flat_earlier = [a for pair in earlier for a in pair]
    if earlier:
        out_specs, out_shape = [out], [jax.ShapeDtypeStruct((t, gw), BF16)]
    else:
        out_specs = [out, out]
        out_shape = [jax.ShapeDtypeStruct((t, gw), BF16), jax.ShapeDtypeStruct((t, gw), F32)]
    return pl.pallas_call(
        functools.partial(_attn_kernel, dilation=dilation, span=span, n_earlier=len(earlier)),
        grid=(per_g, t // bt),
        in_specs=[cur(q0), cur(k0), cur(v0), tab, tab] + [out] * len(flat_earlier),
        out_specs=out_specs,
        out_shape=out_shape,
        scratch_shapes=[pltpu.VMEM((width // LANES, bt, LANES), F32),
                        pltpu.VMEM((width // LANES, bt, LANES), F32),
                        pltpu.VMEM((width // LANES, dilation, ATT_BLOCK + bt // dilation, LANES), F32),
                        pltpu.VMEM((width // LANES, dilation, ATT_BLOCK + bt // dilation, LANES), F32),
                        pltpu.VMEM((width // LANES, bt, LANES), F32),
                        pltpu.VMEM((width // LANES, bt, LANES), F32),
                        pltpu.VMEM((width // LANES, bt, LANES), F32)],
        compiler_params=_cparams(("parallel", "arbitrary")),
        name=f"attn_g{gi}",
    )(proj, proj, proj, cos_t, sin_t, *flat_earlier)


def _attention(proj, cos_t, sin_t, t):
    earlier = []
    for gi, (window, dilation) in enumerate(ATT_GROUPS[:-1]):
        earlier.append(_attn_group(proj, cos_t, sin_t, gi, dilation, window // dilation, t))
    window, dilation = ATT_GROUPS[-1]
    (o_att,) = _attn_group(proj, cos_t, sin_t, len(ATT_GROUPS) - 1, dilation, window // dilation, t,
                           earlier=tuple(earlier))
    return o_att


def _mix_kernel(odn_ref, oatt_ref, wdn_ref, watt_ref, gdn_ref, gatt_ref, out_ref):
    y_dn = _dot(odn_ref[...], wdn_ref[...])
    y_att = _dot(oatt_ref[...], watt_ref[...])
    merged = (_sigmoid(gdn_ref[...].astype(F32)) * y_dn
              + _sigmoid(gatt_ref[...].astype(F32)) * y_att)
    out_ref[...] = merged.astype(out_ref.dtype)


def _mix(o_dn, o_att, w_dn, w_att, proj, t, d):
    tm = min(1024, t)
    tn = ATT_GROUP_WIDTH
    g0 = 3 * ATT_WIDTH // tn
    row_dn = pl.BlockSpec((tm, DN_WIDTH), lambda j, i: (i, 0))
    row_g = pl.BlockSpec((tm, ATT_GROUP_WIDTH), lambda j, i: (i, 0))
    return pl.pallas_call(
        _mix_kernel,
        grid=(d // tn, t // tm),
        in_specs=[row_dn, row_g,
                  pl.BlockSpec((DN_WIDTH, tn), lambda j, i: (0, j)),
                  pl.BlockSpec((ATT_GROUP_WIDTH, tn), lambda j, i: (0, j)),
                  pl.BlockSpec((tm, tn), lambda j, i: (i, g0 + j)),
                  pl.BlockSpec((tm, tn), lambda j, i: (i, g0 + d // tn + j))],
        out_specs=pl.BlockSpec((tm, tn), lambda j, i: (i, j)),
        out_shape=jax.ShapeDtypeStruct((t, d), BF16),
        compiler_params=_cparams(("parallel", "parallel")),
        name="mix",
    )(o_dn, o_att, w_dn, w_att, proj, proj)


def _layer_norm(y, g, b):
    mu = jnp.mean(y, axis=-1, keepdims=True)
    yc = y - mu
    var = jnp.mean(yc * yc, axis=-1, keepdims=True)
    return yc * lax.rsqrt(var + LN_EPS) * g + b


def _wo_kernel(m_ref, w_ref, x_ref, y_ref, *, alpha):
    y_ref[...] = alpha * x_ref[...] + _dot(m_ref[...], w_ref[...])


def _wo_residual(merged, w_o, x2, t, d, alpha):
    tm, tn = min(1024, t), min(1024, d)
    tile = pl.BlockSpec((tm, tn), lambda i, j: (i, j))
    return pl.pallas_call(
        functools.partial(_wo_kernel, alpha=alpha),
        grid=(t // tm, d // tn),
        in_specs=[pl.BlockSpec((tm, d), lambda i, j: (i, 0)),
                  pl.BlockSpec((d, tn), lambda i, j: (0, j)),
                  tile],
        out_specs=tile,
        out_shape=jax.ShapeDtypeStruct((t, d), F32),
        compiler_params=_cparams(("parallel", "parallel")),
        name="wo",
    )(merged, w_o, x2)


def _router_kernel(y_ref, g_ref, b_ref, w_ref, bias_ref,
                   x1_ref, x1p_ref, topi_ref, topw_ref, rank_ref, cnt_ref, run_ref):
    i = pl.program_id(0)
    tm, d = y_ref.shape

    @pl.when(i == 0)
    def _():
        run_ref[...] = jnp.zeros_like(run_ref)

    x1 = _layer_norm(y_ref[...], g_ref[...], b_ref[...])
    x1_ref[...] = x1
    x1p_ref[...] = _pack_halves(x1[:, :d // 2], x1[:, d // 2:])

    x_hi, x_mid = _split2(x1)
    w_hi, w_mid = _split2(w_ref[...])
    logits = _dot(x_hi, w_hi) + _dot(x_hi, w_mid) + _dot(x_mid, w_hi)
    s = _sigmoid(logits)
    sc = s + bias_ref[...]
    lane_i = lax.broadcasted_iota(jnp.int32, (tm, N_EXPERTS), 1)
    grp = lane_i // GROUP_SIZE
    lane = lane_i.astype(F32)
    neg = -jnp.inf

    def first_argmax(v):
        m = jnp.max(v, axis=-1, keepdims=True)
        idx = jnp.min(jnp.where(v == m, lane, float(N_EXPERTS)), axis=-1, keepdims=True)
        return m, idx

    gscore = []
    for gi in range(N_GROUPS):
        vg = jnp.where(grp == gi, sc, neg)
        m1, i1 = first_argmax(vg)
        m2 = jnp.max(jnp.where(lane == i1, neg, vg), axis=-1, keepdims=True)
        gscore.append(m1 + m2)
    emask = jnp.zeros((tm, N_EXPERTS), jnp.bool_)
    for gi in range(N_GROUPS):
        ahead = jnp.zeros((tm, 1), jnp.int32)
        for gj in range(N_GROUPS):
            if gj == gi:
                continue
            beats = (gscore[gj] > gscore[gi]) | ((gscore[gj] == gscore[gi]) & (gj < gi))
            ahead = ahead + beats.astype(jnp.int32)
        emask = emask | ((grp == gi) & (ahead < TOPK_GROUPS))
    masked = jnp.where(emask, sc, neg)

    sel =jnp.zeros((tm, N_EXPERTS), jnp.bool_)
    idxs, vals = [], []
    for _ in range(TOP_K):
        _, ik = first_argmax(masked)
        hit = lane == ik
        sel = sel | hit
        masked = jnp.where(hit, neg, masked)
        idxs.append(ik)
        vals.append(jnp.sum(jnp.where(hit, s, 0.0), axis=-1, keepdims=True))
    wsum = vals[0]
    for v in vals[1:]:
        wsum = wsum + v

    sel_b = jnp.where(sel, 1.0, 0.0).astype(BF16)
    strict = jnp.where(lax.broadcasted_iota(jnp.int32, (tm, tm), 1)
                       < lax.broadcasted_iota(jnp.int32, (tm, tm), 0), 1.0, 0.0).astype(BF16)
    rank_excl = run_ref[0:1, :] + _dot(strict, sel_b)
    run_new = run_ref[0:1, :] + jnp.sum(sel_b.astype(F32), axis=0, keepdims=True)
    run_ref[...] = jnp.broadcast_to(run_new, run_ref.shape)
    cnt_ref[...] = jnp.broadcast_to(run_new, cnt_ref.shape).astype(jnp.int32)

    topi = jnp.zeros((tm, N_EXPERTS), F32)
    topw = jnp.zeros((tm, N_EXPERTS), F32)
    rnk = jnp.zeros((tm, N_EXPERTS), F32)
    for kk in range(TOP_K):
        hit = lane == idxs[kk]
        rk = jnp.sum(jnp.where(hit, rank_excl, 0.0), axis=-1, keepdims=True)
        topi = jnp.where(lane == kk, idxs[kk], topi)
        topw = jnp.where(lane == kk, vals[kk] / wsum * ROUTED_SCALE, topw)
        rnk = jnp.where(lane == kk, rk, rnk)
    topi_ref[...] = topi.astype(jnp.int32)
    topw_ref[...] = topw
    rank_ref[...] = rnk.astype(jnp.int32)


def _ln_router(y, g, b, w_router, bias, t, d):
    tm = min(256, t)
    row = pl.BlockSpec((tm, N_EXPERTS), lambda i: (i, 0))
    full = pl.BlockSpec((tm, d), lambda i: (i, 0))
    vec = pl.BlockSpec((1, d), lambda i: (0, 0))
    return pl.pallas_call(
        _router_kernel,
        grid=(t // tm,),
        in_specs=[full, vec, vec,
                  pl.BlockSpec((d, N_EXPERTS), lambda i: (0, 0)),
                  pl.BlockSpec((1, N_EXPERTS), lambda i: (0, 0))],
        out_specs=[full, pl.BlockSpec((tm, d // 2), lambda i: (i, 0)), row, row, row,
                   pl.BlockSpec((SUBLANES, N_EXPERTS), lambda i: (0, 0))],
        out_shape=[jax.ShapeDtypeStruct((t, d), F32),
                   jax.ShapeDtypeStruct((t, d // 2), jnp.uint32),
                   jax.ShapeDtypeStruct((t, N_EXPERTS), jnp.int32),
                   jax.ShapeDtypeStruct((t, N_EXPERTS), F32),
                   jax.ShapeDtypeStruct((t, N_EXPERTS), jnp.int32),
                   jax.ShapeDtypeStruct((SUBLANES, N_EXPERTS), jnp.int32)],
        scratch_shapes=[pltpu.VMEM((SUBLANES, N_EXPERTS), F32)],
        compiler_params=_cparams(("arbitrary",)),
        name="ln_router",
    )(y, g.reshape(1, d).astype(F32), b.reshape(1, d).astype(F32), w_router.astype(F32),
      bias.reshape(1, N_EXPERTS).astype(F32))


def _slot_kernel(topi_ref, rank_ref, start_ref, pos_ref):
    topi = topi_ref[...]
    lane = lax.broadcasted_iota(jnp.int32, topi.shape, 1)
    start = jnp.broadcast_to(start_ref[...], topi.shape)
    pos = rank_ref[...]
    for kk in range(TOP_K):
        seg = jnp.sum(jnp.where(lane == topi[:, kk:kk + 1], start, 0.0), axis=-1, keepdims=True)
        pos = jnp.where(lane == kk, pos + seg.astype(jnp.int32), pos)
    pos_ref[...] = pos


def _slots(topi, rank, pad_start, t):
    tm = min(1024, t)
    row = pl.BlockSpec((tm, N_EXPERTS), lambda i: (i, 0))
    return pl.pallas_call(
        _slot_kernel,
        grid=(t // tm,),
        in_specs=[row, row, pl.BlockSpec((1, N_EXPERTS), lambda i: (0, 0))],
        out_specs=row,
        out_shape=jax.ShapeDtypeStruct((t, N_EXPERTS), jnp.int32),
        compiler_params=_cparams(("parallel",)),
        name="slots",
    )(topi, rank, pad_start.astype(F32).reshape(1, N_EXPERTS))


def _row_copy(src_ref, src_row, dst_ref, dst_row, sem):
    return pltpu.make_async_copy(src_ref.at[pl.ds(src_row, 1)], dst_ref.at[pl.ds(dst_row, 1)], sem)


def _dispatch_kernel(pos_ref, pad_lo_ref, pad_len_ref, tail_ref, x_ref, xs_ref, zbuf, sem, zsem):
    i = pl.program_id(0)
    tb = x_ref.shape[0]
    base = i * (tb * TOP_K)

    def zero_fill(do_start):
        def issue(cp):
            cp.start() if do_start else cp.wait()

        def per_expert(e, carry):
            lo = pad_lo_ref[e]
            n = pad_len_ref[e]
            head = jnp.minimum((SUBLANES - (lo & (SUBLANES - 1))) & (SUBLANES - 1), n)

            def row(r, c2):
                issue(pltpu.make_async_copy(zbuf.at[pl.ds(0, 1)], xs_ref.at[pl.ds(lo + r, 1)], zsem))
                return c2

            lax.fori_loop(0, head, row, 0)
            off = lo + head
            rest = n - head
            for p in (8, 16, 32, 64, 128):
                @pl.when((rest & p) != 0)
                def _(p=p, off=off):
                    issue(pltpu.make_async_copy(
                        zbuf.at[pl.ds(0, p)],
                        xs_ref.at[pl.ds(pl.multiple_of(off, SUBLANES), p)], zsem))

                off = off + (rest & p)
            return carry

        lax.fori_loop(0, N_EXPERTS, per_expert, 0)

        def tail_block(kb, carry):
            row0 = pl.multiple_of(tail_ref[0] + kb * MOE_ROWS, MOE_ROWS)
            issue(pltpu.make_async_copy(zbuf, xs_ref.at[pl.ds(row0, MOE_ROWS)], zsem))
            return carry

        lax.fori_loop(0, tail_ref[1], tail_block, 0)

    @pl.when(i == 0)
    def _():
        zbuf[...] = jnp.zeros_like(zbuf)
        zero_fill(True)

    def start(r, carry):
        for kk in range(TOP_K):
            _row_copy(x_ref, r, xs_ref, pos_ref[base + r * TOP_K + kk], sem).start(priority=kk % 2)
        return carry

    lax.fori_loop(0, tb, start, 0)
    for kk in range(TOP_K):
        pltpu.make_async_copy(x_ref, xs_ref.at[pl.ds(0, tb)], sem).wait()

    @pl.when(i == 0)
    def _():
        zero_fill(False)


def _dispatch(pos_flat, pad_lo, pad_len, tail, x1, n_slots, t, d):
    tb = min(512, t)
    return pl.pallas_call(
        _dispatch_kernel,
        grid_spec=pltpu.PrefetchScalarGridSpec(
            num_scalar_prefetch=4,
            grid=(t // tb,),
            in_specs=[pl.BlockSpec((tb, d), lambda i, *_: (i, 0))],
            out_specs=pl.BlockSpec(memory_space=pl.ANY),
            scratch_shapes=[pltpu.VMEM((MOE_ROWS, d), x1.dtype),
                            pltpu.SemaphoreType.DMA, pltpu.SemaphoreType.DMA],
        ),
        out_shape=jax.ShapeDtypeStruct((n_slots, d), x1.dtype),
        compiler_params=_cparams(("arbitrary",)),
        name="dispatch",
    )(pos_flat, pad_lo, pad_len, tail, x1)


def _expert_kernel(be_ref, nv_ref, first_ref, nxt_ref, ord_ref, nu_ref, xs_ref, wgu_hbm, wdn_hbm,
                   y_ref, gu_stage, dn_stage, gu_cache, dn_cache, act_ref, sem):
    b = pl.program_id(0)
    n_used = nu_ref[0]

    def weight_copies(e):
        return (pltpu.make_async_copy(wgu_hbm.at[e], gu_stage, sem.at[0]),
                pltpu.make_async_copy(wdn_hbm.at[e], dn_stage, sem.at[1]))

    @pl.when(b == 0)
    def _():
        act_ref[...] = jnp.zeros_like(act_ref)

    @pl.when(b > n_used)
    def _():
        y_ref[...] = jnp.zeros_like(y_ref)

    @pl.when(b <= n_used)
    def _():
        rows, hd = xs_ref.shape
        kc = min(EXPERT_CHUNK, hd)
        cur = jnp.minimum(b, n_used - 1)
        e = be_ref[cur]

        @pl.when((first_ref[cur] == 1) & (b < n_used))
        def _():
            @pl.when(b == 0)
            def _():
                for cp in weight_copies(e):
                    cp.start()

            for cp in weight_copies(e):
                cp.wait()
            for c0 in range(0, 2 * hd, kc):
                for j in range(EXPERT_FF // LANES):
                    gu_cache[c0:c0 + kc, 2 * j * LANES:(2 * j + 1) * LANES] = (
                        gu_stage[c0:c0 + kc, j * LANES:(j + 1) * LANES].astype(BF16))
                    gu_cache[c0:c0 + kc, (2 * j + 1) * LANES:(2 * j + 2) * LANES] = (
                        gu_stage[c0:c0 + kc, EXPERT_FF + j * LANES:EXPERT_FF + (j + 1) * LANES]
                        .astype(BF16))
                dn_cache[ord_ref[cur] % 2, :, c0:c0 + kc] = dn_stage[:, c0:c0 + kc].astype(BF16)

            @pl.when(nxt_ref[cur] >= 0)
            def _():
                for cp in weight_copies(nxt_ref[cur]):
                    cp.start()

        prev_slot = ord_ref[jnp.maximum(b - 1, 0)] % 2
        act_prev = act_ref[...]
        nc = min(EXPERT_OUT_CHUNK, hd)
        for c0 in range(0, hd, nc):
            y_ref[:, c0:c0 + nc] = _pack_halves(
                _dot(act_prev, dn_cache[prev_slot, :, c0:c0 + nc]),
                _dot(act_prev, dn_cache[prev_slot, :, hd + c0:hd + c0 + nc]))

        rid = lax.broadcasted_iota(jnp.int32, (rows, 1), 0)
        valid = rid < nv_ref[cur]
        lo, hi = _unpack_halves(xs_ref[...])
        x_lo = jnp.where(valid, lo, 0.0).astype(BF16)
        x_hi = jnp.where(valid, hi, 0.0).astype(BF16)
        for j in range(EXPERT_FF // LANES):
            cs = slice(2 * j * LANES, (2 * j + 2) * LANES)
            hgu = _dot(x_lo, gu_cache[:hd, cs]) + _dot(x_hi, gu_cache[hd:, cs])
            gate, up = hgu[:, :LANES], hgu[:, LANES:]
            act_ref[:, j * LANES:(j + 1) * LANES] = (gate * _sigmoid(gate) * up).astype(BF16)


def _experts(blk_expert, blk_valid, blk_first, blk_next, blk_ord, n_used, xs, w_gu, w_dn, d):
    n_slots = xs.shape[0]
    nb = n_slots // MOE_ROWS

    def in_map(b, be, nv, fi, nx, od, nu):
        return (jnp.minimum(b, nu[0] - 1), 0)

    def out_map(b, be, nv, fi, nx, od, nu):
        return (jnp.maximum(b - 1, 0), 0)

    return pl.pallas_call(
        _expert_kernel,
        grid_spec=pltpu.PrefetchScalarGridSpec(
            num_scalar_prefetch=6,
            grid=(nb + 1,),
            in_specs=[pl.BlockSpec((MOE_ROWS, d // 2), in_map),
                      pl.BlockSpec(memory_space=pl.ANY),
                      pl.BlockSpec(memory_space=pl.ANY)],
            out_specs=pl.BlockSpec((MOE_ROWS, d // 2), out_map),
            scratch_shapes=[pltpu.VMEM((d, 2 * EXPERT_FF), w_gu.dtype),
                            pltpu.VMEM((EXPERT_FF, d), w_dn.dtype),
                            pltpu.VMEM((d, 2 * EXPERT_FF), BF16),
                            pltpu.VMEM((2, EXPERT_FF, d), BF16),
                            pltpu.VMEM((MOE_ROWS, EXPERT_FF), BF16),
                            pltpu.SemaphoreType.DMA((2,))],
        ),
        out_shape=jax.ShapeDtypeStruct((n_slots, d // 2), jnp.uint32),
        compiler_params=_cparams(("arbitrary",)),
        name="experts",
    )(blk_expert, blk_valid, blk_first, blk_next, blk_ord, n_used, xs, w_gu, w_dn)


def _shared_kernel(x_ref, wgu_ref, wdn_ref, y_ref, act_ref):
    i = pl.program_id(0)

    @pl.when(i == 0)
    def _():
        act_ref[...] = jnp.zeros_like(act_ref)

    y_ref[...] = _dot(act_ref[...], wdn_ref[...]).astype(y_ref.dtype)
    hd = x_ref.shape[1]
    lo, hi = _unpack_halves(x_ref[...])
    hgu = _dot(lo.astype(BF16), wgu_ref[:hd, :]) + _dot(hi.astype(BF16), wgu_ref[hd:, :])
    gate, up = hgu[:, :EXPERT_FF], hgu[:, EXPERT_FF:]
    act_ref[...] = (gate * _sigmoid(gate) * up).astype(BF16)


def _shared(x1p, w_gu, w_dn, t, d):
    tm = min(512, t)
    n = t // tm
    return pl.pallas_call(
        _shared_kernel,
        grid=(n + 1,),
        in_specs=[pl.BlockSpec((tm, d // 2), lambda i: (jnp.minimum(i, n - 1), 0)),
                  pl.BlockSpec((d, 2 * EXPERT_FF), lambda i: (0, 0)),
                  pl.BlockSpec((EXPERT_FF, d), lambda i: (0, 0))],
        out_specs=pl.BlockSpec((tm, d), lambda i: (jnp.maximum(i - 1, 0), 0)),
        out_shape=jax.ShapeDtypeStruct((t, d), BF16),
        scratch_shapes=[pltpu.VMEM((tm, EXPERT_FF), BF16)],
        compiler_params=_cparams(("arbitrary",)),
        name="shared",
    )(x1p, w_gu, w_dn)


def _combine_kernel(pos_ref, x_ref, ysh_ref, topw_ref, g_ref, b_ref, y_hbm, out_ref, buf_ref, sem,
                    *, alpha):
    i = pl.program_id(0)
    n = pl.num_programs(0)
    tb = x_ref.shape[0]

    def gather(tile, slot):
        base = tile * (tb * TOP_K)

        def body(r, carry):
            for kk in range(TOP_K):
                _row_copy(y_hbm, pos_ref[base + r * TOP_K + kk], buf_ref.at[slot, kk], r,
                          sem.at[slot]).start(priority=kk % 2)
            return carry

        lax.fori_loop(0, tb, body, 0)

    @pl.when(i == 0)
    def _():
        gather(0, 0)

    @pl.when(i + 1 < n)
    def _():
        gather(i + 1, (i + 1) % 2)

    slot = i % 2
    for kk in range(TOP_K):
        pltpu.make_async_copy(y_hbm.at[pl.ds(0, tb)], buf_ref.at[slot, kk], sem.at[slot]).wait()
    topw = topw_ref[...]
    hd = x_ref.shape[1] // 2
    acc_lo = alpha * x_ref[:, :hd] + ysh_ref[:, :hd].astype(F32)
    acc_hi = alpha * x_ref[:, hd:] + ysh_ref[:, hd:].astype(F32)
    for kk in range(TOP_K):
        lo, hi = _unpack_halves(buf_ref[slot, kk])
        acc_lo = acc_lo + topw[:, kk:kk + 1] * lo
        acc_hi = acc_hi + topw[:, kk:kk + 1] * hi
    inv_d = 1.0 / (2 * hd)
    mu = (jnp.sum(acc_lo, axis=-1, keepdims=True) + jnp.sum(acc_hi, axis=-1, keepdims=True)) * inv_d
    c_lo, c_hi = acc_lo - mu, acc_hi - mu
    var = (jnp.sum(c_lo * c_lo, axis=-1, keepdims=True)
           + jnp.sum(c_hi * c_hi, axis=-1, keepdims=True)) * inv_d
    rstd = lax.rsqrt(var + LN_EPS)
    out_ref[:, :hd] = c_lo * rstd * g_ref[:, :hd] + b_ref[:, :hd]
    out_ref[:, hd:] = c_hi * rstd * g_ref[:, hd:] + b_ref[:, hd:]


def _combine(pos_flat, x1, ysh, topw, g, b, y, t, d, alpha):
    tb = min(128, t)
    row = pl.BlockSpec((tb, d), lambda i, pos: (i, 0))
    vec = pl.BlockSpec((1, d), lambda i, pos: (0, 0))
    return pl.pallas_call(
        functools.partial(_combine_kernel, alpha=alpha),
        grid_spec=pltpu.PrefetchScalarGridSpec(
            num_scalar_prefetch=1,
            grid=(t // tb,),
            in_specs=[row, row, pl.BlockSpec((tb, N_EXPERTS), lambda i, pos: (i, 0)), vec, vec,
                      pl.BlockSpec(memory_space=pl.ANY)],
            out_specs=row,
            scratch_shapes=[pltpu.VMEM((2, TOP_K, tb, d // 2), jnp.uint32),
                            pltpu.SemaphoreType.DMA((2,))],
        ),
        out_shape=jax.ShapeDtypeStruct((t, d), F32),
        compiler_params=_cparams(("arbitrary",)),
        name="combine",
    )(pos_flat, x1, ysh, topw, g.reshape(1, d).astype(F32), b.reshape(1, d).astype(F32), y)


def _rope_tables(t):
    half = ATT_HEAD_DIM // 2
    inv_freq = ROPE_THETA ** (-jnp.arange(half, dtype=F32) / half)
    ang = jnp.arange(t, dtype=F32)[:, None] * inv_freq[None, :]
    cos, sin = jnp.cos(ang), jnp.sin(ang)
    return jnp.concatenate([cos, cos], -1), jnp.concatenate([-sin, sin], -1)


def _token_mixer(x2, w_in, conv_w, a_log, dt_bias, dn_norm_w, w_dn_branch, w_att_branch, w_o,
                 alpha):
    t, d = x2.shape
    n_main = 4 * DN_WIDTH
    n_ba = 2 * DN_HEADS
    w_t = w_in.T
    betab, gcb, gct, x_b = _dn_gates(x2, w_t, n_main, a_log, dt_bias)
    proj_dn = _proj(x_b, w_t, 0, n_main, "proj_dn")
    proj_att = _proj(x_b, w_t, n_main + n_ba, w_t.shape[0] - n_main - n_ba, "proj_att")
    u, w, qe, kd, intra, egl = _dn_prep(proj_dn, conv_w.astype(F32), betab, gcb, gct, t)
    o_dn = _dn_scan(u, w, qe, kd, intra, egl, proj_dn, dn_norm_w, t)

    cos_t, sin_t = _rope_tables(t)
    o_att = _attention(proj_att, cos_t, sin_t, t)
    merged = _mix(o_dn, o_att, w_dn_branch.astype(BF16), w_att_branch.astype(BF16), proj_att, t, d)
    return _wo_residual(merged, w_o.astype(BF16), x2, t, d, alpha)


def _moe(y_mix, ln1_g, ln1_b, w_router, router_bias, w_exp_gate_up, w_exp_down, w_sh_gate_up,
         w_sh_down, ln_g, ln_b, alpha):
    t, d = y_mix.shape
    r = MOE_ROWS
    x1, x1p, topi, topw, rank, cnt = _ln_router(y_mix, ln1_g, ln1_b, w_router, router_bias, t, d)
    counts = cnt[0]
    padded = (counts + r - 1) // r * r
    pad_end = jnp.cumsum(padded)
    pad_start = pad_end - padded
    nb = (t * TOP_K) // r + N_EXPERTS
    blk_row0 = jnp.arange(nb, dtype=jnp.int32) * r
    in_blk = (blk_row0[:, None] >= pad_start[None, :]) & (blk_row0[:, None] < pad_end[None, :])
    expert_ids = jnp.arange(N_EXPERTS, dtype=jnp.int32)
    blk_expert = jnp.where(blk_row0 < pad_end[-1],
                           jnp.sum(jnp.where(in_blk, expert_ids[None, :], 0), axis=1),
                           N_EXPERTS - 1).astype(jnp.int32)
    blk_valid = jnp.sum(jnp.where(in_blk, jnp.clip(counts[None, :] - (blk_row0[:, None] - pad_start[None, :]), 0, r), 0),
                        axis=1).astype(jnp.int32)
    n_used = (pad_end[-1:] // r).astype(jnp.int32)
    blk_ids = jnp.arange(nb, dtype=jnp.int32)
    prev_expert = jnp.concatenate([jnp.full((1,), -1, jnp.int32), blk_expert[:-1]])
    blk_first = (blk_expert != prev_expert).astype(jnp.int32)
    blk_ord = jnp.cumsum(blk_first) - 1
    after = jnp.sum(jnp.where(in_blk, pad_end[None, :] // r, 0), axis=1)
    after_expert = jnp.sum(jnp.where(after[:, None] == blk_ids[None, :], blk_expert[None, :], 0), axis=1)
    blk_next = jnp.where(after < n_used[0], after_expert, -1).astype(jnp.int32)
    pos_flat = _slots(topi, rank, pad_start, t)[:, :TOP_K].reshape(-1)

    tail = jnp.stack([pad_end[-1], (nb * r - pad_end[-1]) // r]).astype(jnp.int32)
    xs = _dispatch(pos_flat, (pad_start + counts).astype(jnp.int32),
                   (padded - counts).astype(jnp.int32), tail, x1p, nb * r, t, d // 2)
    y = _experts(blk_expert, blk_valid, blk_first, blk_next, blk_ord.astype(jnp.int32), n_used, xs,
                 w_exp_gate_up, w_exp_down, d)
    ysh = _shared(x1p, w_sh_gate_up.astype(BF16), w_sh_down.astype(BF16), t, d)
    return _combine(pos_flat, x1, ysh, topw, ln_g, ln_b, y, t, d, alpha)


def kernel(x, w_in, conv_w, a_log, dt_bias, dn_norm_w, w_dn_branch, w_att_branch, w_o, ln1_g, ln1_b,
           w_router, router_bias, w_exp_gate_up, w_exp_down, w_sh_gate_up, w_sh_down, ln2_g, ln2_b):
    bsz, t, d = x.shape
    depth = w_in.shape[0]
    alpha = (2.0 * depth) ** 0.25
    outs = []
    for bi in range(bsz):
        xb = x[bi]
        for l in range(depth):
            y_mix = _token_mixer(xb, w_in[l], conv_w[l], a_log[l], dt_bias[l], dn_norm_w[l],
                                 w_dn_branch[l], w_att_branch[l], w_o[l], alpha)
            xb = _moe(y_mix, ln1_g[l], ln1_b[l], w_router[l], router_bias[l], w_exp_gate_up[l],
                      w_exp_down[l], w_sh_gate_up[l], w_sh_down[l], ln2_g[l], ln2_b[l], alpha)
        outs.append(xb)
    return jnp.stack(outs, 0)
```
